```python
import math, functools
import jax, jax.numpy as jnp
from jax import lax
import numpy as np

D_MODEL = 1024
BATCH = 8
SEQ = 2048
DEPTH = 1
DEC_BATCH = 32
DEC_SEQ = 8
PAST_LEN = 16384
PAGE_SIZE = 128

H_A = 4
DK_A = 128
DV_A = 128
CONV_A = 4
CHUNK_A = 64
H_B = 4
DH_B = 128
H_IDX = 8
D_IDX = 64
TOPK_MAX = 256
Q_BLOCK = 128
N_BUCKETS = 32
MAX_DIST = 2048
D_FF = 2816
CONV_F = 3
EPS = 1e-6

QKV_A = H_A * (2 * DK_A + DV_A)
IN_SIZES = (QKV_A, H_A, H_A, H_A * DV_A, 3 * H_B * DH_B, H_IDX * D_IDX, D_IDX, H_IDX, D_MODEL, D_MODEL)
IN_TOTAL = sum(IN_SIZES)

kernel_name = 'hybrid_deltanet_dsa_convffn_step'


def rms_norm(x, g):
    xf = x.astype(jnp.float32)
    y = xf * lax.rsqrt(jnp.mean(xf * xf, axis=-1, keepdims=True) + EPS)
    return (y * g.astype(jnp.float32)).astype(x.dtype)


def l2_normalize(x):
    xf = x.astype(jnp.float32)
    return (xf * lax.rsqrt(jnp.sum(xf * xf, axis=-1, keepdims=True) + EPS)).astype(x.dtype)


def split_cols(h):
    parts, start = [], 0
    for size in IN_SIZES:
        parts.append(h[..., start:start + size])
        start += size
    return parts


def causal_dwconv(x, buf, w):
    width, t = w.shape[0], x.shape[1]
    xp = jnp.concatenate([buf.astype(x.dtype), x], axis=1)
    out = xp[:, 0:t] * w[0]
    for j in range(1, width):
        out = out + xp[:, j:j + t] * w[j]
    return out, xp[:, t:]


def gated_delta_rule(q, k, v, g, beta, s0):
    f32 = jnp.float32
    b, t, h, dk = q.shape
    dv = v.shape[-1]
    c = math.gcd(t, CHUNK_A)
    n = t // c

    def blocks(a):
        a = a.astype(f32).reshape((b, n, c, h) + a.shape[3:])
        return jnp.moveaxis(a, 3, 2).swapaxes(0, 1)

    qc, kc, vc, gc, bc = blocks(q), blocks(k), blocks(v), blocks(g), blocks(beta)
    gcum = jnp.cumsum(gc, axis=-1)
    tril = jnp.tril(jnp.ones((c, c), bool))
    strict = jnp.tril(jnp.ones((c, c), bool), -1)
    gamma = jnp.exp(jnp.where(tril, gcum[..., :, None] - gcum[..., None, :], -jnp.inf))
    kb = kc * bc[..., None]
    a_mat = jnp.where(strict, jnp.einsum('nbhid,nbhjd->nbhij', kb, kc) * gamma, 0.0)
    rhs = jnp.concatenate([kb * jnp.exp(gcum)[..., None], vc * bc[..., None]], axis=-1)
    sol = lax.linalg.triangular_solve(jnp.eye(c, dtype=f32) + a_mat, rhs,
                                      left_side=True, lower=True, unit_diagonal=True)
    w_mat, u_mat = sol[..., :dk], sol[..., dk:]
    attn = jnp.where(tril, jnp.einsum('nbhid,nbhjd->nbhij', qc, kc) * gamma, 0.0)
    g_last = gcum[..., -1]
    q_dec = qc * jnp.exp(gcum)[..., None]
    k_dec = kc * jnp.exp(g_last[..., None] - gcum)[..., None]

    def step(s, xs):
        w_c, u_c, q_c, attn_c, k_c, gl = xs
        v_new = u_c - jnp.einsum('bhcd,bhde->bhce', w_c, s)
        o = jnp.einsum('bhcd,bhde->bhce', q_c, s) + jnp.einsum('bhij,bhje->bhie', attn_c, v_new)
        s = s * jnp.exp(gl)[..., None, None] + jnp.einsum('bhcd,bhce->bhde', k_c, v_new)
        return s, o

    s_fin, o = lax.scan(step, s0.astype(f32), (w_mat, u_mat, q_dec, attn, k_dec, g_last))
    o = jnp.moveaxis(o.swapaxes(0, 1), 2, 3).reshape(b, t, h, dv)
    return o.astype(q.dtype), s_fin.astype(s0.dtype)


def rel_bucket(dist):
    nd = jnp.maximum(dist, 0)
    max_exact = N_BUCKETS // 2
    nf = jnp.maximum(nd, max_exact).astype(jnp.float32)
    large = max_exact + (jnp.log(nf / max_exact) / math.log(MAX_DIST / max_exact)
                         * (N_BUCKETS - max_exact)).astype(jnp.int32)
    large = jnp.minimum(large, N_BUCKETS - 1)
    return jnp.where(nd < max_exact, nd, large)


def dsa_attend(q, q_idx, w_idx, q_pos, k_idx_keys, gather_kv, rel_bias):
    f32 = jnp.float32
    n_keys = k_idx_keys.shape[1]
    top_k = min(TOPK_MAX, n_keys // 4)
    key_pos = jnp.arange(n_keys, dtype=jnp.int32)
    dots = jnp.einsum('bthd,bsd->bths', q_idx, k_idx_keys) * (D_IDX ** -0.5)
    score = jnp.einsum('bths,bth->bts', jax.nn.relu(dots), w_idx).astype(f32)
    admissible = key_pos[None, :] <= q_pos[:, None]
    score = jnp.where(admissible[None], score, -jnp.inf)
    _, idx = lax.top_k(score, top_k)
    k_sel, v_sel = gather_kv(idx)
    dist = q_pos[None, :, None] - idx
    bias = jnp.moveaxis(rel_bias[rel_bucket(dist)], 3, 2)
    logits = jnp.einsum('bthd,btkhd->bthk', q, k_sel).astype(f32) * (DH_B ** -0.5) + bias.astype(f32)
    logits = jnp.where((dist >= 0)[:, :, None, :], logits, -jnp.inf)
    p = jax.nn.softmax(logits, axis=-1)
    return jnp.einsum('bthk,btkhd->bthd', p.astype(v_sel.dtype), v_sel)


def prompt_attention(qb, kb, vb, q_idx, k_idx, w_idx, rel_bias):
    b, t = qb.shape[:2]
    nb = t // Q_BLOCK
    bidx = jnp.arange(b)[:, None, None]

    def gather(idx):
        return kb[bidx, idx], vb[bidx, idx]

    def to_blocks(a):
        return jnp.moveaxis(a.reshape((b, nb, Q_BLOCK) + a.shape[2:]), 1, 0)

    def one_block(xs):
        q_blk, qi_blk, wi_blk, i = xs
        q_pos = i * Q_BLOCK + jnp.arange(Q_BLOCK, dtype=jnp.int32)
        return dsa_attend(q_blk, qi_blk, wi_blk, q_pos, k_idx, gather, rel_bias)

    out = lax.map(one_block, (to_blocks(qb), to_blocks(q_idx), to_blocks(w_idx), jnp.arange(nb, dtype=jnp.int32)))
    return jnp.moveaxis(out, 0, 1).reshape(b, t, H_B, DH_B)


def sample_attention(qb, kb, vb, q_idx, k_idx, w_idx, cache_k, cache_v, cache_idx_k, page_table, layer, rel_bias):
    b, t = qb.shape[:2]
    past = page_table.shape[1] * PAGE_SIZE
    k_idx_past = cache_idx_k[layer, page_table].reshape(b, past, D_IDX)
    k_idx_keys = jnp.concatenate([k_idx_past.astype(k_idx.dtype), k_idx], axis=1)
    bidx = jnp.arange(b)[:, None, None]

    def gather(idx):
        in_past = (idx < past)[..., None, None]
        pidx = jnp.minimum(idx, past - 1)
        phys = page_table[bidx, pidx // PAGE_SIZE]
        off = pidx % PAGE_SIZE
        nidx = jnp.clip(idx - past, 0, t - 1)
        k_sel = jnp.where(in_past, cache_k[layer, phys, off].astype(kb.dtype), kb[bidx, nidx])
        v_sel = jnp.where(in_past, cache_v[layer, phys, off].astype(vb.dtype), vb[bidx, nidx])
        return k_sel, v_sel

    q_pos = past + jnp.arange(t, dtype=jnp.int32)
    return dsa_attend(qb, q_idx, w_idx, q_pos, k_idx_keys, gather, rel_bias)


def decoder_layer(x, conv_a_buf, rec_a, ffn_buf, attend, w_in, conv_a_w, a_log, dt_bias, norm_a_g,
                  w_proj_a, w_proj_b, w_out, norm1_g, norm2_g, w_up, conv_f_w, conv_f_b, w_down):
    f32 = jnp.float32
    b, t, _ = x.shape
    xn = rms_norm(x, norm1_g)
    (qkv_a, beta_raw, a_raw, z_a, qkv_b, q_idx, k_idx, w_idx, gate_a, gate_b) = split_cols(xn @ w_in)
    conv_qkv, conv_a_new = causal_dwconv(qkv_a, conv_a_buf, conv_a_w)
    conv_qkv = jax.nn.silu(conv_qkv)
    hk = H_A * DK_A
    qa = l2_normalize(conv_qkv[..., :hk].reshape(b, t, H_A, DK_A)) * (DK_A ** -0.5)
    ka = l2_normalize(conv_qkv[..., hk:2 * hk].reshape(b, t, H_A, DK_A))
    va = conv_qkv[..., 2 * hk:].reshape(b, t, H_A, DV_A)
    beta = jax.nn.sigmoid(beta_raw)
    g = -jnp.exp(a_log.astype(f32)) * jax.nn.softplus(a_raw.astype(f32) + dt_bias.astype(f32))
    o_a, rec_a_new = gated_delta_rule(qa, ka, va, g, beta, rec_a)
    o_a = rms_norm(o_a, norm_a_g) * jax.nn.silu(z_a.reshape(b, t, H_A, DV_A))
    hb = H_B * DH_B
    qb = qkv_b[..., :hb].reshape(b, t, H_B, DH_B)
    kb = qkv_b[..., hb:2 * hb].reshape(b, t, H_B, DH_B)
    vb = qkv_b[..., 2 * hb:].reshape(b, t, H_B, DH_B)
    o_b = attend(qb, kb, vb, q_idx.reshape(b, t, H_IDX, D_IDX), k_idx, w_idx * (H_IDX ** -0.5))
    merged = (jax.nn.sigmoid(gate_a) * (o_a.reshape(b, t, -1) @ w_proj_a)
              + jax.nn.sigmoid(gate_b) * (o_b.reshape(b, t, -1) @ w_proj_b))
    x = x + merged @ w_out
    up, ffn_new = causal_dwconv(rms_norm(x, norm2_g) @ w_up, ffn_buf, conv_f_w)
    gate_f, val_f = jnp.split(up + conv_f_b, 2, axis=-1)
    x = x + (jax.nn.silu(gate_f) * val_f) @ w_down
    return x, (kb, vb, k_idx, conv_a_new, rec_a_new, ffn_new)


def setup_inputs(seed: int = 0) -> dict:
    key = jax.random.key(seed)
    ks = jax.random.split(key, 32)
    f32 = jnp.float32
    n_pages = PAST_LEN // PAGE_SIZE
    n_used = DEC_BATCH * n_pages
    n_pool = n_used + n_used // 4

    def nrm(k, shape, scale):
        return jax.random.normal(k, shape, f32) * scale

    page_table = jax.random.permutation(ks[0], n_pool)[:n_used].reshape(DEC_BATCH, n_pages).astype(jnp.int32)
    dt = jnp.exp(jax.random.uniform(ks[13], (DEPTH, H_A), f32, math.log(1e-3), math.log(1e-1)))
    return {
        'x_prompt': nrm(ks[1], (BATCH, SEQ, D_MODEL), 1.0),
        'x_sample': nrm(ks[2], (DEC_BATCH, DEC_SEQ, D_MODEL), 1.0),
        'cache_k': nrm(ks[3], (DEPTH, n_pool, PAGE_SIZE, H_B, DH_B), 1.0),
        'cache_v': nrm(ks[4], (DEPTH, n_pool, PAGE_SIZE, H_B, DH_B), 1.0),
        'cache_idx_k': nrm(ks[5], (DEPTH, n_pool, PAGE_SIZE, D_IDX), 1.0),
        'state_a_conv': nrm(ks[6], (DEPTH, DEC_BATCH, CONV_A - 1, QKV_A), 1.0),
        'state_a_rec': nrm(ks[7], (DEPTH, DEC_BATCH, H_A, DK_A, DV_A), 0.1),
        'state_ffn_conv': nrm(ks[8], (DEPTH, DEC_BATCH, CONV_F - 1, 2 * D_FF), 1.0),
        'page_table': page_table,
        'w_in': nrm(ks[9], (DEPTH, D_MODEL, IN_TOTAL), D_MODEL ** -0.5),
        'conv_a_w': nrm(ks[10], (DEPTH, CONV_A, QKV_A), CONV_A ** -0.5),
        'a_log': jnp.log(jax.random.uniform(ks[11], (DEPTH, H_A), f32, 1.0, 16.0)),
        'dt_bias': dt + jnp.log(-jnp.expm1(-dt)),
        'norm_a_g': 1.0 + nrm(ks[12], (DEPTH, DV_A), 0.02),
        'w_proj_a': nrm(ks[14], (DEPTH, H_A * DV_A, D_MODEL), (H_A * DV_A) ** -0.5),
        'w_proj_b': nrm(ks[15], (DEPTH, H_B * DH_B, D_MODEL), (H_B * DH_B) ** -0.5),
        'w_out': nrm(ks[16], (DEPTH, D_MODEL, D_MODEL), D_MODEL ** -0.5),
        'rel_bias': nrm(ks[17], (N_BUCKETS, H_B), 0.5),
        'norm1_g': 1.0 + nrm(ks[18], (DEPTH, D_MODEL), 0.02),
        'norm2_g': 1.0 + nrm(ks[19], (DEPTH, D_MODEL), 0.02),
        'w_up': nrm(ks[20], (DEPTH, D_MODEL, 2 * D_FF), D_MODEL ** -0.5),
        'conv_f_w': nrm(ks[21], (DEPTH, CONV_F, 2 * D_FF), CONV_F ** -0.5),
        'conv_f_b': nrm(ks[22], (DEPTH, 2 * D_FF), 0.02),
        'w_down': nrm(ks[23], (DEPTH, D_FF, D_MODEL), D_FF ** -0.5),
        'final_g': 1.0 + nrm(ks[24], (D_MODEL,), 0.02),
    }


def reference(x_prompt, x_sample, cache_k, cache_v, cache_idx_k, state_a_conv, state_a_rec, state_ffn_conv,
              page_table, w_in, conv_a_w, a_log, dt_bias, norm_a_g, w_proj_a, w_proj_b, w_out, rel_bias,
              norm1_g, norm2_g, w_up, conv_f_w, conv_f_b, w_down, final_g):
    bp = x_prompt.shape[0]
    dtp = x_prompt.dtype
    attend_prompt = functools.partial(prompt_attention, rel_bias=rel_bias)
    xp, xs = x_prompt, x_sample
    new_p, new_s = [], []
    for l in range(DEPTH):
        lw = (w_in[l], conv_a_w[l], a_log[l], dt_bias[l], norm_a_g[l], w_proj_a[l], w_proj_b[l], w_out[l],
              norm1_g[l], norm2_g[l], w_up[l], conv_f_w[l], conv_f_b[l], w_down[l])
        attend_sample = functools.partial(sample_attention, cache_k=cache_k, cache_v=cache_v,
                                          cache_idx_k=cache_idx_k, page_table=page_table, layer=l,
                                          rel_bias=rel_bias)
        xp, st_p = decoder_layer(xp, jnp.zeros((bp, CONV_A - 1, QKV_A), dtp),
                                 jnp.zeros((bp, H_A, DK_A, DV_A), dtp),
                                 jnp.zeros((bp, CONV_F - 1, 2 * D_FF), dtp), attend_prompt, *lw)
        new_p.append(st_p)
        xs, st_s = decoder_layer(xs, state_a_conv[l], state_a_rec[l], state_ffn_conv[l], attend_sample, *lw)
        new_s.append(st_s)
    y_prompt = rms_norm(xp, final_g)
    y_sample = rms_norm(xs, final_g)
    k_p, v_p, kidx_p, conv_a_p, rec_a_p, ffn_conv_p = [jnp.stack(a) for a in zip(*new_p)]
    k_s, v_s, kidx_s, conv_a_s, rec_a_s, ffn_conv_s = [jnp.stack(a) for a in zip(*new_s)]
    return (y_prompt, y_sample, k_p, v_p, kidx_p, conv_a_p, rec_a_p, ffn_conv_p,
            k_s, v_s, kidx_s, conv_a_s, rec_a_s, ffn_conv_s)
```

```python
import functools
import math

import jax
import jax.numpy as jnp
from jax import lax
from jax.experimental import pallas as pl
from jax.experimental.pallas import tpu as pltpu

F32 = jnp.float32
BF16 = jnp.bfloat16
HI = lax.Precision.HIGHEST

H_A, DK_A, DV_A, CONV_A, CHUNK_A = 4, 128, 128, 4, 64
H_B, DH_B, H_IDX, D_IDX = 4, 128, 8, 64
TOPK_MAX, N_BUCKETS, MAX_DIST = 256, 32, 2048
CONV_F, EPS, PAGE_SIZE = 3, 1e-6, 128
QKV_A = H_A * (2 * DK_A + DV_A)
HB = H_B * DH_B

LANES = 128
SUBLANES = 8
VMEM_LIMIT = 56 * 1024 * 1024

MISC_KIDX, MISC_BETA, MISC_A, MISC_WIDX = 0, D_IDX, D_IDX + H_A, D_IDX + 2 * H_A

INT_MIN = -2 ** 31
NEG_INF = float("-inf")


def _cparams(n_axes):
    return pltpu.CompilerParams(dimension_semantics=("arbitrary",) * n_axes,
                                vmem_limit_bytes=VMEM_LIMIT)


def _const_spec(shape):
    nd = len(shape)
    return pl.BlockSpec(shape, lambda *_: (0,) * nd, pipeline_mode=pl.Buffered(1))


def _silu(x):
    return x * jax.nn.sigmoid(x)


def _dot(a, b, precision=None):
    return jnp.dot(a, b, preferred_element_type=F32, precision=precision)


def _dot_nt(a, b, precision=None):
    return lax.dot_general(a, b, (((1,), (1,)), ((), ())), preferred_element_type=F32,
                           precision=precision)


def _dot_tn(a, b, precision=None):
    return lax.dot_general(a, b, (((0,), (0,)), ((), ())), preferred_element_type=F32,
                           precision=precision)


def _rms(x, g):
    return x * lax.rsqrt(jnp.mean(x * x, axis=-1, keepdims=True) + EPS) * g


def _shift_rows(x, k, prev):
    n = prev.shape[0]
    out = pltpu.roll(x, k, 0)
    row = lax.broadcasted_iota(jnp.int32, x.shape, 0)
    for r in range(k):
        out = jnp.where(row == r, prev[n - k + r:n - k + r + 1, :], out)
    return out


PROJ_WIDTHS = (QKV_A, H_A * DV_A, HB, HB, HB, H_IDX * D_IDX, None, None, LANES)


def _proj_kernel(x_ref, g_ref, w_ref, qkva_ref, za_ref, qb_ref, kb_ref, vb_ref, qidx_ref,
                 ga_ref, gb_ref, misc_ref, kidx_ref):
    xn = _rms(x_ref[...], g_ref[...]).astype(BF16)
    off = 0
    for o_ref in (qkva_ref, za_ref, qb_ref, kb_ref, vb_ref, qidx_ref, ga_ref, gb_ref, misc_ref):
        wd = o_ref.shape[-1]
        o_ref[...] = _dot(xn, w_ref[:, off:off + wd])
        off += wd
    kidx_ref[...] = misc_ref[:, MISC_KIDX:MISC_KIDX + D_IDX]


def _prep_w_in(w_in_l, d_model):
    sizes = (QKV_A, H_A, H_A, H_A * DV_A, 3 * HB, H_IDX * D_IDX, D_IDX, H_IDX, d_model, d_model)
    parts, start = [], 0
    for s in sizes:
        parts.append(w_in_l[:, start:start + s])
        start += s
    qkv_a, beta, a_raw, z_a, qkv_b, q_idx, k_idx, w_idx, gate_a, gate_b = parts
    pad = jnp.zeros((w_in_l.shape[0], LANES - D_IDX - 2 * H_A - H_IDX), w_in_l.dtype)
    misc = jnp.concatenate([k_idx, beta, a_raw, w_idx, pad], axis=1)
    return jnp.concatenate([qkv_a, z_a, qkv_b, q_idx, gate_a, gate_b, misc], axis=1).astype(BF16)


def _proj(x2d, g_row, w_bf, tm):
    n, d = x2d.shape
    widths = (QKV_A, H_A * DV_A, HB, HB, HB, H_IDX * D_IDX, d, d, LANES, D_IDX)
    row = lambda i: (i, 0)
    return pl.pallas_call(
        _proj_kernel,
        grid=(n // tm,),
        in_specs=[pl.BlockSpec((tm, d), row), _const_spec(g_row.shape), _const_spec(w_bf.shape)],
        out_specs=[pl.BlockSpec((tm, wd), row) for wd in widths],
        out_shape=[jax.ShapeDtypeStruct((n, wd), F32) for wd in widths],
        compiler_params=_cparams(1),
        name="proj",
    )(x2d, g_row, w_bf)


def _gdn_kernel(qkv_ref, misc_ref, z_ref, cst_ref, rec0_ref, cw_ref, alog_ref, dtb_ref, ng_ref,
                o_ref, cnew_ref, rnew_ref, s_scr, prev_scr, *, rows, chunk):
    t = pl.program_id(1)
    nt = pl.num_programs(1)
    npr = CONV_A - 1

    @pl.when(t == 0)
    def _():
        s_scr[...] = rec0_ref[0]
        prev_scr[SUBLANES - npr:SUBLANES, :] = cst_ref[0]

    x = qkv_ref[0]
    prev = prev_scr[...]
    conv = _shift_rows(x, 3, prev) * cw_ref[0:1, :]
    conv = conv + _shift_rows(x, 2, prev) * cw_ref[1:2, :]
    conv = conv + _shift_rows(x, 1, prev) * cw_ref[2:3, :]
    conv = conv + x * cw_ref[3:4, :]
    prev_scr[SUBLANES - npr:SUBLANES, :] = x[rows - npr:rows, :]

    @pl.when(t == nt - 1)
    def _():
        cnew_ref[0] = x[rows - npr:rows, :]

    act = _silu(conv)
    misc = misc_ref[0]
    beta_full = jax.nn.sigmoid(misc)
    sp = misc + dtb_ref[...]
    softplus = jnp.maximum(sp, 0.0) + jnp.log1p(jnp.exp(-jnp.abs(sp)))
    g_full = -jnp.exp(alog_ref[...]) * softplus

    ri = lax.broadcasted_iota(jnp.int32, (chunk, chunk), 0)
    ci = lax.broadcasted_iota(jnp.int32, (chunk, chunk), 1)
    tril = ri >= ci
    strict = ri > ci
    tril_f = tril.astype(F32)
    eye = (ri == ci).astype(F32)
    lane = lax.broadcasted_iota(jnp.int32, (chunk, LANES), 1)
    n_levels = int(math.log2(chunk))
    hk = H_A * DK_A

    for c0 in range(0, rows, chunk):
        gcum_full = _dot(tril_f, g_full[c0:c0 + chunk, :], HI)
        for h in range(H_A):
            q = act[c0:c0 + chunk, h * DK_A:(h + 1) * DK_A]
            k = act[c0:c0 + chunk, hk + h * DK_A:hk + (h + 1) * DK_A]
            v = act[c0:c0 + chunk, 2 * hk + h * DV_A:2 * hk + (h + 1) * DV_A]
            q = q * lax.rsqrt(jnp.sum(q * q, axis=-1, keepdims=True) + EPS) * (DK_A ** -0.5)
            k = k * lax.rsqrt(jnp.sum(k * k, axis=-1, keepdims=True) + EPS)
            beta = beta_full[c0:c0 + chunk, MISC_BETA + h:MISC_BETA + h + 1]
            gc = gcum_full[:, MISC_A + h:MISC_A + h + 1]
            sel = (lane == MISC_A + h).astype(F32)
            g_row = _dot_nt(sel, gcum_full, HI)
            diff = jnp.where(tril, gc - g_row, 0.0)
            gamma = jnp.where(tril, jnp.exp(diff), 0.0)
            kb = k * beta
            a_mat = jnp.where(strict, _dot_nt(kb, k, HI) * gamma, 0.0)
            xp = -a_mat
            inv = eye + xp
            for _ in range(n_levels - 1):
                xp = _dot(xp, xp, HI)
                inv = inv + _dot(inv, xp, HI)
            egc = jnp.exp(gc)
            rhs = jnp.concatenate([kb * egc, v * beta], axis=-1)
            sol = _dot(inv, rhs, HI)
            w_mat, u_mat = sol[:, :DK_A], sol[:, DK_A:]
            attn = jnp.where(tril, _dot_nt(q, k, HI) * gamma, 0.0)
            g_last = gc[chunk - 1:chunk, :]
            q_dec = q * egc
            k_dec = k * jnp.exp(g_last - gc)
            s = s_scr[h]
            v_new = u_mat - _dot(w_mat, s, HI)
            o = _dot(q_dec, s, HI) + _dot(attn, v_new, HI)
            s_scr[h] = s * jnp.exp(g_last) + _dot_tn(k_dec, v_new, HI)
            z = z_ref[0, c0:c0 + chunk, h * DV_A:(h + 1) * DV_A]
            o_ref[0, c0:c0 + chunk, h * DV_A:(h + 1) * DV_A] = _rms(o, ng_ref[...]) * _silu(z)

    @pl.when(t == nt - 1)
    def _():
        rnew_ref[0] = s_scr[...]


def _gdn(qkv_a, misc, z_a, conv_state, rec0, conv_w, alog_row, dtb_row, ng_row, rows):
    b, t, _ = qkv_a.shape
    chunk = math.gcd(t, CHUNK_A)
    blk = lambda w: pl.BlockSpec((1, rows, w), lambda i, j: (i, j, 0))
    per_b3 = lambda s: pl.BlockSpec((1,) + s, lambda i, j: (i, 0, 0))
    per_b4 = lambda s: pl.BlockSpec((1,) + s, lambda i, j: (i, 0, 0, 0))
    return pl.pallas_call(
        functools.partial(_gdn_kernel, rows=rows, chunk=chunk),
        grid=(b, t // rows),
        in_specs=[blk(QKV_A), blk(LANES), blk(H_A * DV_A),
                  per_b3((CONV_A - 1, QKV_A)), per_b4((H_A, DK_A, DV_A)),
                  _const_spec(conv_w.shape), _const_spec(alog_row.shape),
                  _const_spec(dtb_row.shape), _const_spec(ng_row.shape)],
        out_specs=[blk(H_A * DV_A), per_b3((CONV_A - 1, QKV_A)), per_b4((H_A, DK_A, DV_A))],
        out_shape=[jax.ShapeDtypeStruct((b, t, H_A * DV_A), F32),
                   jax.ShapeDtypeStruct((b, CONV_A - 1, QKV_A), F32),
                   jax.ShapeDtypeStruct((b, H_A, DK_A, DV_A), F32)],
        scratch_shapes=[pltpu.VMEM((H_A, DK_A, DV_A), F32), pltpu.VMEM((SUBLANES, QKV_A), F32)],
        compiler_params=_cparams(2),
        name="gdn",
    )(qkv_a, misc, z_a, conv_state, rec0, conv_w, alog_row, dtb_row, ng_row)


def _rel_bias_tile(dist, bias_ref):
    nd = jnp.maximum(dist, 0)
    max_exact = N_BUCKETS // 2
    nf = jnp.maximum(nd, max_exact).astype(F32)
    large = max_exact + (jnp.log(nf / max_exact) / math.log(MAX_DIST / max_exact)
                         * (N_BUCKETS - max_exact)).astype(jnp.int32)
    large = jnp.minimum(large, N_BUCKETS - 1)
    bucket = jnp.where(nd < max_exact, nd, large)
    outs = []
    for h in range(H_B):
        val = jnp.full(dist.shape, bias_ref[0, h], F32)
        for bk in range(1, N_BUCKETS):
            val = jnp.where(bucket == bk, bias_ref[bk, h], val)
        outs.append(val)
    return outs


def _order_key(score, admissible):
    bits = pltpu.bitcast(score, jnp.int32)
    key = jnp.where(bits < 0, bits ^ jnp.int32(0x7FFFFFFF), bits)
    key = jnp.where(bits == jnp.int32(INT_MIN), 0, key)
    return jnp.where(admissible, key, jnp.int32(INT_MIN))


def _count(m):
    return jnp.sum(m.astype(F32), axis=-1, keepdims=True)


def _kth_largest(key, top_k):
    rows = key.shape[0]

    def body(i, theta):
        inc = lax.shift_left(jnp.int32(1), jnp.int32(31) - i)
        cand = theta + inc
        cnt = _count(key >= cand)
        return jnp.where(cnt >= top_k, cand, theta)

    return lax.fori_loop(0, 32, body, jnp.full((rows, 1), INT_MIN, jnp.int32))


def _select_topk(key, admissible, top_k, mask_ref, tri_ref):
    theta = _kth_largest(key, top_k)
    gt = key > theta
    eq = key == theta
    need = top_k - _count(gt)
    all_row = theta == jnp.int32(INT_MIN)
    simple = jnp.logical_or(_count(eq) == need, all_row)
    n_bad = jnp.sum(jnp.where(simple, 0.0, 1.0))

    @pl.when(n_bad == 0.0)
    def _():
        mask_ref[...] = jnp.logical_and(key >= theta, admissible).astype(F32)

    @pl.when(n_bad != 0.0)
    def _():
        run = jnp.zeros((key.shape[0], 1), F32)
        for c0 in range(0, key.shape[1], LANES):
            eq_c = eq[:, c0:c0 + LANES].astype(F32)
            before = _dot(eq_c.astype(BF16), tri_ref[...]) + run
            take = jnp.logical_and(eq[:, c0:c0 + LANES], before < need)
            m = jnp.logical_and(jnp.logical_or(gt[:, c0:c0 + LANES], take), admissible[:, c0:c0 + LANES])
            mask_ref[:, c0:c0 + LANES] = m.astype(F32)
            run = run + jnp.sum(eq_c, axis=-1, keepdims=True)


def _init_tri(tri_ref):
    r = lax.broadcasted_iota(jnp.int32, (LANES, LANES), 0)
    c = lax.broadcasted_iota(jnp.int32, (LANES, LANES), 1)
    tri_ref[...] = (r < c).astype(BF16)


def _attn_p_kernel(bias_ref, qb_ref, kb_ref, vb_ref, qidx_ref, miscq_ref, misck_ref, o_ref,
                   k_scr, v_scr, kidx_scr, btab_scr, mask_scr, tri_scr, *, tq, top_k):
    b = pl.program_id(0)
    i = pl.program_id(1)
    nq = pl.num_programs(1)
    t_keys = kb_ref.shape[1]

    @pl.when(jnp.logical_and(b == 0, i == 0))
    def _():
        _init_tri(tri_scr)
        r = lax.broadcasted_iota(jnp.int32, (tq, tq), 0)
        c = lax.broadcasted_iota(jnp.int32, (tq, tq), 1)
        for m in range(2 * nq - 1):
            dd = nq - 1 - m
            if dd >= 0:
                tiles = _rel_bias_tile(dd * tq + r - c, bias_ref)
                for h in range(H_B):
                    btab_scr[h, :, m * tq:(m + 1) * tq] = tiles[h]
            else:
                for h in range(H_B):
                    btab_scr[h, :, m * tq:(m + 1) * tq] = jnp.zeros((tq, tq), F32)

    @pl.when(i == 0)
    def _():
        k_scr[...] = kb_ref[0].astype(BF16)
        v_scr[...] = vb_ref[0].astype(BF16)
        kidx_scr[...] = misck_ref[0, :, MISC_KIDX:MISC_KIDX + D_IDX].astype(BF16)

    qidx = qidx_ref[0]
    miscq = miscq_ref[0]
    kidx = kidx_scr[...]
    score = jnp.zeros((tq, t_keys), F32)
    for h in range(H_IDX):
        dots = _dot_nt(qidx[:, h * D_IDX:(h + 1) * D_IDX].astype(BF16), kidx) * (D_IDX ** -0.5)
        w = miscq[:, MISC_WIDX + h:MISC_WIDX + h + 1] * (H_IDX ** -0.5)
        score = score + jnp.maximum(dots, 0.0) * w

    qpos = i * tq + lax.broadcasted_iota(jnp.int32, (tq, t_keys), 0)
    kpos = lax.broadcasted_iota(jnp.int32, (tq, t_keys), 1)
    admissible = kpos <= qpos
    key = _order_key(score, admissible)
    _select_topk(key, admissible, top_k, mask_scr, tri_scr)

    sel = mask_scr[...] > 0.5
    qb = qb_ref[0]
    boff = pl.multiple_of((nq - 1 - i) * tq, LANES)
    for h in range(H_B):
        qh = qb[:, h * DH_B:(h + 1) * DH_B].astype(BF16)
        logits = _dot_nt(qh, k_scr[:, h * DH_B:(h + 1) * DH_B]) * (DH_B ** -0.5)
        logits = logits + btab_scr[h, :, pl.ds(boff, t_keys)]
        logits = jnp.where(sel, logits, NEG_INF)
        m = jnp.max(logits, axis=-1, keepdims=True)
        p = jnp.exp(logits - m)
        denom = jnp.sum(p, axis=-1, keepdims=True)
        out = _dot(p.astype(BF16), v_scr[:, h * DH_B:(h + 1) * DH_B])
        o_ref[0, :, h * DH_B:(h + 1) * DH_B] = out / denom


def _attn_p(rel_bias, q_b, k_b, v_b, q_idx, misc, tq=128):
    b, t, _ = q_b.shape
    nq = t // tq
    top_k = min(TOPK_MAX, t // 4)
    qblk = lambda w: pl.BlockSpec((1, tq, w), lambda bi, i: (bi, i, 0))
    full = lambda w: pl.BlockSpec((1, t, w), lambda bi, i: (bi, 0, 0))
    return pl.pallas_call(
        functools.partial(_attn_p_kernel, tq=tq, top_k=top_k),
        grid=(b, nq),
        in_specs=[pl.BlockSpec(memory_space=pltpu.SMEM),
                  qblk(HB), full(HB), full(HB), qblk(H_IDX * D_IDX), qblk(LANES), full(LANES)],
        out_specs=qblk(HB),
        out_shape=jax.ShapeDtypeStruct((b, t, HB), F32),
        scratch_shapes=[pltpu.VMEM((t, HB), BF16), pltpu.VMEM((t, HB), BF16),
                        pltpu.VMEM((t, D_IDX), BF16),
                        pltpu.VMEM((H_B, tq, (2 * nq - 1) * tq), F32),
                        pltpu.VMEM((tq, t), F32), pltpu.VMEM((LANES, LANES), BF16)],
        compiler_params=_cparams(2),
        name="attn_p",
    )(rel_bias, q_b, k_b, v_b, q_idx, misc, misc)


def _attn_s_kernel(pt_ref, bias_ref, qblk_ref, qidx_ref, widx_ref, knew_ref, vnew_ref, kidxnew_ref,
                   *refs, t_new, n_pages, ppg, top_k):
    idx_pages = refs[0:ppg]
    k_pages = refs[ppg:2 * ppg]
    v_pages = refs[2 * ppg:3 * ppg]
    o_ref = refs[3 * ppg]
    score_scr, logit_scr, mask_scr, acc_scr, btab_scr, tri_scr = refs[3 * ppg + 1:]
    b = pl.program_id(0)
    phase = pl.program_id(1)
    g = pl.program_id(2)
    ng = pl.num_programs(2)
    past = n_pages * PAGE_SIZE
    n_cols = past + PAGE_SIZE
    hq = H_B * t_new
    near0 = max(0, (past - MAX_DIST) // PAGE_SIZE) * PAGE_SIZE
    n_near = n_cols - near0

    @pl.when(jnp.logical_and(b == 0, jnp.logical_and(phase == 0, g == 0)))
    def _():
        _init_tri(tri_scr)
        trow = lax.broadcasted_iota(jnp.int32, (t_new, n_near), 0)
        col = near0 + lax.broadcasted_iota(jnp.int32, (t_new, n_near), 1)
        tiles = _rel_bias_tile(past + trow - col, bias_ref)
        for h in range(H_B):
            btab_scr[h] = tiles[h]

    qblk = qblk_ref[0].astype(BF16)
    qidx = qidx_ref[0].astype(BF16)
    widx = widx_ref[0] * (H_IDX ** -0.5)

    def scores_of(kidx_page):
        dots = _dot_nt(qidx, kidx_page.astype(BF16)) * (D_IDX ** -0.5)
        contrib = jnp.maximum(dots, 0.0) * widx
        s = contrib[0:t_new]
        for h in range(1, H_IDX):
            s = s + contrib[h * t_new:(h + 1) * t_new]
        return s

    @pl.when(phase == 0)
    def _():
        for m in range(ppg):
            c0 = pl.multiple_of((g * ppg + m) * PAGE_SIZE, PAGE_SIZE)
            score_scr[:, pl.ds(c0, PAGE_SIZE)] = scores_of(idx_pages[m][...])
            logit_scr[:, pl.ds(c0, PAGE_SIZE)] = _dot_nt(qblk, k_pages[m][...].astype(BF16))

        @pl.when(g == ng - 1)
        def _():
            score_scr[:, past:past + PAGE_SIZE] = scores_of(kidxnew_ref[0])
            logit_scr[:, past:past + PAGE_SIZE] = _dot_nt(qblk, knew_ref[0].astype(BF16))

    @pl.when(jnp.logical_and(phase == 1, g == 0))
    def _():
        trow = lax.broadcasted_iota(jnp.int32, (t_new, n_cols), 0)
        col = lax.broadcasted_iota(jnp.int32, (t_new, n_cols), 1)
        admissible = col <= past + trow
        key = _order_key(score_scr[...], admissible)
        _select_topk(key, admissible, top_k, mask_scr, tri_scr)
        sel = mask_scr[...] > 0.5
        for h in range(H_B):
            rs = slice(h * t_new, (h + 1) * t_new)
            near = logit_scr[rs, near0:n_cols] * (DH_B ** -0.5) + btab_scr[h]
            near = jnp.where(sel[:, near0:n_cols], near, NEG_INF)
            mx = jnp.max(near, axis=-1, keepdims=True)
            if near0 > 0:
                far = logit_scr[rs, 0:near0] * (DH_B ** -0.5) + bias_ref[N_BUCKETS - 1, h]
                far = jnp.where(sel[:, 0:near0], far, NEG_INF)
                mx = jnp.maximum(mx, jnp.max(far, axis=-1, keepdims=True))
            pn = jnp.exp(near - mx)
            denom = jnp.sum(pn, axis=-1, keepdims=True)
            if near0 > 0:
                pf = jnp.exp(far - mx)
                denom = denom + jnp.sum(pf, axis=-1, keepdims=True)
                logit_scr[rs, 0:near0] = pf / denom
            logit_scr[rs, near0:n_cols] = pn / denom
        p_new = logit_scr[:, past:past + PAGE_SIZE].astype(BF16)
        acc_scr[...] = _dot(p_new, vnew_ref[0].astype(BF16))

    @pl.when(phase == 1)
    def _():
        acc = acc_scr[...]
        for m in range(ppg):
            c0 = pl.multiple_of((g * ppg + m) * PAGE_SIZE, PAGE_SIZE)
            p = logit_scr[:, pl.ds(c0, PAGE_SIZE)].astype(BF16)
            acc = acc + _dot(p, v_pages[m][...].astype(BF16))
        acc_scr[...] = acc

        @pl.when(g == ng - 1)
        def _():
            for h in range(H_B):
                o_ref[0, :, h * DH_B:(h + 1) * DH_B] = acc[h * t_new:(h + 1) * t_new, h * DH_B:(h + 1) * DH_B]


def _attn_s(page_table, rel_bias, q_b, k_new, v_new, q_idx, kidx_new, misc, cache_k, cache_v, cache_idx_k,
            layer, ppg=8):
    b, t_new, _ = q_b.shape
    n_pages = page_table.shape[1]
    past = n_pages * PAGE_SIZE
    n_cols = past + PAGE_SIZE
    top_k = min(TOPK_MAX, (past + t_new) // 4)
    ng = n_pages // ppg
    hq = H_B * t_new
    near0 = max(0, (past - MAX_DIST) // PAGE_SIZE) * PAGE_SIZE

    qh = q_b.reshape(b, t_new, H_B, DH_B).transpose(0, 2, 1, 3)
    eye = jnp.eye(H_B, dtype=q_b.dtype)
    qblk = (qh[:, :, :, None, :] * eye[None, :, None, :, None]).reshape(b, hq, HB)
    qidx = q_idx.reshape(b, t_new, H_IDX, D_IDX).transpose(0, 2, 1, 3).reshape(b, H_IDX * t_new, D_IDX)
    widx = misc[:, :, MISC_WIDX:MISC_WIDX + H_IDX].transpose(0, 2, 1).reshape(b, H_IDX * t_new, 1)
    pad_page = lambda a: jnp.pad(a, ((0, 0), (0, PAGE_SIZE - t_new), (0, 0)))
    k_new, v_new, kidx_new = pad_page(k_new), pad_page(v_new), pad_page(kidx_new)
    ck = cache_k.reshape(cache_k.shape[0], cache_k.shape[1], PAGE_SIZE, HB)
    cv = cache_v.reshape(cache_v.shape[0], cache_v.shape[1], PAGE_SIZE, HB)

    per_b = lambda s: pl.BlockSpec((1,) + s, lambda bi, ph, g, pt: (bi, 0, 0))

    def page_spec(width, m, active_phase):
        def imap(bi, ph, g, pt):
            gg = jnp.where(ph == active_phase, g, jnp.where(ph < active_phase, 0, ng - 1))
            return (layer, pt[bi, gg * ppg + m], 0, 0)
        return pl.BlockSpec((None, None, PAGE_SIZE, width), imap)

    in_specs = ([pl.BlockSpec(memory_space=pltpu.SMEM),
                 per_b((hq, HB)), per_b((H_IDX * t_new, D_IDX)), per_b((H_IDX * t_new, 1)),
                 per_b((PAGE_SIZE, HB)), per_b((PAGE_SIZE, HB)), per_b((PAGE_SIZE, D_IDX))]
                + [page_spec(D_IDX, m, 0) for m in range(ppg)]
                + [page_spec(HB, m, 0) for m in range(ppg)]
                + [page_spec(HB, m, 1) for m in range(ppg)])
    grid_spec = pltpu.PrefetchScalarGridSpec(
        num_scalar_prefetch=1,
        grid=(b, 2, ng),
        in_specs=in_specs,
        out_specs=pl.BlockSpec((1, t_new, HB), lambda bi, ph, g, pt: (bi, 0, 0)),
        scratch_shapes=[pltpu.VMEM((t_new, n_cols), F32), pltpu.VMEM((hq, n_cols), F32),
                        pltpu.VMEM((t_new, n_cols), F32), pltpu.VMEM((hq, HB), F32),
                        pltpu.VMEM((H_B, t_new, n_cols - near0), F32),
                        pltpu.VMEM((LANES, LANES), BF16)])
    return pl.pallas_call(
        functools.partial(_attn_s_kernel, t_new=t_new, n_pages=n_pages, ppg=ppg, top_k=top_k),
        grid_spec=grid_spec,
        out_shape=jax.ShapeDtypeStruct((b, t_new, HB), F32),
        compiler_params=_cparams(3),
        name="attn_s",
    )(page_table, rel_bias, qblk, qidx, widx, k_new, v_new, kidx_new,
      *([cache_idx_k] * ppg), *([ck] * ppg), *([cv] * ppg))


def _ffn_kernel(x_ref, oa_ref, ob_ref, ga_ref, gb_ref, fst_ref, wpa_ref, wpb_ref, wout_ref, n2_ref,
                wup_ref, cfw_ref, cfb_ref, wdown_ref, fg_ref, y_ref, fnew_ref, prev_scr, *, tm, ch, d_ff):
    t = pl.program_id(1)
    nt = pl.num_programs(1)
    npr = CONV_F - 1

    @pl.when(t == 0)
    def _():
        prev_scr[SUBLANES - npr:SUBLANES, :] = fst_ref[0]

    pa = _dot(oa_ref[0].astype(BF16), wpa_ref[...])
    pb = _dot(ob_ref[0].astype(BF16), wpb_ref[...])
    merged = jax.nn.sigmoid(ga_ref[0]) * pa + jax.nn.sigmoid(gb_ref[0]) * pb
    x1 = x_ref[0] + _dot(merged.astype(BF16), wout_ref[...])
    xn2 = _rms(x1, n2_ref[...]).astype(BF16)

    def conv_cols(c0):
        up = _dot(xn2, wup_ref[:, c0:c0 + ch])
        prev = prev_scr[:, c0:c0 + ch]
        out = _shift_rows(up, 2, prev) * cfw_ref[0:1, c0:c0 + ch]
        out = out + _shift_rows(up, 1, prev) * cfw_ref[1:2, c0:c0 + ch]
        out = out + up * cfw_ref[2:3, c0:c0 + ch]
        prev_scr[SUBLANES - npr:SUBLANES, c0:c0 + ch] = up[tm - npr:tm, :]
        return out + cfb_ref[:, c0:c0 + ch]

    acc = jnp.zeros(x1.shape, F32)
    for c0 in range(0, d_ff, ch):
        gate = conv_cols(c0)
        val = conv_cols(d_ff + c0)
        acc = acc + _dot((_silu(gate) * val).astype(BF16), wdown_ref[c0:c0 + ch, :])
    y_ref[0] = _rms(x1 + acc, fg_ref[...])

    @pl.when(t == nt - 1)
    def _():
        fnew_ref[0] = prev_scr[SUBLANES - npr:SUBLANES, :]


def _ffn(x, o_a, o_b, gate_a, gate_b, ffn_state, wpa, wpb, wout, n2_row, wup, cfw, cfb_row, wdown, fg_row, tm):
    b, t, d = x.shape
    d_ff = wdown.shape[0]
    ch = 256
    blk = lambda w: pl.BlockSpec((1, tm, w), lambda i, j: (i, j, 0))
    per_b = pl.BlockSpec((1, CONV_F - 1, 2 * d_ff), lambda i, j: (i, 0, 0))
    consts = (wpa, wpb, wout, n2_row, wup, cfw, cfb_row, wdown, fg_row)
    return pl.pallas_call(
        functools.partial(_ffn_kernel, tm=tm, ch=ch, d_ff=d_ff),
        grid=(b, t // tm),
        in_specs=[blk(d), blk(o_a.shape[-1]), blk(o_b.shape[-1]), blk(d), blk(d), per_b]
                 + [_const_spec(c.shape) for c in consts],
        out_specs=[blk(d), per_b],
        out_shape=[jax.ShapeDtypeStruct((b, t, d), F32),
                   jax.ShapeDtypeStruct((b, CONV_F - 1, 2 * d_ff), F32)],
        scratch_shapes=[pltpu.VMEM((SUBLANES, 2 * d_ff), F32)],
        compiler_params=_cparams(2),
        name="ffn",
    )(x, o_a, o_b, gate_a, gate_b, ffn_state, *consts)


def _misc_row(vals, col0):
    return jnp.zeros((1, LANES), F32).at[0, col0:col0 + vals.shape[0]].set(vals.astype(F32))


def _layer(x, conv_state, rec_state, ffn_state, attend, lw, proj_tm, gdn_rows, ffn_tm):
    b, t, d = x.shape
    (w_in_bf, conv_a_w, alog_row, dtb_row, ng_row, wpa, wpb, wout, n1_row, n2_row, wup, cfw, cfb_row,
     wdown, fg_row) = lw
    (qkv_a, z_a, q_b, k_b, v_b, q_idx, gate_a, gate_b, misc, k_idx) = _proj(
        x.reshape(b * t, d), n1_row, w_in_bf, proj_tm)
    r3 = lambda a: a.reshape(b, t, a.shape[-1])
    qkv_a, z_a, q_b, k_b, v_b, q_idx, gate_a, gate_b, misc, k_idx = map(
        r3, (qkv_a, z_a, q_b, k_b, v_b, q_idx, gate_a, gate_b, misc, k_idx))
    o_a, conv_new, rec_new = _gdn(qkv_a, misc, z_a, conv_state, rec_state, conv_a_w, alog_row, dtb_row,
                                  ng_row, gdn_rows)
    o_b = attend(q_b, k_b, v_b, q_idx, k_idx, misc)
    y, ffn_new = _ffn(x, o_a, o_b, gate_a, gate_b, ffn_state, wpa, wpb, wout, n2_row, wup, cfw, cfb_row,
                      wdown, fg_row, ffn_tm)
    kv_shape = (b, t, H_B, DH_B)
    return y, (k_b.reshape(kv_shape), v_b.reshape(kv_shape), k_idx, conv_new, rec_new, ffn_new)


def kernel(x_prompt, x_sample, cache_k, cache_v, cache_idx_k, state_a_conv, state_a_rec, state_ffn_conv,
           page_table, w_in, conv_a_w, a_log, dt_bias, norm_a_g, w_proj_a, w_proj_b, w_out, rel_bias,
           norm1_g, norm2_g, w_up, conv_f_w, conv_f_b, w_down, final_g):
    depth = w_in.shape[0]
    assert depth == 1, "the final norm is fused into the layer's last kernel"
    bp, tp, d = x_prompt.shape
    bs, ts, _ = x_sample.shape
    d_ff = w_down.shape[1]
    l = 0
    lw = (_prep_w_in(w_in[l], d), conv_a_w[l], _misc_row(a_log[l], MISC_A), _misc_row(dt_bias[l], MISC_A),
          norm_a_g[l].reshape(1, -1), w_proj_a[l].astype(BF16), w_proj_b[l].astype(BF16),
          w_out[l].astype(BF16), norm1_g[l].reshape(1, -1), norm2_g[l].reshape(1, -1),
          w_up[l].astype(BF16), conv_f_w[l], conv_f_b[l].reshape(1, -1), w_down[l].astype(BF16),
          final_g.reshape(1, -1))

    def attend_prompt(q_b, k_b, v_b, q_idx, k_idx, misc):
        return _attn_p(rel_bias, q_b, k_b, v_b, q_idx, misc)

    def attend_sample(q_b, k_b, v_b, q_idx, k_idx, misc):
        return _attn_s(page_table, rel_bias, q_b, k_b, v_b, q_idx, k_idx, misc, cache_k, cache_v,
                       cache_idx_k, l)

    zeros = lambda *s: jnp.zeros(s, x_prompt.dtype)
    y_p, st_p = _layer(x_prompt, zeros(bp, CONV_A - 1, QKV_A), zeros(bp, H_A, DK_A, DV_A),
                       zeros(bp, CONV_F - 1, 2 * d_ff), attend_prompt, lw,
                       proj_tm=256, gdn_rows=min(tp, 256), ffn_tm=min(tp, 256))
    y_s, st_s = _layer(x_sample, state_a_conv[l], state_a_rec[l], state_ffn_conv[l], attend_sample, lw,
                       proj_tm=min(bs * ts, 256), gdn_rows=ts, ffn_tm=ts)
    stack = lambda a: a[None]
    return (y_p, y_s) + tuple(map(stack, st_p)) + tuple(map(stack, st_s))
```

```python
import functools
import math

import jax
import jax.numpy as jnp
from jax import lax
from jax.experimental import pallas as pl
from jax.experimental.pallas import tpu as pltpu

F32 = jnp.float32
BF16 = jnp.bfloat16
HI = lax.Precision.HIGHEST

H_A, DK_A, DV_A, CONV_A, CHUNK_A = 4, 128, 128, 4, 64
H_B, DH_B, H_IDX, D_IDX = 4, 128, 8, 64
TOPK_MAX, N_BUCKETS, MAX_DIST = 256, 32, 2048
CONV_F, EPS, PAGE_SIZE = 3, 1e-6, 128
QKV_A = H_A * (2 * DK_A + DV_A)
HB = H_B * DH_B

LANES = 128
SUBLANES = 8
VMEM_LIMIT = 56 * 1024 * 1024

MISC_KIDX, MISC_BETA, MISC_A, MISC_WIDX = 0, D_IDX, D_IDX + H_A, D_IDX + 2 * H_A

GDN_MODE_QK = "bf"
GDN_MODE_INV = "bf"
GDN_MODE_STATE = "bf"

INT_MIN = -2 ** 31
NEG_INF = float("-inf")


def _cparams(n_axes):
    return pltpu.CompilerParams(dimension_semantics=("arbitrary",) * n_axes,
                                vmem_limit_bytes=VMEM_LIMIT)


def _const_spec(shape):
    nd = len(shape)
    return pl.BlockSpec(shape, lambda *_: (0,) * nd, pipeline_mode=pl.Buffered(1))


def _silu(x):
    return x * jax.nn.sigmoid(x)


def _dot(a, b, precision=None):
    return jnp.dot(a, b, preferred_element_type=F32, precision=precision)


def _dot_nt(a, b, precision=None):
    return lax.dot_general(a, b, (((1,), (1,)), ((), ())), preferred_element_type=F32,
                           precision=precision)


def _dot_tn(a, b, precision=None):
    return lax.dot_general(a, b, (((0,), (0,)), ((), ())), preferred_element_type=F32,
                           precision=precision)


_DIMS = {"nn": (((1,), (0,)), ((), ())), "nt": (((1,), (1,)), ((), ())), "tn": (((0,), (0,)), ((), ()))}


def _mm(a, b, mode, form="nn"):
    if mode == "bf":
        return lax.dot_general(a.astype(BF16), b.astype(BF16), _DIMS[form], preferred_element_type=F32)
    return lax.dot_general(a, b, _DIMS[form], preferred_element_type=F32, precision=HI)


def _cumsum_rows(x):
    row = lax.broadcasted_iota(jnp.int32, x.shape, 0)
    s = 1
    while s < x.shape[0]:
        x = x + jnp.where(row >= s, pltpu.roll(x, s, 0), 0.0)
        s *= 2
    return x


def _rms(x, g):
    return x * lax.rsqrt(jnp.mean(x * x, axis=-1, keepdims=True) + EPS) * g


def _shift_rows(x, k, prev):
    n = prev.shape[0]
    out = pltpu.roll(x, k, 0)
    row = lax.broadcasted_iota(jnp.int32, x.shape, 0)
    for r in range(k):
        out = jnp.where(row == r, prev[n - k + r:n - k + r + 1, :], out)
    return out


PROJ_WIDTHS = (QKV_A, H_A * DV_A, HB, HB, HB, H_IDX * D_IDX, None, None, LANES)


def _proj_kernel(x_ref, g_ref, w_ref, qkva_ref, za_ref, qb_ref, kb_ref, vb_ref, qidx_ref,
                 ga_ref, gb_ref, misc_ref, kidx_ref):
    xn = _rms(x_ref[...], g_ref[...]).astype(BF16)
    off = 0
    for o_ref in (qkva_ref, za_ref, qb_ref, kb_ref, vb_ref, qidx_ref, ga_ref, gb_ref, misc_ref):
        wd = o_ref.shape[-1]
        o_ref[...] = _dot(xn, w_ref[:, off:off + wd])
        off += wd
    kidx_ref[...] = misc_ref[:, MISC_KIDX:MISC_KIDX + D_IDX]


def _prep_w_in(w_in_l, d_model):
    sizes = (QKV_A, H_A, H_A, H_A * DV_A, 3 * HB, H_IDX * D_IDX, D_IDX, H_IDX, d_model, d_model)
    parts, start = [], 0
    for s in sizes:
        parts.append(w_in_l[:, start:start + s])
        start += s
    qkv_a, beta, a_raw, z_a, qkv_b, q_idx, k_idx, w_idx, gate_a, gate_b = parts
    pad = jnp.zeros((w_in_l.shape[0], LANES - D_IDX - 2 * H_A - H_IDX), w_in_l.dtype)
    misc = jnp.concatenate([k_idx, beta, a_raw, w_idx, pad], axis=1)
    return jnp.concatenate([qkv_a, z_a, qkv_b, q_idx, gate_a, gate_b, misc], axis=1).astype(BF16)


def _proj(x2d, g_row, w_bf, tm):
    n, d = x2d.shape
    widths = (QKV_A, H_A * DV_A, HB, HB, HB, H_IDX * D_IDX, d, d, LANES, D_IDX)
    row = lambda i: (i, 0)
    return pl.pallas_call(
        _proj_kernel,
        grid=(n // tm,),
        in_specs=[pl.BlockSpec((tm, d), row), _const_spec(g_row.shape), _const_spec(w_bf.shape)],
        out_specs=[pl.BlockSpec((tm, wd), row) for wd in widths],
        out_shape=[jax.ShapeDtypeStruct((n, wd), F32) for wd in widths],
        compiler_params=_cparams(1),
        name="proj",
    )(x2d, g_row, w_bf)


def _gdn_kernel(qkv_ref, misc_ref, z_ref, cst_ref, rec0_ref, cw_ref, alog_ref, dtb_ref, ng_ref,
                o_ref, cnew_ref, rnew_ref, s_scr, prev_scr, *, rows, chunk):
    t = pl.program_id(1)
    nt = pl.num_programs(1)
    npr = CONV_A - 1

    @pl.when(t == 0)
    def _():
        s_scr[...] = rec0_ref[0]
        prev_scr[SUBLANES - npr:SUBLANES, :] = cst_ref[0]

    x = qkv_ref[0]
    prev = prev_scr[...]
    conv = _shift_rows(x, 3, prev) * cw_ref[0:1, :]
    conv = conv + _shift_rows(x, 2, prev) * cw_ref[1:2, :]
    conv = conv + _shift_rows(x, 1, prev) * cw_ref[2:3, :]
    conv = conv + x * cw_ref[3:4, :]
    prev_scr[SUBLANES - npr:SUBLANES, :] = x[rows - npr:rows, :]

    @pl.when(t == nt - 1)
    def _():
        cnew_ref[0] = x[rows - npr:rows, :]

    act = _silu(conv)
    misc = misc_ref[0]
    beta_full = jax.nn.sigmoid(misc)
    sp = misc + dtb_ref[...]
    softplus = jnp.maximum(sp, 0.0) + jnp.log1p(jnp.exp(-jnp.abs(sp)))
    g_full = -jnp.exp(alog_ref[...]) * softplus

    ri = lax.broadcasted_iota(jnp.int32, (chunk, chunk), 0)
    ci = lax.broadcasted_iota(jnp.int32, (chunk, chunk), 1)
    tril = ri >= ci
    strict = ri > ci
    eye = (ri == ci).astype(F32)
    lane = lax.broadcasted_iota(jnp.int32, (chunk, LANES), 1)
    n_levels = int(math.log2(chunk))
    hk = H_A * DK_A
    chunks = list(range(0, rows, chunk))
    pairs = [(c0, h) for c0 in chunks for h in range(H_A)]

    gcum = {c0: _cumsum_rows(g_full[c0:c0 + chunk, :]) for c0 in chunks}
    gcum_t = {c0: gcum[c0].T for c0 in chunks} if chunk >= 64 else None

    pre = {}
    for c0, h in pairs:
        q = act[c0:c0 + chunk, h * DK_A:(h + 1) * DK_A]
        k = act[c0:c0 + chunk, hk + h * DK_A:hk + (h + 1) * DK_A]
        v = act[c0:c0 + chunk, 2 * hk + h * DV_A:2 * hk + (h + 1) * DV_A]
        q = q * lax.rsqrt(jnp.sum(q * q, axis=-1, keepdims=True) + EPS) * (DK_A ** -0.5)
        k = k * lax.rsqrt(jnp.sum(k * k, axis=-1, keepdims=True) + EPS)
        beta = beta_full[c0:c0 + chunk, MISC_BETA + h:MISC_BETA + h + 1]
        gc = gcum[c0][:, MISC_A + h:MISC_A + h + 1]
        if gcum_t is not None:
            g_row = gcum_t[c0][MISC_A + h:MISC_A + h + 1, :]
        else:
            g_row = _mm((lane == MISC_A + h).astype(F32), gcum[c0], "hi", "nt")
        gamma = jnp.where(tril, jnp.exp(jnp.where(tril, gc - g_row, 0.0)), 0.0)
        kb = k * beta
        egc = jnp.exp(gc)
        g_last = gc[chunk - 1:chunk, :]
        pre[c0, h] = dict(
            xp=-jnp.where(strict, _mm(kb, k, GDN_MODE_QK, "nt") * gamma, 0.0),
            rhs=jnp.concatenate([kb * egc, v * beta], axis=-1),
            attn=jnp.where(tril, _mm(q, k, GDN_MODE_QK, "nt") * gamma, 0.0),
            q_dec=q * egc, k_dec=k * jnp.exp(g_last - gc), e_last=jnp.exp(g_last))
    for p in pre.values():
        p["inv"] = eye + p["xp"]
    for _ in range(n_levels - 1):
        for p in pre.values():
            p["xp"] = _mm(p["xp"], p["xp"], GDN_MODE_INV)
        for p in pre.values():
            p["inv"] = p["inv"] + _mm(p["inv"], p["xp"], GDN_MODE_INV)
    for p in pre.values():
        p["sol"] = _mm(p["inv"], p["rhs"], GDN_MODE_INV)

    for c0, h in pairs:
        p = pre[c0, h]
        s = s_scr[h]
        v_new = p["sol"][:, DK_A:] - _mm(p["sol"][:, :DK_A], s, GDN_MODE_STATE)
        o = _mm(p["q_dec"], s, GDN_MODE_STATE) + _mm(p["attn"], v_new, GDN_MODE_STATE)
        s_scr[h] = s * p["e_last"] + _mm(p["k_dec"], v_new, GDN_MODE_STATE, "tn")
        z = z_ref[0, c0:c0 + chunk, h * DV_A:(h + 1) * DV_A]
        o_ref[0, c0:c0 + chunk, h * DV_A:(h + 1) * DV_A] = _rms(o, ng_ref[...]) * _silu(z)

    @pl.when(t == nt - 1)
    def _():
        rnew_ref[0] = s_scr[...]


def _gdn(qkv_a, misc, z_a, conv_state, rec0, conv_w, alog_row, dtb_row, ng_row, rows):
    b, t, _ = qkv_a.shape
    chunk = math.gcd(t, CHUNK_A)
    blk = lambda w: pl.BlockSpec((1, rows, w), lambda i, j: (i, j, 0))
    per_b3 = lambda s: pl.BlockSpec((1,) + s, lambda i, j: (i, 0, 0))
    per_b4 = lambda s: pl.BlockSpec((1,) + s, lambda i, j: (i, 0, 0, 0))
    return pl.pallas_call(
        functools.partial(_gdn_kernel, rows=rows, chunk=chunk),
        grid=(b, t // rows),
        in_specs=[blk(QKV_A), blk(LANES), blk(H_A * DV_A),
                  per_b3((CONV_A - 1, QKV_A)), per_b4((H_A, DK_A, DV_A)),
                  _const_spec(conv_w.shape), _const_spec(alog_row.shape),
                  _const_spec(dtb_row.shape), _const_spec(ng_row.shape)],
        out_specs=[blk(H_A * DV_A), per_b3((CONV_A - 1, QKV_A)), per_b4((H_A, DK_A, DV_A))],
        out_shape=[jax.ShapeDtypeStruct((b, t, H_A * DV_A), F32),
                   jax.ShapeDtypeStruct((b, CONV_A - 1, QKV_A), F32),
                   jax.ShapeDtypeStruct((b, H_A, DK_A, DV_A), F32)],
        scratch_shapes=[pltpu.VMEM((H_A, DK_A, DV_A), F32), pltpu.VMEM((SUBLANES, QKV_A), F32)],
        compiler_params=_cparams(2),
        name="gdn",
    )(qkv_a, misc, z_a, conv_state, rec0, conv_w, alog_row, dtb_row, ng_row)


def _rel_bias_tile(dist, bias_ref):
    nd = jnp.maximum(dist, 0)
    max_exact = N_BUCKETS // 2
    nf = jnp.maximum(nd, max_exact).astype(F32)
    large = max_exact + (jnp.log(nf / max_exact) / math.log(MAX_DIST / max_exact)
                         * (N_BUCKETS - max_exact)).astype(jnp.int32)
    large = jnp.minimum(large, N_BUCKETS - 1)
    bucket = jnp.where(nd < max_exact, nd, large)
    outs = []
    for h in range(H_B):
        val = jnp.full(dist.shape, bias_ref[0, h], F32)
        for bk in range(1, N_BUCKETS):
            val = jnp.where(bucket == bk, bias_ref[bk, h], val)
        outs.append(val)
    return outs


def _order_key(score, admissible):
    bits = pltpu.bitcast(score, jnp.int32)
    key = jnp.where(bits < 0, bits ^ jnp.int32(0x7FFFFFFF), bits)
    key = jnp.where(bits == jnp.int32(INT_MIN), 0, key)
    return jnp.where(admissible, key, jnp.int32(INT_MIN))


def _count(m):
    return jnp.sum(m.astype(F32), axis=-1, keepdims=True)


def _kth_largest(key, top_k):
    rows = key.shape[0]

    def body(i, theta):
        inc = lax.shift_left(jnp.int32(1), jnp.int32(31) - i)
        cand = theta + inc
        cnt = _count(key >= cand)
        return jnp.where(cnt >= top_k, cand, theta)

    return lax.fori_loop(0, 32, body, jnp.full((rows, 1), INT_MIN, jnp.int32))


def _select_topk(key, admissible, top_k, mask_ref, tri_ref):
    theta = _kth_largest(key, top_k)
    gt = key > theta
    eq = key == theta
    need = top_k - _count(gt)
    all_row = theta == jnp.int32(INT_MIN)
    simple = jnp.logical_or(_count(eq) == need, all_row)
    n_bad = jnp.sum(jnp.where(simple, 0.0, 1.0))

    @pl.when(n_bad == 0.0)
    def _():
        mask_ref[...] = jnp.logical_and(key >= theta, admissible).astype(F32)

    @pl.when(n_bad != 0.0)
    def _():
        run = jnp.zeros((key.shape[0], 1), F32)
        for c0 in range(0, key.shape[1], LANES):
            eq_c = eq[:, c0:c0 + LANES].astype(F32)
            before = _dot(eq_c.astype(BF16), tri_ref[...]) + run
            take = jnp.logical_and(eq[:, c0:c0 + LANES], before < need)
            m = jnp.logical_and(jnp.logical_or(gt[:, c0:c0 + LANES], take), admissible[:, c0:c0 + LANES])
            mask_ref[:, c0:c0 + LANES] = m.astype(F32)
            run = run + jnp.sum(eq_c, axis=-1, keepdims=True)


def _init_tri(tri_ref):
    r = lax.broadcasted_iota(jnp.int32, (LANES, LANES), 0)
    c = lax.broadcasted_iota(jnp.int32, (LANES, LANES), 1)
    tri_ref[...] = (r < c).astype(BF16)


def _attn_p_kernel(bias_ref, qb_ref, kb_ref, vb_ref, qidx_ref, miscq_ref, misck_ref, o_ref,
                   k_scr, v_scr, kidx_scr, btab_scr, mask_scr, tri_scr, *, tq, top_k):
    b = pl.program_id(0)
    i = pl.program_id(1)
    nq = pl.num_programs(1)
    t_keys = kb_ref.shape[1]

    @pl.when(jnp.logical_and(b == 0, i == 0))
    def _():
        _init_tri(tri_scr)
        r = lax.broadcasted_iota(jnp.int32, (tq, tq), 0)
        c = lax.broadcasted_iota(jnp.int32, (tq, tq), 1)
        for m in range(2 * nq - 1):
            dd = nq - 1 - m
            if dd >= 0:
                tiles = _rel_bias_tile(dd * tq + r - c, bias_ref)
                for h in range(H_B):
                    btab_scr[h, :, m * tq:(m + 1) * tq] = tiles[h]
            else:
                for h in range(H_B):
                    btab_scr[h, :, m * tq:(m + 1) * tq] = jnp.zeros((tq, tq), F32)

    @pl.when(i == 0)
    def _():
        k_scr[...] = kb_ref[0].astype(BF16)
        v_scr[...] = vb_ref[0].astype(BF16)
        kidx_scr[...] = misck_ref[0, :, MISC_KIDX:MISC_KIDX + D_IDX].astype(BF16)

    qidx = qidx_ref[0]
    miscq = miscq_ref[0]
    kidx = kidx_scr[...]
    score = jnp.zeros((tq, t_keys), F32)
    for h in range(H_IDX):
        dots = _dot_nt(qidx[:, h * D_IDX:(h + 1) * D_IDX].astype(BF16), kidx) * (D_IDX ** -0.5)
        w = miscq[:, MISC_WIDX + h:MISC_WIDX + h + 1] * (H_IDX ** -0.5)
        score = score + jnp.maximum(dots, 0.0) * w

    qpos = i * tq + lax.broadcasted_iota(jnp.int32, (tq, t_keys), 0)
    kpos = lax.broadcasted_iota(jnp.int32, (tq, t_keys), 1)
    admissible = kpos <= qpos
    key = _order_key(score, admissible)
    _select_topk(key, admissible, top_k, mask_scr, tri_scr)

    sel = mask_scr[...] > 0.5
    qb = qb_ref[0]
    boff = pl.multiple_of((nq - 1 - i) * tq, LANES)
    for h in range(H_B):
        qh = qb[:, h * DH_B:(h + 1) * DH_B].astype(BF16)
        logits = _dot_nt(qh, k_scr[:, h * DH_B:(h + 1) * DH_B]) * (DH_B ** -0.5)
        logits = logits + btab_scr[h, :, pl.ds(boff, t_keys)]
        logits = jnp.where(sel, logits, NEG_INF)
        m = jnp.max(logits, axis=-1, keepdims=True)
        p = jnp.exp(logits - m)
        denom = jnp.sum(p, axis=-1, keepdims=True)
        out = _dot(p.astype(BF16), v_scr[:, h * DH_B:(h + 1) * DH_B])
        o_ref[0, :, h * DH_B:(h + 1) * DH_B] = out / denom


def _attn_p(rel_bias, q_b, k_b, v_b, q_idx, misc, tq=128):
    b, t, _ = q_b.shape
    nq = t // tq
    top_k = min(TOPK_MAX, t // 4)
    qblk = lambda w: pl.BlockSpec((1, tq, w), lambda bi, i: (bi, i, 0))
    full = lambda w: pl.BlockSpec((1, t, w), lambda bi, i: (bi, 0, 0))
    return pl.pallas_call(
        functools.partial(_attn_p_kernel, tq=tq, top_k=top_k),
        grid=(b, nq),
        in_specs=[pl.BlockSpec(memory_space=pltpu.SMEM),
                  qblk(HB), full(HB), full(HB), qblk(H_IDX * D_IDX), qblk(LANES), full(LANES)],
        out_specs=qblk(HB),
        out_shape=jax.ShapeDtypeStruct((b, t, HB), F32),
        scratch_shapes=[pltpu.VMEM((t, HB), BF16), pltpu.VMEM((t, HB), BF16),
                        pltpu.VMEM((t, D_IDX), BF16),
                        pltpu.VMEM((H_B, tq, (2 * nq - 1) * tq), F32),
                        pltpu.VMEM((tq, t), F32), pltpu.VMEM((LANES, LANES), BF16)],
        compiler_params=_cparams(2),
        name="attn_p",
    )(rel_bias, q_b, k_b, v_b, q_idx, misc, misc)


def _attn_s_kernel(pt_ref, bias_ref, qb_ref, qidx_ref, widx_ref, knew_ref, vnew_ref, kidxnew_ref,
                   *refs, t_new, n_pages, ppg, top_k):
    idx_pages = refs[0:ppg]
    k_pages = refs[ppg:2 * ppg]
    v_pages = refs[2 * ppg:3 * ppg]
    o_ref = refs[3 * ppg]
    score_scr, logit_scr, mask_scr, acc_scr, btab_scr, tri_scr = refs[3 * ppg + 1:]
    b = pl.program_id(0)
    phase = pl.program_id(1)
    g = pl.program_id(2)
    ng = pl.num_programs(2)
    past = n_pages * PAGE_SIZE
    n_cols = past + PAGE_SIZE
    hq = H_B * t_new
    near0 = max(0, (past - MAX_DIST) // PAGE_SIZE) * PAGE_SIZE
    n_near = n_cols - near0

    @pl.when(jnp.logical_and(b == 0, jnp.logical_and(phase == 0, g == 0)))
    def _():
        _init_tri(tri_scr)
        trow = lax.broadcasted_iota(jnp.int32, (t_new, n_near), 0)
        col = near0 + lax.broadcasted_iota(jnp.int32, (t_new, n_near), 1)
        tiles = _rel_bias_tile(past + trow - col, bias_ref)
        for h in range(H_B):
            btab_scr[h] = tiles[h]

    qb = qb_ref[0].astype(BF16)
    qidx = qidx_ref[0].astype(BF16)
    widx = widx_ref[0] * (H_IDX ** -0.5)

    def scores_of(kidx_t):
        dots = _dot(qidx, kidx_t.astype(BF16)) * (D_IDX ** -0.5)
        contrib = jnp.maximum(dots, 0.0) * widx
        s = contrib[0:t_new]
        for h in range(1, H_IDX):
            s = s + contrib[h * t_new:(h + 1) * t_new]
        return s

    def paged_head(page_ref, h):
        return page_ref[pl.ds(h, PAGE_SIZE, stride=H_B), :].astype(BF16)

    def new_head(new_ref, h):
        return new_ref[0, :, h * DH_B:(h + 1) * DH_B].astype(BF16)

    def store_logits(cols, k_head):
        for h in range(H_B):
            logit_scr[h * t_new:(h + 1) * t_new, cols] = _dot_nt(qb[:, h * DH_B:(h + 1) * DH_B], k_head(h))

    @pl.when(phase == 0)
    def _():
        for m in range(ppg):
            c0 = pl.multiple_of((g * ppg + m) * PAGE_SIZE, PAGE_SIZE)
            score_scr[:, pl.ds(c0, PAGE_SIZE)] = scores_of(idx_pages[m][...])
            store_logits(pl.ds(c0, PAGE_SIZE), functools.partial(paged_head, k_pages[m]))

        @pl.when(g == ng - 1)
        def _():
            score_scr[:, past:past + PAGE_SIZE] = scores_of(kidxnew_ref[0])
            store_logits(slice(past, past + PAGE_SIZE), functools.partial(new_head, knew_ref))

    @pl.when(jnp.logical_and(phase == 1, g == 0))
    def _():
        trow = lax.broadcasted_iota(jnp.int32, (t_new, n_cols), 0)
        col = lax.broadcasted_iota(jnp.int32, (t_new, n_cols), 1)
        admissible = col <= past + trow
        key = _order_key(score_scr[...], admissible)
        _select_topk(key, admissible, top_k, mask_scr, tri_scr)
        sel = mask_scr[...] > 0.5
        for h in range(H_B):
            rs = slice(h * t_new, (h + 1) * t_new)
            near = logit_scr[rs, near0:n_cols] * (DH_B ** -0.5) + btab_scr[h]
            near = jnp.where(sel[:, near0:n_cols], near, NEG_INF)
            mx = jnp.max(near, axis=-1, keepdims=True)
            if near0 > 0:
                far = logit_scr[rs, 0:near0] * (DH_B ** -0.5) + bias_ref[N_BUCKETS - 1, h]
                far = jnp.where(sel[:, 0:near0], far, NEG_INF)
                mx = jnp.maximum(mx, jnp.max(far, axis=-1, keepdims=True))
            pn = jnp.exp(near - mx)
            denom = jnp.sum(pn, axis=-1, keepdims=True)
            if near0 > 0:
                pf = jnp.exp(far - mx)
                denom = denom + jnp.sum(pf, axis=-1, keepdims=True)
                logit_scr[rs, 0:near0] = pf / denom
            logit_scr[rs, near0:n_cols] = pn / denom
        for h in range(H_B):
            rs = slice(h * t_new, (h + 1) * t_new)
            acc_scr[rs, :] = _dot(logit_scr[rs, past:past + PAGE_SIZE].astype(BF16), new_head(vnew_ref, h))

    @pl.when(phase == 1)
    def _():
        for h in range(H_B):
            rs = slice(h * t_new, (h + 1) * t_new)
            acc = acc_scr[rs, :]
            for m in range(ppg):
                c0 = pl.multiple_of((g * ppg + m) * PAGE_SIZE, PAGE_SIZE)
                acc = acc + _dot(logit_scr[rs, pl.ds(c0, PAGE_SIZE)].astype(BF16), paged_head(v_pages[m], h))
            acc_scr[rs, :] = acc

        @pl.when(g == ng - 1)
        def _():
            for h in range(H_B):
                o_ref[0, :, h * DH_B:(h + 1) * DH_B] = acc_scr[h * t_new:(h + 1) * t_new, :]


def _attn_s(page_table, rel_bias, q_b, k_new, v_new, q_idx, kidx_new, misc, cache_k, cache_v, cache_idx_k,
            layer, ppg=8):
    b, t_new, _ = q_b.shape
    n_pages = page_table.shape[1]
    past = n_pages * PAGE_SIZE
    n_cols = past + PAGE_SIZE
    top_k = min(TOPK_MAX, (past + t_new) // 4)
    ng = n_pages // ppg
    hq = H_B * t_new
    near0 = max(0, (past - MAX_DIST) // PAGE_SIZE) * PAGE_SIZE

    qidx = q_idx.reshape(b, t_new, H_IDX, D_IDX).transpose(0, 2, 1, 3).reshape(b, H_IDX * t_new, D_IDX)
    widx = misc[:, :, MISC_WIDX:MISC_WIDX + H_IDX].transpose(0, 2, 1).reshape(b, H_IDX * t_new, 1)
    pad_page = lambda a: jnp.pad(a, ((0, 0), (0, PAGE_SIZE - t_new), (0, 0)))
    k_new, v_new, kidx_new_t = pad_page(k_new), pad_page(v_new), pad_page(kidx_new).transpose(0, 2, 1)
    ck = cache_k.reshape(cache_k.shape[0], cache_k.shape[1], PAGE_SIZE * H_B, DH_B)
    cv = cache_v.reshape(cache_v.shape[0], cache_v.shape[1], PAGE_SIZE * H_B, DH_B)
    cidx_t = cache_idx_k.transpose(0, 1, 3, 2)

    per_b = lambda s: pl.BlockSpec((1,) + s, lambda bi, ph, g, pt: (bi, 0, 0))

    def page_spec(rows, width, m, active_phase):
        def imap(bi, ph, g, pt):
            gg = jnp.where(ph == active_phase, g, jnp.where(ph < active_phase, 0, ng - 1))
            return (layer, pt[bi, gg * ppg + m], 0, 0)
        return pl.BlockSpec((None, None, rows, width), imap)

    in_specs = ([pl.BlockSpec(memory_space=pltpu.SMEM),
                 per_b((t_new, HB)), per_b((H_IDX * t_new, D_IDX)), per_b((H_IDX * t_new, 1)),
                 per_b((PAGE_SIZE, HB)), per_b((PAGE_SIZE, HB)), per_b((D_IDX, PAGE_SIZE))]
                + [page_spec(D_IDX, PAGE_SIZE, m, 0) for m in range(ppg)]
                + [page_spec(PAGE_SIZE * H_B, DH_B, m, 0) for m in range(ppg)]
                + [page_spec(PAGE_SIZE * H_B, DH_B, m, 1) for m in range(ppg)])
    grid_spec = pltpu.PrefetchScalarGridSpec(
        num_scalar_prefetch=1,
        grid=(b, 2, ng),
        in_specs=in_specs,
        out_specs=pl.BlockSpec((1, t_new, HB), lambda bi, ph, g, pt: (bi, 0, 0)),
        scratch_shapes=[pltpu.VMEM((t_new, n_cols), F32), pltpu.VMEM((hq, n_cols), F32),
                        pltpu.VMEM((t_new, n_cols), F32), pltpu.VMEM((hq, DH_B), F32),
                        pltpu.VMEM((H_B, t_new, n_cols - near0), F32),
                        pltpu.VMEM((LANES, LANES), BF16)])
    return pl.pallas_call(
        functools.partial(_attn_s_kernel, t_new=t_new, n_pages=n_pages, ppg=ppg, top_k=top_k),
        grid_spec=grid_spec,
        out_shape=jax.ShapeDtypeStruct((b, t_new, HB), F32),
        compiler_params=_cparams(3),
        name="attn_s",
    )(page_table, rel_bias, q_b, qidx, widx, k_new, v_new, kidx_new_t,
      *([cidx_t] * ppg), *([ck] * ppg), *([cv] * ppg))


def _ffn_kernel(x_ref, oa_ref, ob_ref, ga_ref, gb_ref, fst_ref, wpa_ref, wpb_ref, wout_ref, n2_ref,
                wup_ref, cfw_ref, cfb_ref, wdown_ref, fg_ref, y_ref, fnew_ref, prev_scr, *, tm, ch, d_ff):
    t = pl.program_id(1)
    nt = pl.num_programs(1)
    npr = CONV_F - 1

    @pl.when(t == 0)
    def _():
        prev_scr[SUBLANES - npr:SUBLANES, :] = fst_ref[0]

    pa = _dot(oa_ref[0].astype(BF16), wpa_ref[...])
    pb = _dot(ob_ref[0].astype(BF16), wpb_ref[...])
    merged = jax.nn.sigmoid(ga_ref[0]) * pa + jax.nn.sigmoid(gb_ref[0]) * pb
    x1 = x_ref[0] + _dot(merged.astype(BF16), wout_ref[...])
    xn2 = _rms(x1, n2_ref[...]).astype(BF16)

    def conv_cols(c0):
        up = _dot(xn2, wup_ref[:, c0:c0 + ch])
        prev = prev_scr[:, c0:c0 + ch]
        out = _shift_rows(up, 2, prev) * cfw_ref[0:1, c0:c0 + ch]
        out = out + _shift_rows(up, 1, prev) * cfw_ref[1:2, c0:c0 + ch]
        out = out + up * cfw_ref[2:3, c0:c0 + ch]
        prev_scr[SUBLANES - npr:SUBLANES, c0:c0 + ch] = up[tm - npr:tm, :]
        return out + cfb_ref[:, c0:c0 + ch]

    acc = jnp.zeros(x1.shape, F32)
    for c0 in range(0, d_ff, ch):
        gate = conv_cols(c0)
        val = conv_cols(d_ff + c0)
        acc = acc + _dot((_silu(gate) * val).astype(BF16), wdown_ref[c0:c0 + ch, :])
    y_ref[0] = _rms(x1 + acc, fg_ref[...])

    @pl.when(t == nt - 1)
    def _():
        fnew_ref[0] = prev_scr[SUBLANES - npr:SUBLANES, :]


def _ffn(x, o_a, o_b, gate_a, gate_b, ffn_state, wpa, wpb, wout, n2_row, wup, cfw, cfb_row, wdown, fg_row, tm):
    b, t, d = x.shape
    d_ff = wdown.shape[0]
    ch = 256
    blk = lambda w: pl.BlockSpec((1, tm, w), lambda i, j: (i, j, 0))
    per_b = pl.BlockSpec((1, CONV_F - 1, 2 * d_ff), lambda i, j: (i, 0, 0))
    consts = (wpa, wpb, wout, n2_row, wup, cfw, cfb_row, wdown, fg_row)
    return pl.pallas_call(
        functools.partial(_ffn_kernel, tm=tm, ch=ch, d_ff=d_ff),
        grid=(b, t // tm),
        in_specs=[blk(d), blk(o_a.shape[-1]), blk(o_b.shape[-1]), blk(d), blk(d), per_b]
                 + [_const_spec(c.shape) for c in consts],
        out_specs=[blk(d), per_b],
        out_shape=[jax.ShapeDtypeStruct((b, t, d), F32),
                   jax.ShapeDtypeStruct((b, CONV_F - 1, 2 * d_ff), F32)],
        scratch_shapes=[pltpu.VMEM((SUBLANES, 2 * d_ff), F32)],
        compiler_params=_cparams(2),
        name="ffn",
    )(x, o_a, o_b, gate_a, gate_b, ffn_state, *consts)


def _misc_row(vals, col0):
    return jnp.zeros((1, LANES), F32).at[0, col0:col0 + vals.shape[0]].set(vals.astype(F32))


def _layer(x, conv_state, rec_state, ffn_state, attend, lw, proj_tm, gdn_rows, ffn_tm):
    b, t, d = x.shape
    (w_in_bf, conv_a_w, alog_row, dtb_row, ng_row, wpa, wpb, wout, n1_row, n2_row, wup, cfw, cfb_row,
     wdown, fg_row) = lw
    (qkv_a, z_a, q_b, k_b, v_b, q_idx, gate_a, gate_b, misc, k_idx) = _proj(
        x.reshape(b * t, d), n1_row, w_in_bf, proj_tm)
    r3 = lambda a: a.reshape(b, t, a.shape[-1])
    qkv_a, z_a, q_b, k_b, v_b, q_idx, gate_a, gate_b, misc, k_idx = map(
        r3, (qkv_a, z_a, q_b, k_b, v_b, q_idx, gate_a, gate_b, misc, k_idx))
    o_a, conv_new, rec_new = _gdn(qkv_a, misc, z_a, conv_state, rec_state, conv_a_w, alog_row, dtb_row,
                                  ng_row, gdn_rows)
    o_b = attend(q_b, k_b, v_b, q_idx, k_idx, misc)
    y, ffn_new = _ffn(x, o_a, o_b, gate_a, gate_b, ffn_state, wpa, wpb, wout, n2_row, wup, cfw, cfb_row,
                      wdown, fg_row, ffn_tm)
    kv_shape = (b, t, H_B, DH_B)
    return y, (k_b.reshape(kv_shape), v_b.reshape(kv_shape), k_idx, conv_new, rec_new, ffn_new)


def kernel(x_prompt, x_sample, cache_k, cache_v, cache_idx_k, state_a_conv, state_a_rec, state_ffn_conv,
           page_table, w_in, conv_a_w, a_log, dt_bias, norm_a_g, w_proj_a, w_proj_b, w_out, rel_bias,
           norm1_g, norm2_g, w_up, conv_f_w, conv_f_b, w_down, final_g):
    depth = w_in.shape[0]
    assert depth == 1, "the final norm is fused into the layer's last kernel"
    bp, tp, d = x_prompt.shape
    bs, ts, _ = x_sample.shape
    d_ff = w_down.shape[1]
    l = 0
    lw = (_prep_w_in(w_in[l], d), conv_a_w[l], _misc_row(a_log[l], MISC_A), _misc_row(dt_bias[l], MISC_A),
          norm_a_g[l].reshape(1, -1), w_proj_a[l].astype(BF16), w_proj_b[l].astype(BF16),
          w_out[l].astype(BF16), norm1_g[l].reshape(1, -1), norm2_g[l].reshape(1, -1),
          w_up[l].astype(BF16), conv_f_w[l], conv_f_b[l].reshape(1, -1), w_down[l].astype(BF16),
          final_g.reshape(1, -1))

    def attend_prompt(q_b, k_b, v_b, q_idx, k_idx, misc):
        return _attn_p(rel_bias, q_b, k_b, v_b, q_idx, misc)

    def attend_sample(q_b, k_b, v_b, q_idx, k_idx, misc):
        return _attn_s(page_table, rel_bias, q_b, k_b, v_b, q_idx, k_idx, misc, cache_k, cache_v,
                       cache_idx_k, l)

    zeros = lambda *s: jnp.zeros(s, x_prompt.dtype)
    y_p, st_p = _layer(x_prompt, zeros(bp, CONV_A - 1, QKV_A), zeros(bp, H_A, DK_A, DV_A),
                       zeros(bp, CONV_F - 1, 2 * d_ff), attend_prompt, lw,
                       proj_tm=256, gdn_rows=min(tp, 256), ffn_tm=min(tp, 256))
    y_s, st_s = _layer(x_sample, state_a_conv[l], state_a_rec[l], state_ffn_conv[l], attend_sample, lw,
                       proj_tm=min(bs * ts, 256), gdn_rows=ts, ffn_tm=ts)
    stack = lambda a: a[None]
    return (y_p, y_s) + tuple(map(stack, st_p)) + tuple(map(stack, st_s))
```

```python
import functools
import math

import jax
import jax.numpy as jnp
from jax import lax
from jax.experimental import pallas as pl
from jax.experimental.pallas import tpu as pltpu

F32 = jnp.float32
BF16 = jnp.bfloat16
HI = lax.Precision.HIGHEST

H_A, DK_A, DV_A, CONV_A, CHUNK_A = 4, 128, 128, 4, 64
H_B, DH_B, H_IDX, D_IDX = 4, 128, 8, 64
TOPK_MAX, N_BUCKETS, MAX_DIST = 256, 32, 2048
CONV_F, EPS, PAGE_SIZE = 3, 1e-6, 128
QKV_A = H_A * (2 * DK_A + DV_A)
HB = H_B * DH_B

LANES = 128
SUBLANES = 8
VMEM_LIMIT = 56 * 1024 * 1024

MISC_KIDX, MISC_BETA, MISC_A, MISC_WIDX = 0, D_IDX, D_IDX + H_A, D_IDX + 2 * H_A

GDN_MODE_QK = "bf"
GDN_MODE_INV = "bf"
GDN_MODE_STATE = "bf"

INT_MIN = -2 ** 31
NEG_INF = float("-inf")


def _cparams(n_axes):
    return pltpu.CompilerParams(dimension_semantics=("arbitrary",) * n_axes,
                                vmem_limit_bytes=VMEM_LIMIT)


def _const_spec(shape):
    nd = len(shape)
    return pl.BlockSpec(shape, lambda *_: (0,) * nd, pipeline_mode=pl.Buffered(1))


def _silu(x):
    return x * jax.nn.sigmoid(x)


def _dot(a, b, precision=None):
    return jnp.dot(a, b, preferred_element_type=F32, precision=precision)


def _dot_nt(a, b, precision=None):
    return lax.dot_general(a, b, (((1,), (1,)), ((), ())), preferred_element_type=F32,
                           precision=precision)


_DIMS = {"nn": (((1,), (0,)), ((), ())), "nt": (((1,), (1,)), ((), ())), "tn": (((0,), (0,)), ((), ()))}


def _mm(a, b, mode, form="nn"):
    if mode == "bf":
        return lax.dot_general(a.astype(BF16), b.astype(BF16), _DIMS[form], preferred_element_type=F32)
    return lax.dot_general(a, b, _DIMS[form], preferred_element_type=F32, precision=HI)


def _cumsum_rows(x):
    row = lax.broadcasted_iota(jnp.int32, x.shape, 0)
    s = 1
    while s < x.shape[0]:
        x = x + jnp.where(row >= s, pltpu.roll(x, s, 0), 0.0)
        s *= 2
    return x


def _rms(x, g):
    return x * lax.rsqrt(jnp.mean(x * x, axis=-1, keepdims=True) + EPS) * g


def _shift_rows(x, k, prev):
    n = prev.shape[0]
    out = pltpu.roll(x, k, 0)
    head = out[0:SUBLANES]
    row = lax.broadcasted_iota(jnp.int32, head.shape, 0)
    for r in range(k):
        head = jnp.where(row == r, prev[n - k + r:n - k + r + 1, :], head)
    return head if x.shape[0] == SUBLANES else jnp.concatenate([head, out[SUBLANES:]], axis=0)


def _proj_kernel(x_ref, g_ref, w_ref, qkva_ref, za_ref, qb_ref, kb_ref, vb_ref, qidx_ref,
                 ga_ref, gb_ref, misc_ref, kidx_ref):
    xn = _rms(x_ref[...], g_ref[...]).astype(BF16)
    off = 0
    for o_ref in (qkva_ref, za_ref, qb_ref, kb_ref, vb_ref, qidx_ref, ga_ref, gb_ref, misc_ref):
        wd = o_ref.shape[-1]
        o_ref[...] = _dot(xn, w_ref[:, off:off + wd])
        off += wd
    kidx_ref[...] = misc_ref[:, MISC_KIDX:MISC_KIDX + D_IDX]


def _prep_w_in(w_in_l, d_model):
    sizes = (QKV_A, H_A, H_A, H_A * DV_A, 3 * HB, H_IDX * D_IDX, D_IDX, H_IDX, d_model, d_model)
    parts, start = [], 0
    for s in sizes:
        parts.append(w_in_l[:, start:start + s])
        start += s
    qkv_a, beta, a_raw, z_a, qkv_b, q_idx, k_idx, w_idx, gate_a, gate_b = parts
    pad = jnp.zeros((w_in_l.shape[0], LANES - D_IDX - 2 * H_A - H_IDX), w_in_l.dtype)
    misc = jnp.concatenate([k_idx, beta, a_raw, w_idx, pad], axis=1)
    return jnp.concatenate([qkv_a, z_a, qkv_b, q_idx, gate_a, gate_b, misc], axis=1).astype(BF16)


def _proj(x2d, g_row, w_bf, tm):
    n, d = x2d.shape
    widths = (QKV_A, H_A * DV_A, HB, HB, HB, H_IDX * D_IDX, d, d, LANES, D_IDX)
    row = lambda i: (i, 0)
    return pl.pallas_call(
        _proj_kernel,
        grid=(n // tm,),
        in_specs=[pl.BlockSpec((tm, d), row), _const_spec(g_row.shape), _const_spec(w_bf.shape)],
        out_specs=[pl.BlockSpec((tm, wd), row) for wd in widths],
        out_shape=[jax.ShapeDtypeStruct((n, wd), F32) for wd in widths],
        compiler_params=_cparams(1),
        name="proj",
    )(x2d, g_row, w_bf)


def _gdn_kernel(qkv_ref, misc_ref, z_ref, cst_ref, rec0_ref, cw_ref, alog_ref, dtb_ref, ng_ref,
                o_ref, cnew_ref, rnew_ref, s_scr, prev_scr, *, rows, chunk):
    t = pl.program_id(1)
    nt = pl.num_programs(1)
    npr = CONV_A - 1

    @pl.when(t == 0)
    def _():
        s_scr[...] = rec0_ref[0]
        prev_scr[SUBLANES - npr:SUBLANES, :] = cst_ref[0]

    x = qkv_ref[0]
    prev = prev_scr[...]
    conv = _shift_rows(x, 3, prev) * cw_ref[0:1, :]
    conv = conv + _shift_rows(x, 2, prev) * cw_ref[1:2, :]
    conv = conv + _shift_rows(x, 1, prev) * cw_ref[2:3, :]
    conv = conv + x * cw_ref[3:4, :]
    prev_scr[SUBLANES - npr:SUBLANES, :] = x[rows - npr:rows, :]

    @pl.when(t == nt - 1)
    def _():
        cnew_ref[0] = x[rows - npr:rows, :]

    act = _silu(conv)
    misc = misc_ref[0]
    beta_full = jax.nn.sigmoid(misc)
    sp = misc + dtb_ref[...]
    softplus = jnp.maximum(sp, 0.0) + jnp.log1p(jnp.exp(-jnp.abs(sp)))
    g_full = -jnp.exp(alog_ref[...]) * softplus

    ri = lax.broadcasted_iota(jnp.int32, (chunk, chunk), 0)
    ci = lax.broadcasted_iota(jnp.int32, (chunk, chunk), 1)
    tril = ri >= ci
    strict = ri > ci
    eye = (ri == ci).astype(F32)
    lane = lax.broadcasted_iota(jnp.int32, (chunk, LANES), 1)
    n_levels = int(math.log2(chunk))
    hk = H_A * DK_A
    chunks = list(range(0, rows, chunk))
    pairs = [(c0, h) for c0 in chunks for h in range(H_A)]

    gcum = {c0: _cumsum_rows(g_full[c0:c0 + chunk, :]) for c0 in chunks}
    gcum_t = {c0: gcum[c0].T for c0 in chunks} if chunk >= 64 else None

    pre = {}
    for c0, h in pairs:
        q = act[c0:c0 + chunk, h * DK_A:(h + 1) * DK_A]
        k = act[c0:c0 + chunk, hk + h * DK_A:hk + (h + 1) * DK_A]
        v = act[c0:c0 + chunk, 2 * hk + h * DV_A:2 * hk + (h + 1) * DV_A]
        q = q * lax.rsqrt(jnp.sum(q * q, axis=-1, keepdims=True) + EPS) * (DK_A ** -0.5)
        k = k * lax.rsqrt(jnp.sum(k * k, axis=-1, keepdims=True) + EPS)
        beta = beta_full[c0:c0 + chunk, MISC_BETA + h:MISC_BETA + h + 1]
        gc = gcum[c0][:, MISC_A + h:MISC_A + h + 1]
        if gcum_t is not None:
            g_row = gcum_t[c0][MISC_A + h:MISC_A + h + 1, :]
        else:
            g_row = _mm((lane == MISC_A + h).astype(F32), gcum[c0], "hi", "nt")
        gamma = jnp.where(tril, jnp.exp(jnp.where(tril, gc - g_row, 0.0)), 0.0)
        kb = k * beta
        egc = jnp.exp(gc)
        g_last = gc[chunk - 1:chunk, :]
        pre[c0, h] = dict(
            xp=-jnp.where(strict, _mm(kb, k, GDN_MODE_QK, "nt") * gamma, 0.0),
            rhs=jnp.concatenate([kb * egc, v * beta], axis=-1),
            attn=jnp.where(tril, _mm(q, k, GDN_MODE_QK, "nt") * gamma, 0.0),
            q_dec=q * egc, k_dec=k * jnp.exp(g_last - gc), e_last=jnp.exp(g_last))
    for p in pre.values():
        p["inv"] = eye + p["xp"]
    for _ in range(n_levels - 1):
        for p in pre.values():
            p["xp"] = _mm(p["xp"], p["xp"], GDN_MODE_INV)
        for p in pre.values():
            p["inv"] = p["inv"] + _mm(p["inv"], p["xp"], GDN_MODE_INV)
    for p in pre.values():
        p["sol"] = _mm(p["inv"], p["rhs"], GDN_MODE_INV)

    for c0, h in pairs:
        p = pre[c0, h]
        s = s_scr[h]
        v_new = p["sol"][:, DK_A:] - _mm(p["sol"][:, :DK_A], s, GDN_MODE_STATE)
        o = _mm(p["q_dec"], s, GDN_MODE_STATE) + _mm(p["attn"], v_new, GDN_MODE_STATE)
        s_scr[h] = s * p["e_last"] + _mm(p["k_dec"], v_new, GDN_MODE_STATE, "tn")
        z = z_ref[0, c0:c0 + chunk, h * DV_A:(h + 1) * DV_A]
        o_ref[0, c0:c0 + chunk, h * DV_A:(h + 1) * DV_A] = _rms(o, ng_ref[...]) * _silu(z)

    @pl.when(t == nt - 1)
    def _():
        rnew_ref[0] = s_scr[...]


def _gdn(qkv_a, misc, z_a, conv_state, rec0, conv_w, alog_row, dtb_row, ng_row, rows):
    b, t, _ = qkv_a.shape
    chunk = math.gcd(t, CHUNK_A)
    blk = lambda w: pl.BlockSpec((1, rows, w), lambda i, j: (i, j, 0))
    per_b3 = lambda s: pl.BlockSpec((1,) + s, lambda i, j: (i, 0, 0))
    per_b4 = lambda s: pl.BlockSpec((1,) + s, lambda i, j: (i, 0, 0, 0))
    return pl.pallas_call(
        functools.partial(_gdn_kernel, rows=rows, chunk=chunk),
        grid=(b, t // rows),
        in_specs=[blk(QKV_A), blk(LANES), blk(H_A * DV_A),
                  per_b3((CONV_A - 1, QKV_A)), per_b4((H_A, DK_A, DV_A)),
                  _const_spec(conv_w.shape), _const_spec(alog_row.shape),
                  _const_spec(dtb_row.shape), _const_spec(ng_row.shape)],
        out_specs=[blk(H_A * DV_A), per_b3((CONV_A - 1, QKV_A)), per_b4((H_A, DK_A, DV_A))],
        out_shape=[jax.ShapeDtypeStruct((b, t, H_A * DV_A), F32),
                   jax.ShapeDtypeStruct((b, CONV_A - 1, QKV_A), F32),
                   jax.ShapeDtypeStruct((b, H_A, DK_A, DV_A), F32)],
        scratch_shapes=[pltpu.VMEM((H_A, DK_A, DV_A), F32), pltpu.VMEM((SUBLANES, QKV_A), F32)],
        compiler_params=_cparams(2),
        name="gdn",
    )(qkv_a, misc, z_a, conv_state, rec0, conv_w, alog_row, dtb_row, ng_row)


def _rel_bias_tile(dist, bias_ref):
    nd = jnp.maximum(dist, 0)
    max_exact = N_BUCKETS // 2
    nf = jnp.maximum(nd, max_exact).astype(F32)
    large = max_exact + (jnp.log(nf / max_exact) / math.log(MAX_DIST / max_exact)
                         * (N_BUCKETS - max_exact)).astype(jnp.int32)
    large = jnp.minimum(large, N_BUCKETS - 1)
    bucket = jnp.where(nd < max_exact, nd, large)
    outs = []
    for h in range(H_B):
        val = jnp.full(dist.shape, bias_ref[0, h], F32)
        for bk in range(1, N_BUCKETS):
            val = jnp.where(bucket == bk, bias_ref[bk, h], val)
        outs.append(val)
    return outs


def _order_key(score, admissible):
    bits = pltpu.bitcast(score, jnp.int32)
    key = jnp.where(bits < 0, bits ^ jnp.int32(0x7FFFFFFF), bits)
    key = jnp.where(bits == jnp.int32(INT_MIN), 0, key)
    return jnp.where(admissible, key, jnp.int32(INT_MIN))


def _count(m):
    x = m.astype(F32)
    parts = [x[:, j:j + LANES] for j in range(0, x.shape[1], LANES)]
    while len(parts) > 1:
        odd = [parts[-1]] if len(parts) % 2 else []
        parts = [a + b for a, b in zip(parts[0::2], parts[1::2])] + odd
    return jnp.sum(parts[0], axis=-1, keepdims=True)


def _kth_largest(key, top_k):
    rows = key.shape[0]

    def body(i, theta):
        inc = lax.shift_left(jnp.int32(1), jnp.int32(31) - i)
        cand = theta + inc
        cnt = _count(key >= cand)
        return jnp.where(cnt >= top_k, cand, theta)

    return lax.fori_loop(0, 32, body, jnp.full((rows, 1), INT_MIN, jnp.int32))


def _select_topk(key, admissible, top_k, mask_ref, tri_ref):
    theta = _kth_largest(key, top_k)
    gt = key > theta
    eq = key == theta
    need = top_k - _count(gt)
    all_row = theta == jnp.int32(INT_MIN)
    simple = jnp.logical_or(_count(eq) == need, all_row)
    n_bad = jnp.sum(jnp.where(simple, 0.0, 1.0))

    @pl.when(n_bad == 0.0)
    def _():
        mask_ref[...] = jnp.logical_and(key >= theta, admissible).astype(F32)

    @pl.when(n_bad != 0.0)
    def _():
        run = jnp.zeros((key.shape[0], 1), F32)
        for c0 in range(0, key.shape[1], LANES):
            eq_c = eq[:, c0:c0 + LANES].astype(F32)
            before = _dot(eq_c.astype(BF16), tri_ref[...]) + run
            take = jnp.logical_and(eq[:, c0:c0 + LANES], before < need)
            m = jnp.logical_and(jnp.logical_or(gt[:, c0:c0 + LANES], take), admissible[:, c0:c0 + LANES])
            mask_ref[:, c0:c0 + LANES] = m.astype(F32)
            run = run + jnp.sum(eq_c, axis=-1, keepdims=True)


def _init_tri(tri_ref):
    r = lax.broadcasted_iota(jnp.int32, (LANES, LANES), 0)
    c = lax.broadcasted_iota(jnp.int32, (LANES, LANES), 1)
    tri_ref[...] = (r < c).astype(BF16)


def _attn_p_kernel(bias_ref, qb_ref, kb_ref, vb_ref, qidx_ref, miscq_ref, misck_ref, o_ref,
                   k_scr, v_scr, kidx_scr, btab_scr, mask_scr, tri_scr, *, tq, top_k, n_groups):
    b = pl.program_id(0)
    i = pl.program_id(1)
    nq = pl.num_programs(1)

    @pl.when(jnp.logical_and(b == 0, i == 0))
    def _():
        _init_tri(tri_scr)
        r = lax.broadcasted_iota(jnp.int32, (tq, tq), 0)
        c = lax.broadcasted_iota(jnp.int32, (tq, tq), 1)
        for m in range(2 * nq - 1):
            dd = nq - 1 - m
            if dd >= 0:
                tiles = _rel_bias_tile(dd * tq + r - c, bias_ref)
                for h in range(H_B):
                    btab_scr[h, :, m * tq:(m + 1) * tq] = tiles[h]
            else:
                for h in range(H_B):
                    btab_scr[h, :, m * tq:(m + 1) * tq] = jnp.zeros((tq, tq), F32)

    @pl.when(i == 0)
    def _():
        k_scr[...] = kb_ref[0].astype(BF16)
        v_scr[...] = vb_ref[0].astype(BF16)
        kidx_scr[...] = misck_ref[0, :, MISC_KIDX:MISC_KIDX + D_IDX].astype(BF16)

    def attend(n_keys):
        qidx = qidx_ref[0]
        miscq = miscq_ref[0]
        kidx = kidx_scr[0:n_keys, :]
        score = jnp.zeros((tq, n_keys), F32)
        for h in range(H_IDX):
            dots = _dot_nt(qidx[:, h * D_IDX:(h + 1) * D_IDX].astype(BF16), kidx) * (D_IDX ** -0.5)
            w = miscq[:, MISC_WIDX + h:MISC_WIDX + h + 1] * (H_IDX ** -0.5)
            score = score + jnp.maximum(dots, 0.0) * w

        qpos = i * tq + lax.broadcasted_iota(jnp.int32, (tq, n_keys), 0)
        kpos = lax.broadcasted_iota(jnp.int32, (tq, n_keys), 1)
        admissible = kpos <= qpos
        key = _order_key(score, admissible)
        mask_ref = mask_scr.at[:, 0:n_keys]
        _select_topk(key, admissible, top_k, mask_ref, tri_scr)

        sel = mask_ref[...] > 0.5
        qb = qb_ref[0]
        boff = pl.multiple_of((nq - 1 - i) * tq, LANES)
        for h in range(H_B):
            qh = qb[:, h * DH_B:(h + 1) * DH_B].astype(BF16)
            logits = _dot_nt(qh, k_scr[0:n_keys, h * DH_B:(h + 1) * DH_B]) * (DH_B ** -0.5)
            logits = logits + btab_scr[h, :, pl.ds(boff, n_keys)]
            logits = jnp.where(sel, logits, NEG_INF)
            m = jnp.max(logits, axis=-1, keepdims=True)
            p = jnp.exp(logits - m)
            denom = jnp.sum(p, axis=-1, keepdims=True)
            out = _dot(p.astype(BF16), v_scr[0:n_keys, h * DH_B:(h + 1) * DH_B])
            o_ref[0, :, h * DH_B:(h + 1) * DH_B] = out / denom

    per_group = nq // n_groups
    for grp in range(n_groups):
        pl.when(i // per_group == grp)(functools.partial(attend, (grp + 1) * per_group * tq))


def _attn_p(rel_bias, q_b, k_b, v_b, q_idx, misc, tq=128):
    b, t, _ = q_b.shape
    nq = t // tq
    top_k = min(TOPK_MAX, t // 4)
    n_groups = math.gcd(nq, 4)
    qblk = lambda w: pl.BlockSpec((1, tq, w), lambda bi, i: (bi, i, 0))
    full = lambda w: pl.BlockSpec((1, t, w), lambda bi, i: (bi, 0, 0))
    return pl.pallas_call(
        functools.partial(_attn_p_kernel, tq=tq, top_k=top_k, n_groups=n_groups),
        grid=(b, nq),
        in_specs=[pl.BlockSpec(memory_space=pltpu.SMEM),
                  qblk(HB), full(HB), full(HB), qblk(H_IDX * D_IDX), qblk(LANES), full(LANES)],
        out_specs=qblk(HB),
        out_shape=jax.ShapeDtypeStruct((b, t, HB), F32),
        scratch_shapes=[pltpu.VMEM((t, HB), BF16), pltpu.VMEM((t, HB), BF16),
                        pltpu.VMEM((t, D_IDX), BF16),
                        pltpu.VMEM((H_B, tq, (2 * nq - 1) * tq), F32),
                        pltpu.VMEM((tq, t), F32), pltpu.VMEM((LANES, LANES), BF16)],
        compiler_params=_cparams(2),
        name="attn_p",
    )(rel_bias, q_b, k_b, v_b, q_idx, misc, misc)


def _attn_s_kernel(pt_ref, bias_ref, qb_ref, qidx_ref, widx_ref, knew_ref, vnew_ref, kidxnew_ref,
                   cidx_hbm, ck_hbm, cv_hbm, o_ref,
                   score_scr, logit_scr, mask_scr, acc_scr, btab_scr, tri_scr, idx_buf, k_buf, v_buf, sems,
                   *, layer, t_new, n_pages, gp, top_k):
    b = pl.program_id(0)
    n_grp = n_pages // gp
    past = n_pages * PAGE_SIZE
    n_cols = past + PAGE_SIZE
    near0 = max(0, (past - MAX_DIST) // PAGE_SIZE) * PAGE_SIZE
    n_near = n_cols - near0
    pg_rows = PAGE_SIZE * H_B
    grp_keys = gp * PAGE_SIZE

    @pl.when(b == 0)
    def _():
        _init_tri(tri_scr)
        trow = lax.broadcasted_iota(jnp.int32, (t_new, n_near), 0)
        col = near0 + lax.broadcasted_iota(jnp.int32, (t_new, n_near), 1)
        tiles = _rel_bias_tile(past + trow - col, bias_ref)
        for h in range(H_B):
            btab_scr[h] = tiles[h]

    def group_copies(hbm, buf, sem_row, slot, grp):
        out = []
        for m in range(gp):
            page = pt_ref[b, grp * gp + m]
            dst = buf.at[slot, m] if buf is idx_buf else buf.at[slot, pl.ds(m * pg_rows, pg_rows)]
            out.append(pltpu.make_async_copy(hbm.at[layer, page], dst, sems.at[sem_row, slot]))
        return out

    def start_idx_k(slot, grp):
        for c in group_copies(cidx_hbm, idx_buf, 0, slot, grp) + group_copies(ck_hbm, k_buf, 1, slot, grp):
            c.start()

    def wait_idx_k(slot, grp):
        for c in group_copies(cidx_hbm, idx_buf, 0, slot, grp) + group_copies(ck_hbm, k_buf, 1, slot, grp):
            c.wait()

    def start_v(slot, grp):
        for c in group_copies(cv_hbm, v_buf, 2, slot, grp):
            c.start()

    def wait_v(slot, grp):
        for c in group_copies(cv_hbm, v_buf, 2, slot, grp):
            c.wait()

    qb = qb_ref[0].astype(BF16)
    qidx = qidx_ref[0].astype(BF16)
    widx = widx_ref[0] * (H_IDX ** -0.5)
    head_rows = lambda h: slice(h * t_new, (h + 1) * t_new)
    head_lanes = lambda h: slice(h * DH_B, (h + 1) * DH_B)

    def scores_of(kidx_t):
        dots = _dot(qidx, kidx_t.astype(BF16)) * (D_IDX ** -0.5)
        contrib = jnp.maximum(dots, 0.0) * widx
        s = contrib[0:t_new]
        for h in range(1, H_IDX):
            s = s + contrib[h * t_new:(h + 1) * t_new]
        return s

    start_idx_k(0, 0)

    def pass1(grp, carry):
        slot = grp % 2

        @pl.when(grp + 1 < n_grp)
        def _():
            start_idx_k(1 - slot, grp + 1)

        wait_idx_k(slot, grp)
        c0 = pl.multiple_of(grp * grp_keys, grp_keys)
        for m in range(gp):
            score_scr[:, pl.ds(c0 + m * PAGE_SIZE, PAGE_SIZE)] = scores_of(idx_buf[slot, m])
        for h in range(H_B):
            k_h = k_buf[slot, pl.ds(h, grp_keys, stride=H_B), :].astype(BF16)
            logit_scr[head_rows(h), pl.ds(c0, grp_keys)] = _dot_nt(qb[:, head_lanes(h)], k_h)
        return carry

    lax.fori_loop(0, n_grp, pass1, 0)
    start_v(0, 0)

    score_scr[:, past:n_cols] = scores_of(kidxnew_ref[0])
    for h in range(H_B):
        k_h = knew_ref[0, :, head_lanes(h)].astype(BF16)
        logit_scr[head_rows(h), past:n_cols] = _dot_nt(qb[:, head_lanes(h)], k_h)

    trow = lax.broadcasted_iota(jnp.int32, (t_new, n_cols), 0)
    col = lax.broadcasted_iota(jnp.int32, (t_new, n_cols), 1)
    admissible = col <= past + trow
    key = _order_key(score_scr[...], admissible)
    _select_topk(key, admissible, top_k, mask_scr, tri_scr)
    sel = mask_scr[...] > 0.5
    for h in range(H_B):
        rs = head_rows(h)
        near = logit_scr[rs, near0:n_cols] * (DH_B ** -0.5) + btab_scr[h]
        near = jnp.where(sel[:, near0:n_cols], near, NEG_INF)
        mx = jnp.max(near, axis=-1, keepdims=True)
        if near0 > 0:
            far = logit_scr[rs, 0:near0] * (DH_B ** -0.5) + bias_ref[N_BUCKETS - 1, h]
            far = jnp.where(sel[:, 0:near0], far, NEG_INF)
            mx = jnp.maximum(mx, jnp.max(far, axis=-1, keepdims=True))
        pn = jnp.exp(near - mx)
        denom = jnp.sum(pn, axis=-1, keepdims=True)
        if near0 > 0:
            pf = jnp.exp(far - mx)
            denom = denom + jnp.sum(pf, axis=-1, keepdims=True)
            logit_scr[rs, 0:near0] = pf / denom
        logit_scr[rs, near0:n_cols] = pn / denom
        v_h = vnew_ref[0, :, head_lanes(h)].astype(BF16)
        acc_scr[rs, :] = _dot(logit_scr[rs, past:n_cols].astype(BF16), v_h)

    def pass2(grp, carry):
        slot = grp % 2

        @pl.when(grp + 1 < n_grp)
        def _():
            start_v(1 - slot, grp + 1)

        wait_v(slot, grp)
        c0 = pl.multiple_of(grp * grp_keys, grp_keys)
        for h in range(H_B):
            v_h = v_buf[slot, pl.ds(h, grp_keys, stride=H_B), :].astype(BF16)
            p_h = logit_scr[head_rows(h), pl.ds(c0, grp_keys)].astype(BF16)
            acc_scr[head_rows(h), :] += _dot(p_h, v_h)
        return carry

    lax.fori_loop(0, n_grp, pass2, 0)
    for h in range(H_B):
        o_ref[0, :, head_lanes(h)] = acc_scr[head_rows(h), :]


def _attn_s(page_table, rel_bias, q_b, k_new, v_new, q_idx, kidx_new, misc, cache_k, cache_v, cache_idx_k,
            layer):
    b, t_new, _ = q_b.shape
    n_pages = page_table.shape[1]
    past = n_pages * PAGE_SIZE
    n_cols = past + PAGE_SIZE
    top_k = min(TOPK_MAX, (past + t_new) // 4)
    gp = math.gcd(n_pages, 16)
    hq = H_B * t_new
    near0 = max(0, (past - MAX_DIST) // PAGE_SIZE) * PAGE_SIZE

    qidx = q_idx.reshape(b, t_new, H_IDX, D_IDX).transpose(0, 2, 1, 3).reshape(b, H_IDX * t_new, D_IDX)
    widx = misc[:, :, MISC_WIDX:MISC_WIDX + H_IDX].transpose(0, 2, 1).reshape(b, H_IDX * t_new, 1)
    pad_page = lambda a: jnp.pad(a, ((0, 0), (0, PAGE_SIZE - t_new), (0, 0)))
    k_new, v_new, kidx_new_t = pad_page(k_new), pad_page(v_new), pad_page(kidx_new).transpose(0, 2, 1)
    ck = cache_k.reshape(cache_k.shape[0], cache_k.shape[1], PAGE_SIZE * H_B, DH_B)
    cv = cache_v.reshape(cache_v.shape[0], cache_v.shape[1], PAGE_SIZE * H_B, DH_B)
    cidx_t = cache_idx_k.transpose(0, 1, 3, 2)

    per_b = lambda s: pl.BlockSpec((1,) + s, lambda bi, pt: (bi, 0, 0))
    hbm = pl.BlockSpec(memory_space=pl.ANY)
    in_specs = [pl.BlockSpec(memory_space=pltpu.SMEM),
                per_b((t_new, HB)), per_b((H_IDX * t_new, D_IDX)), per_b((H_IDX * t_new, 1)),
                per_b((PAGE_SIZE, HB)), per_b((PAGE_SIZE, HB)), per_b((D_IDX, PAGE_SIZE)),
                hbm, hbm, hbm]
    grid_spec = pltpu.PrefetchScalarGridSpec(
        num_scalar_prefetch=1,
        grid=(b,),
        in_specs=in_specs,
        out_specs=pl.BlockSpec((1, t_new, HB), lambda bi, pt: (bi, 0, 0)),
        scratch_shapes=[pltpu.VMEM((t_new, n_cols), F32), pltpu.VMEM((hq, n_cols), F32),
                        pltpu.VMEM((t_new, n_cols), F32), pltpu.VMEM((hq, DH_B), F32),
                        pltpu.VMEM((H_B, t_new, n_cols - near0), F32),
                        pltpu.VMEM((LANES, LANES), BF16),
                        pltpu.VMEM((2, gp, D_IDX, PAGE_SIZE), F32),
                        pltpu.VMEM((2, gp * PAGE_SIZE * H_B, DH_B), F32),
                        pltpu.VMEM((2, gp * PAGE_SIZE * H_B, DH_B), F32),
                        pltpu.SemaphoreType.DMA((3, 2))])
    return pl.pallas_call(
        functools.partial(_attn_s_kernel, layer=layer, t_new=t_new, n_pages=n_pages, gp=gp, top_k=top_k),
        grid_spec=grid_spec,
        out_shape=jax.ShapeDtypeStruct((b, t_new, HB), F32),
        compiler_params=_cparams(1),
        name="attn_s",
    )(page_table, rel_bias, q_b, qidx, widx, k_new, v_new, kidx_new_t, cidx_t, ck, cv)


def _ffn_kernel(x_ref, oa_ref, ob_ref, ga_ref, gb_ref, fst_ref, wpa_ref, wpb_ref, wout_ref, n2_ref,
                wup_ref, cfw_ref, cfb_ref, wdown_ref, fg_ref, y_ref, fnew_ref, prev_scr, *, tm, ch, d_ff, seq):
    t = pl.program_id(1)
    nt = pl.num_programs(1)
    npr = CONV_F - 1
    if seq is None:
        @pl.when(t == 0)
        def _():
            prev_scr[SUBLANES - npr:SUBLANES, :] = fst_ref[0]

    pa = _dot(oa_ref[0].astype(BF16), wpa_ref[...])
    pb = _dot(ob_ref[0].astype(BF16), wpb_ref[...])
    merged = jax.nn.sigmoid(ga_ref[0]) * pa + jax.nn.sigmoid(gb_ref[0]) * pb
    x1 = x_ref[0] + _dot(merged.astype(BF16), wout_ref[...])
    xn2 = _rms(x1, n2_ref[...]).astype(BF16)

    def conv_cols(c0):
        cols = slice(c0, c0 + ch)
        up = _dot(xn2, wup_ref[:, cols])
        if seq is None:
            prev = prev_scr[:, cols]
            s2, s1 = _shift_rows(up, 2, prev), _shift_rows(up, 1, prev)
            prev_scr[SUBLANES - npr:SUBLANES, cols] = up[tm - npr:tm, :]
        else:
            hist = fst_ref[0, :, cols]
            tpos = lax.broadcasted_iota(jnp.int32, up.shape, 0) & (seq - 1)
            s2 = jnp.where(tpos < 2, hist, pltpu.roll(up, 2, 0))
            s1 = jnp.where(tpos < 1, pltpu.roll(hist, tm - 1, 0), pltpu.roll(up, 1, 0))
            fnew_ref[:, :, cols] = up.reshape(tm // seq, seq, ch)[:, seq - npr:seq, :]
        out = s2 * cfw_ref[0:1, cols]
        out = out + s1 * cfw_ref[1:2, cols]
        out = out + up * cfw_ref[2:3, cols]
        return out + cfb_ref[:, cols]

    acc = jnp.zeros(x1.shape, F32)
    for c0 in range(0, d_ff, ch):
        gate = conv_cols(c0)
        val = conv_cols(d_ff + c0)
        acc = acc + _dot((_silu(gate) * val).astype(BF16), wdown_ref[c0:c0 + ch, :])
    y_ref[0] = _rms(x1 + acc, fg_ref[...])

    if seq is None:
        @pl.when(t == nt - 1)
        def _():
            fnew_ref[0] = prev_scr[SUBLANES - npr:SUBLANES, :]


def _ffn(x, o_a, o_b, gate_a, gate_b, ffn_state, wpa, wpb, wout, n2_row, wup, cfw, cfb_row, wdown, fg_row, tm):
    b, t, d = x.shape
    d_ff = wdown.shape[0]
    ch = 256
    npr = CONV_F - 1
    consts = (wpa, wpb, wout, n2_row, wup, cfw, cfb_row, wdown, fg_row)
    if t == SUBLANES and t > npr:
        seq, rows = t, b * t
        flat = lambda a: a.reshape(1, rows, a.shape[-1])
        x, o_a, o_b, gate_a, gate_b = map(flat, (x, o_a, o_b, gate_a, gate_b))
        hist = flat(jnp.pad(ffn_state, ((0, 0), (0, t - npr), (0, 0))))
        grid, tm = (1, 1), rows
        blk = lambda w: pl.BlockSpec((1, rows, w), lambda i, j: (0, 0, 0))
        state_in, state_out = blk(2 * d_ff), pl.BlockSpec((b, npr, 2 * d_ff), lambda i, j: (0, 0, 0))
    else:
        seq, hist = None, ffn_state
        grid = (b, t // tm)
        blk = lambda w: pl.BlockSpec((1, tm, w), lambda i, j: (i, j, 0))
        state_in = state_out = pl.BlockSpec((1, npr, 2 * d_ff), lambda i, j: (i, 0, 0))
    y, ffn_new = pl.pallas_call(
        functools.partial(_ffn_kernel, tm=tm, ch=ch, d_ff=d_ff, seq=seq),
        grid=grid,
        in_specs=[blk(d), blk(o_a.shape[-1]), blk(o_b.shape[-1]), blk(d), blk(d), state_in]
                 + [_const_spec(c.shape) for c in consts],
        out_specs=[blk(d), state_out],
        out_shape=[jax.ShapeDtypeStruct(x.shape, F32),
                   jax.ShapeDtypeStruct((b, npr, 2 * d_ff), F32)],
        scratch_shapes=[pltpu.VMEM((SUBLANES, 2 * d_ff), F32)],
        compiler_params=_cparams(2),
        name="ffn",
    )(x, o_a, o_b, gate_a, gate_b, hist, *consts)
    return y.reshape(b, t, d), ffn_new


def _misc_row(vals, col0):
    return jnp.zeros((1, LANES), F32).at[0, col0:col0 + vals.shape[0]].set(vals.astype(F32))


def _layer(x, conv_state, rec_state, ffn_state, attend, lw, proj_tm, gdn_rows, ffn_tm):
    b, t, d = x.shape
    (w_in_bf, conv_a_w, alog_row, dtb_row, ng_row, wpa, wpb, wout, n1_row, n2_row, wup, cfw, cfb_row,
     wdown, fg_row) = lw
    (qkv_a, z_a, q_b, k_b, v_b, q_idx, gate_a, gate_b, misc, k_idx) = _proj(
        x.reshape(b * t, d), n1_row, w_in_bf, proj_tm)
    r3 = lambda a: a.reshape(b, t, a.shape[-1])
    qkv_a, z_a, q_b, k_b, v_b, q_idx, gate_a, gate_b, misc, k_idx = map(
        r3, (qkv_a, z_a, q_b, k_b, v_b, q_idx, gate_a, gate_b, misc, k_idx))
    o_a, conv_new, rec_new = _gdn(qkv_a, misc, z_a, conv_state, rec_state, conv_a_w, alog_row, dtb_row,
                                  ng_row, gdn_rows)
    o_b = attend(q_b, k_b, v_b, q_idx, k_idx, misc)
    y, ffn_new = _ffn(x, o_a, o_b, gate_a, gate_b, ffn_state, wpa, wpb, wout, n2_row, wup, cfw, cfb_row,
                      wdown, fg_row, ffn_tm)
    kv_shape = (b, t, H_B, DH_B)
    return y, (k_b.reshape(kv_shape), v_b.reshape(kv_shape), k_idx, conv_new, rec_new, ffn_new)


def kernel(x_prompt, x_sample, cache_k, cache_v, cache_idx_k, state_a_conv, state_a_rec, state_ffn_conv,
           page_table, w_in, conv_a_w, a_log, dt_bias, norm_a_g, w_proj_a, w_proj_b, w_out, rel_bias,
           norm1_g, norm2_g, w_up, conv_f_w, conv_f_b, w_down, final_g):
    depth = w_in.shape[0]
    assert depth == 1, "the final norm is fused into the layer's last kernel"
    bp, tp, d = x_prompt.shape
    bs, ts, _ = x_sample.shape
    d_ff = w_down.shape[1]
    l = 0
    lw = (_prep_w_in(w_in[l], d), conv_a_w[l], _misc_row(a_log[l], MISC_A), _misc_row(dt_bias[l], MISC_A),
          norm_a_g[l].reshape(1, -1), w_proj_a[l].astype(BF16), w_proj_b[l].astype(BF16),
          w_out[l].astype(BF16), norm1_g[l].reshape(1, -1), norm2_g[l].reshape(1, -1),
          w_up[l].astype(BF16), conv_f_w[l], conv_f_b[l].reshape(1, -1), w_down[l].astype(BF16),
          final_g.reshape(1, -1))

    def attend_prompt(q_b, k_b, v_b, q_idx, k_idx, misc):
        return _attn_p(rel_bias, q_b, k_b, v_b, q_idx, misc)

    def attend_sample(q_b, k_b, v_b, q_idx, k_idx, misc):
        return _attn_s(page_table, rel_bias, q_b, k_b, v_b, q_idx, k_idx, misc, cache_k, cache_v,
                       cache_idx_k, l)

    zeros = lambda *s: jnp.zeros(s, x_prompt.dtype)
    y_p, st_p = _layer(x_prompt, zeros(bp, CONV_A - 1, QKV_A), zeros(bp, H_A, DK_A, DV_A),
                       zeros(bp, CONV_F - 1, 2 * d_ff), attend_prompt, lw,
                       proj_tm=256, gdn_rows=min(tp, 256), ffn_tm=min(tp, 256))
    y_s, st_s = _layer(x_sample, state_a_conv[l], state_a_rec[l], state_ffn_conv[l], attend_sample, lw,
                       proj_tm=min(bs * ts, 256), gdn_rows=ts, ffn_tm=ts)
    stack = lambda a: a[None]
    return (y_p, y_s) + tuple(map(stack, st_p)) + tuple(map(stack, st_s))
```

```python
import functools
import math

import jax
import jax.numpy as jnp
from jax import lax
from jax.experimental import pallas as pl
from jax.experimental.pallas import tpu as pltpu

F32 = jnp.float32
BF16 = jnp.bfloat16
HI = lax.Precision.HIGHEST

H_A, DK_A, DV_A, CONV_A, CHUNK_A = 4, 128, 128, 4, 64
H_B, DH_B, H_IDX, D_IDX = 4, 128, 8, 64
TOPK_MAX, N_BUCKETS, MAX_DIST = 256, 32, 2048
CONV_F, EPS, PAGE_SIZE = 3, 1e-6, 128
QKV_A = H_A * (2 * DK_A + DV_A)
HB = H_B * DH_B

LANES = 128
SUBLANES = 8
VMEM_LIMIT = 56 * 1024 * 1024

MISC_KIDX, MISC_BETA, MISC_A, MISC_WIDX = 0, D_IDX, D_IDX + H_A, D_IDX + 2 * H_A

GDN_MODE_QK = "bf"
GDN_MODE_INV = "bf"
GDN_MODE_STATE = "bf"

INT_MIN = -2 ** 31
V_SLOTS = 4
NEG_INF = float("-inf")


def _cparams(n_axes):
    return pltpu.CompilerParams(dimension_semantics=("arbitrary",) * n_axes,
                                vmem_limit_bytes=VMEM_LIMIT)


def _const_spec(shape):
    nd = len(shape)
    return pl.BlockSpec(shape, lambda *_: (0,) * nd, pipeline_mode=pl.Buffered(1))


def _silu(x):
    return x * jax.nn.sigmoid(x)


def _dot(a, b, precision=None):
    return jnp.dot(a, b, preferred_element_type=F32, precision=precision)


def _dot_nt(a, b, precision=None):
    return lax.dot_general(a, b, (((1,), (1,)), ((), ())), preferred_element_type=F32,
                           precision=precision)


_DIMS = {"nn": (((1,), (0,)), ((), ())), "nt": (((1,), (1,)), ((), ())), "tn": (((0,), (0,)), ((), ()))}


def _mm(a, b, mode, form="nn"):
    if mode == "bf":
        return lax.dot_general(a.astype(BF16), b.astype(BF16), _DIMS[form], preferred_element_type=F32)
    return lax.dot_general(a, b, _DIMS[form], preferred_element_type=F32, precision=HI)


def _cumsum_rows(x):
    row = lax.broadcasted_iota(jnp.int32, x.shape, 0)
    s = 1
    while s < x.shape[0]:
        x = x + jnp.where(row >= s, pltpu.roll(x, s, 0), 0.0)
        s *= 2
    return x


def _rms(x, g):
    return x * lax.rsqrt(jnp.mean(x * x, axis=-1, keepdims=True) + EPS) * g


def _shift_rows(x, k, prev):
    n = prev.shape[0]
    out = pltpu.roll(x, k, 0)
    head = out[0:SUBLANES]
    row = lax.broadcasted_iota(jnp.int32, head.shape, 0)
    for r in range(k):
        head = jnp.where(row == r, prev[n - k + r:n - k + r + 1, :], head)
    return head if x.shape[0] == SUBLANES else jnp.concatenate([head, out[SUBLANES:]], axis=0)


def _proj_kernel(x_ref, g_ref, w_ref, qkva_ref, za_ref, qb_ref, kb_ref, vb_ref, qidx_ref,
                 ga_ref, gb_ref, misc_ref, kidx_ref):
    xn = _rms(x_ref[...], g_ref[...]).astype(BF16)
    off = 0
    for o_ref in (qkva_ref, za_ref, qb_ref, kb_ref, vb_ref, qidx_ref, ga_ref, gb_ref, misc_ref):
        wd = o_ref.shape[-1]
        o_ref[...] = _dot(xn, w_ref[:, off:off + wd])
        off += wd
    kidx_ref[...] = misc_ref[:, MISC_KIDX:MISC_KIDX + D_IDX]


def _prep_w_in(w_in_l, d_model):
    sizes = (QKV_A, H_A, H_A, H_A * DV_A, 3 * HB, H_IDX * D_IDX, D_IDX, H_IDX, d_model, d_model)
    parts, start = [], 0
    for s in sizes:
        parts.append(w_in_l[:, start:start + s])
        start += s
    qkv_a, beta, a_raw, z_a, qkv_b, q_idx, k_idx, w_idx, gate_a, gate_b = parts
    pad = jnp.zeros((w_in_l.shape[0], LANES - D_IDX - 2 * H_A - H_IDX), w_in_l.dtype)
    misc = jnp.concatenate([k_idx, beta, a_raw, w_idx, pad], axis=1)
    return jnp.concatenate([qkv_a, z_a, qkv_b, q_idx, gate_a, gate_b, misc], axis=1).astype(BF16)


def _proj(x2d, g_row, w_bf, tm):
    n, d = x2d.shape
    widths = (QKV_A, H_A * DV_A, HB, HB, HB, H_IDX * D_IDX, d, d, LANES, D_IDX)
    row = lambda i: (i, 0)
    return pl.pallas_call(
        _proj_kernel,
        grid=(n // tm,),
        in_specs=[pl.BlockSpec((tm, d), row), _const_spec(g_row.shape), _const_spec(w_bf.shape)],
        out_specs=[pl.BlockSpec((tm, wd), row) for wd in widths],
        out_shape=[jax.ShapeDtypeStruct((n, wd), F32) for wd in widths],
        compiler_params=_cparams(1),
        name="proj",
    )(x2d, g_row, w_bf)


def _gdn_kernel(qkv_ref, misc_ref, z_ref, cst_ref, rec0_ref, cw_ref, alog_ref, dtb_ref, ng_ref,
                o_ref, cnew_ref, rnew_ref, s_scr, prev_scr, *, rows, chunk):
    t = pl.program_id(1)
    nt = pl.num_programs(1)
    npr = CONV_A - 1

    @pl.when(t == 0)
    def _():
        s_scr[...] = rec0_ref[0]
        prev_scr[SUBLANES - npr:SUBLANES, :] = cst_ref[0]

    x = qkv_ref[0]
    prev = prev_scr[...]
    conv = _shift_rows(x, 3, prev) * cw_ref[0:1, :]
    conv = conv + _shift_rows(x, 2, prev) * cw_ref[1:2, :]
    conv = conv + _shift_rows(x, 1, prev) * cw_ref[2:3, :]
    conv = conv + x * cw_ref[3:4, :]
    prev_scr[SUBLANES - npr:SUBLANES, :] = x[rows - npr:rows, :]

    @pl.when(t == nt - 1)
    def _():
        cnew_ref[0] = x[rows - npr:rows, :]

    act = _silu(conv)
    misc = misc_ref[0]
    beta_full = jax.nn.sigmoid(misc)
    sp = misc + dtb_ref[...]
    softplus = jnp.maximum(sp, 0.0) + jnp.log1p(jnp.exp(-jnp.abs(sp)))
    g_full = -jnp.exp(alog_ref[...]) * softplus

    ri = lax.broadcasted_iota(jnp.int32, (chunk, chunk), 0)
    ci = lax.broadcasted_iota(jnp.int32, (chunk, chunk), 1)
    tril = ri >= ci
    strict = ri > ci
    eye = (ri == ci).astype(F32)
    lane = lax.broadcasted_iota(jnp.int32, (chunk, LANES), 1)
    n_levels = int(math.log2(chunk))
    hk = H_A * DK_A
    chunks = list(range(0, rows, chunk))
    pairs = [(c0, h) for c0 in chunks for h in range(H_A)]

    gcum = {c0: _cumsum_rows(g_full[c0:c0 + chunk, :]) for c0 in chunks}
    gcum_t = {c0: gcum[c0].T for c0 in chunks} if chunk >= 64 else None

    pre = {}
    for c0, h in pairs:
        q = act[c0:c0 + chunk, h * DK_A:(h + 1) * DK_A]
        k = act[c0:c0 + chunk, hk + h * DK_A:hk + (h + 1) * DK_A]
        v = act[c0:c0 + chunk, 2 * hk + h * DV_A:2 * hk + (h + 1) * DV_A]
        q = q * lax.rsqrt(jnp.sum(q * q, axis=-1, keepdims=True) + EPS) * (DK_A ** -0.5)
        k = k * lax.rsqrt(jnp.sum(k * k, axis=-1, keepdims=True) + EPS)
        beta = beta_full[c0:c0 + chunk, MISC_BETA + h:MISC_BETA + h + 1]
        gc = gcum[c0][:, MISC_A + h:MISC_A + h + 1]
        if gcum_t is not None:
            g_row = gcum_t[c0][MISC_A + h:MISC_A + h + 1, :]
        else:
            g_row = _mm((lane == MISC_A + h).astype(F32), gcum[c0], "hi", "nt")
        gamma = jnp.where(tril, jnp.exp(jnp.where(tril, gc - g_row, 0.0)), 0.0)
        kb = k * beta
        egc = jnp.exp(gc)
        g_last = gc[chunk - 1:chunk, :]
        pre[c0, h] = dict(
            xp=-jnp.where(strict, _mm(kb, k, GDN_MODE_QK, "nt") * gamma, 0.0),
            rhs=jnp.concatenate([kb * egc, v * beta], axis=-1),
            attn=jnp.where(tril, _mm(q, k, GDN_MODE_QK, "nt") * gamma, 0.0),
            q_dec=q * egc, k_dec=k * jnp.exp(g_last - gc), e_last=jnp.exp(g_last))
    for p in pre.values():
        p["inv"] = eye + p["xp"]
    for _ in range(n_levels - 1):
        for p in pre.values():
            p["xp"] = _mm(p["xp"], p["xp"], GDN_MODE_INV)
        for p in pre.values():
            p["inv"] = p["inv"] + _mm(p["inv"], p["xp"], GDN_MODE_INV)
    for p in pre.values():
        p["sol"] = _mm(p["inv"], p["rhs"], GDN_MODE_INV)

    for c0, h in pairs:
        p = pre[c0, h]
        s = s_scr[h]
        v_new = p["sol"][:, DK_A:] - _mm(p["sol"][:, :DK_A], s, GDN_MODE_STATE)
        o = _mm(p["q_dec"], s, GDN_MODE_STATE) + _mm(p["attn"], v_new, GDN_MODE_STATE)
        s_scr[h] = s * p["e_last"] + _mm(p["k_dec"], v_new, GDN_MODE_STATE, "tn")
        z = z_ref[0, c0:c0 + chunk, h * DV_A:(h + 1) * DV_A]
        o_ref[0, c0:c0 + chunk, h * DV_A:(h + 1) * DV_A] = _rms(o, ng_ref[...]) * _silu(z)

    @pl.when(t == nt - 1)
    def _():
        rnew_ref[0] = s_scr[...]


def _gdn(qkv_a, misc, z_a, conv_state, rec0, conv_w, alog_row, dtb_row, ng_row, rows):
    b, t, _ = qkv_a.shape
    chunk = math.gcd(t, CHUNK_A)
    blk = lambda w: pl.BlockSpec((1, rows, w), lambda i, j: (i, j, 0))
    per_b3 = lambda s: pl.BlockSpec((1,) + s, lambda i, j: (i, 0, 0))
    per_b4 = lambda s: pl.BlockSpec((1,) + s, lambda i, j: (i, 0, 0, 0))
    return pl.pallas_call(
        functools.partial(_gdn_kernel, rows=rows, chunk=chunk),
        grid=(b, t // rows),
        in_specs=[blk(QKV_A), blk(LANES), blk(H_A * DV_A),
                  per_b3((CONV_A - 1, QKV_A)), per_b4((H_A, DK_A, DV_A)),
                  _const_spec(conv_w.shape), _const_spec(alog_row.shape),
                  _const_spec(dtb_row.shape), _const_spec(ng_row.shape)],
        out_specs=[blk(H_A * DV_A), per_b3((CONV_A - 1, QKV_A)), per_b4((H_A, DK_A, DV_A))],
        out_shape=[jax.ShapeDtypeStruct((b, t, H_A * DV_A), F32),
                   jax.ShapeDtypeStruct((b, CONV_A - 1, QKV_A), F32),
                   jax.ShapeDtypeStruct((b, H_A, DK_A, DV_A), F32)],
        scratch_shapes=[pltpu.VMEM((H_A, DK_A, DV_A), F32), pltpu.VMEM((SUBLANES, QKV_A), F32)],
        compiler_params=_cparams(2),
        name="gdn",
    )(qkv_a, misc, z_a, conv_state, rec0, conv_w, alog_row, dtb_row, ng_row)


def _rel_bias_tile(dist, bias_ref):
    nd = jnp.maximum(dist, 0)
    max_exact = N_BUCKETS // 2
    nf = jnp.maximum(nd, max_exact).astype(F32)
    large = max_exact + (jnp.log(nf / max_exact) / math.log(MAX_DIST / max_exact)
                         * (N_BUCKETS - max_exact)).astype(jnp.int32)
    large = jnp.minimum(large, N_BUCKETS - 1)
    bucket = jnp.where(nd < max_exact, nd, large)
    outs = []
    for h in range(H_B):
        val = jnp.full(dist.shape, bias_ref[0, h], F32)
        for bk in range(1, N_BUCKETS):
            val = jnp.where(bucket == bk, bias_ref[bk, h], val)
        outs.append(val)
    return outs


def _order_key(score, admissible):
    bits = pltpu.bitcast(score, jnp.int32)
    key = jnp.where(bits < 0, bits ^ jnp.int32(0x7FFFFFFF), bits)
    key = jnp.where(bits == jnp.int32(INT_MIN), 0, key)
    return jnp.where(admissible, key, jnp.int32(INT_MIN))


def _count(m, ka):
    x = m.astype(F32)
    step = LANES if ka == 1 else SUBLANES
    take = (lambda j: x[:, j:j + step]) if ka == 1 else (lambda j: x[j:j + step, :])
    parts = [take(j) for j in range(0, x.shape[ka], step)]
    while len(parts) > 1:
        odd = [parts[-1]] if len(parts) % 2 else []
        parts = [a + b for a, b in zip(parts[0::2], parts[1::2])] + odd
    return jnp.sum(parts[0], axis=ka, keepdims=True)


def _kth_largest(key_ref, top_k, ka):
    shape = (key_ref.shape[0], 1) if ka == 1 else (1, key_ref.shape[1])

    def body(i, theta):
        inc = lax.shift_left(jnp.int32(1), jnp.int32(31) - i)
        cand = theta + inc
        cnt = _count(key_ref[...] >= cand, ka)
        return jnp.where(cnt >= top_k, cand, theta)

    return lax.fori_loop(0, 32, body, jnp.full(shape, INT_MIN, jnp.int32))


def _select_topk(key_ref, admissible, top_k, mask_ref, tri_ref, ka):
    theta = _kth_largest(key_ref, top_k, ka)
    key = key_ref[...]
    gt = key > theta
    eq = key == theta
    need = top_k - _count(gt, ka)
    all_row = theta == jnp.int32(INT_MIN)
    simple = jnp.logical_or(_count(eq, ka) == need, all_row)
    n_bad = jnp.sum(jnp.where(simple, 0.0, 1.0))

    @pl.when(n_bad == 0.0)
    def _():
        mask_ref[...] = jnp.logical_and(key >= theta, admissible).astype(F32)

    @pl.when(n_bad != 0.0)
    def _():
        run = jnp.zeros(theta.shape, F32)
        for c0 in range(0, key.shape[ka], LANES):
            grp = (slice(None), slice(c0, c0 + LANES)) if ka == 1 else (slice(c0, c0 + LANES), slice(None))
            eq_c = eq[grp].astype(F32)
            if ka == 1:
                before = _dot(eq_c.astype(BF16), tri_ref[...]) + run
            else:
                before = _dot(tri_ref[...], eq_c.astype(BF16)) + run
            take = jnp.logical_and(eq[grp], before < need)
            m = jnp.logical_and(jnp.logical_or(gt[grp], take), admissible[grp])
            mask_ref[grp] = m.astype(F32)
            run = run + jnp.sum(eq_c, axis=ka, keepdims=True)


def _init_tri(tri_ref, ka):
    r = lax.broadcasted_iota(jnp.int32, (LANES, LANES), 0)
    c = lax.broadcasted_iota(jnp.int32, (LANES, LANES), 1)
    tri_ref[...] = ((r < c) if ka == 1 else (r > c)).astype(BF16)


def _attn_p_kernel(bias_ref, qb_ref, kb_ref, vb_ref, qidx_ref, miscq_ref, misck_ref, o_ref,
                   k_scr, vt_scr, kidx_scr, btab_scr, mask_scr, key_scr, tri_scr, *, tq, top_k, n_groups):
    b = pl.program_id(0)
    i = pl.program_id(1)
    nq = pl.num_programs(1)

    @pl.when(jnp.logical_and(b == 0, i == 0))
    def _():
        _init_tri(tri_scr, 0)
        kk = lax.broadcasted_iota(jnp.int32, (tq, tq), 0)
        qq = lax.broadcasted_iota(jnp.int32, (tq, tq), 1)
        for m in range(2 * nq - 1):
            dd = nq - 1 - m
            if dd >= 0:
                tiles = _rel_bias_tile(dd * tq + qq - kk, bias_ref)
                for h in range(H_B):
                    btab_scr[h, m * tq:(m + 1) * tq, :] = tiles[h]
            else:
                for h in range(H_B):
                    btab_scr[h, m * tq:(m + 1) * tq, :] = jnp.zeros((tq, tq), F32)

    @pl.when(i == 0)
    def _():
        k_scr[...] = kb_ref[0].astype(BF16)
        for h in range(H_B):
            vt_scr[h * DH_B:(h + 1) * DH_B, :] = vb_ref[0, :, h * DH_B:(h + 1) * DH_B].T.astype(BF16)
        kidx_scr[...] = misck_ref[0, :, MISC_KIDX:MISC_KIDX + D_IDX].astype(BF16)

    def attend(n_keys):
        qidx = qidx_ref[0]
        misc_t = miscq_ref[0].T
        kidx = kidx_scr[0:n_keys, :]
        score = jnp.zeros((n_keys, tq), F32)
        for h in range(H_IDX):
            dots = _dot_nt(kidx, qidx[:, h * D_IDX:(h + 1) * D_IDX].astype(BF16))
            w = misc_t[MISC_WIDX + h:MISC_WIDX + h + 1, :] * (H_IDX ** -0.5) * (D_IDX ** -0.5)
            score = score + jnp.maximum(dots, 0.0) * w

        kpos = lax.broadcasted_iota(jnp.int32, (n_keys, tq), 0)
        qpos = i * tq + lax.broadcasted_iota(jnp.int32, (n_keys, tq), 1)
        admissible = kpos <= qpos
        key_ref = key_scr.at[0:n_keys, :]
        key_ref[...] = _order_key(score, admissible)
        mask_ref = mask_scr.at[0:n_keys, :]
        _select_topk(key_ref, admissible, top_k, mask_ref, tri_scr, 0)

        sel = mask_ref[...] > 0.5
        qb = qb_ref[0]
        boff = pl.multiple_of((nq - 1 - i) * tq, tq)
        for h in range(H_B):
            hd = slice(h * DH_B, (h + 1) * DH_B)
            logits = _dot_nt(k_scr[0:n_keys, hd], qb[:, hd].astype(BF16)) * (DH_B ** -0.5)
            logits = logits + btab_scr[h, pl.ds(boff, n_keys), :]
            logits = jnp.where(sel, logits, NEG_INF)
            m = jnp.max(logits, axis=0, keepdims=True)
            p = jnp.exp(logits - m)
            denom = jnp.sum(p, axis=0, keepdims=True)
            out_t = _dot(vt_scr[hd, 0:n_keys], p.astype(BF16)) / denom
            o_ref[0, :, hd] = out_t.T

    per_group = nq // n_groups
    for grp in range(n_groups):
        pl.when(i // per_group == grp)(functools.partial(attend, (grp + 1) * per_group * tq))


def _attn_p(rel_bias, q_b, k_b, v_b, q_idx, misc, tq=128):
    b, t, _ = q_b.shape
    nq = t // tq
    top_k = min(TOPK_MAX, t // 4)
    n_groups = math.gcd(nq, 4)
    qblk = lambda w: pl.BlockSpec((1, tq, w), lambda bi, i: (bi, i, 0))
    full = lambda w: pl.BlockSpec((1, t, w), lambda bi, i: (bi, 0, 0))
    return pl.pallas_call(
        functools.partial(_attn_p_kernel, tq=tq, top_k=top_k, n_groups=n_groups),
        grid=(b, nq),
        in_specs=[pl.BlockSpec(memory_space=pltpu.SMEM),
                  qblk(HB), full(HB), full(HB), qblk(H_IDX * D_IDX), qblk(LANES), full(LANES)],
        out_specs=qblk(HB),
        out_shape=jax.ShapeDtypeStruct((b, t, HB), F32),
        scratch_shapes=[pltpu.VMEM((t, HB), BF16), pltpu.VMEM((HB, t), BF16),
                        pltpu.VMEM((t, D_IDX), BF16),
                        pltpu.VMEM((H_B, (2 * nq - 1) * tq, tq), F32),
                        pltpu.VMEM((t, tq), F32), pltpu.VMEM((t, tq), jnp.int32),
                        pltpu.VMEM((LANES, LANES), BF16)],
        compiler_params=_cparams(2),
        name="attn_p",
    )(rel_bias, q_b, k_b, v_b, q_idx, misc, misc)


def _attn_s_kernel(pt_ref, bias_ref, qb_ref, qidx_ref, widx_ref, knew_ref, vnew_ref, kidxnew_ref,
                   cidx_hbm, ck_hbm, cv_hbm, o_ref,
                   score_scr, logit_scr, mask_scr, key_scr, acc_scr, btab_scr, tri_scr, idx_buf, k_buf, v_buf, sems,
                   *, layer, t_new, n_pages, gp, top_k):
    b = pl.program_id(0)
    n_grp = n_pages // gp
    past = n_pages * PAGE_SIZE
    n_cols = past + PAGE_SIZE
    near0 = max(0, (past - MAX_DIST) // PAGE_SIZE) * PAGE_SIZE
    n_near = n_cols - near0
    pg_rows = PAGE_SIZE * H_B
    grp_keys = gp * PAGE_SIZE

    @pl.when(b == 0)
    def _():
        _init_tri(tri_scr, 1)
        trow = lax.broadcasted_iota(jnp.int32, (t_new, n_near), 0)
        col = near0 + lax.broadcasted_iota(jnp.int32, (t_new, n_near), 1)
        tiles = _rel_bias_tile(past + trow - col, bias_ref)
        for h in range(H_B):
            btab_scr[h] = tiles[h]

    def group_copies(hbm, buf, sem_row, slot, grp):
        out = []
        for m in range(gp):
            page = pt_ref[b, grp * gp + m]
            dst = buf.at[slot, m] if buf is idx_buf else buf.at[slot, pl.ds(m * pg_rows, pg_rows)]
            out.append(pltpu.make_async_copy(hbm.at[layer, page], dst, sems.at[sem_row, slot]))
        return out

    def start_idx_k(slot, grp):
        for c in group_copies(cidx_hbm, idx_buf, 0, slot, grp) + group_copies(ck_hbm, k_buf, 1, slot, grp):
            c.start()

    def wait_idx_k(slot, grp):
        for c in group_copies(cidx_hbm, idx_buf, 0, slot, grp) + group_copies(ck_hbm, k_buf, 1, slot, grp):
            c.wait()

    def start_v(slot, grp):
        for c in group_copies(cv_hbm, v_buf, 2, slot, grp):
            c.start()

    def wait_v(slot, grp):
        for c in group_copies(cv_hbm, v_buf, 2, slot, grp):
            c.wait()

    qb = qb_ref[0].astype(BF16)
    qidx = qidx_ref[0].astype(BF16)
    widx = widx_ref[0] * (H_IDX ** -0.5)
    head_rows = lambda h: slice(h * t_new, (h + 1) * t_new)
    head_lanes = lambda h: slice(h * DH_B, (h + 1) * DH_B)

    def scores_of(kidx_t):
        dots = _dot(qidx, kidx_t.astype(BF16)) * (D_IDX ** -0.5)
        contrib = jnp.maximum(dots, 0.0) * widx
        s = contrib[0:t_new]
        for h in range(1, H_IDX):
            s = s + contrib[h * t_new:(h + 1) * t_new]
        return s

    start_idx_k(0, 0)

    def pass1(grp, carry):
        slot = grp % 2

        @pl.when(grp + 1 < n_grp)
        def _():
            start_idx_k(1 - slot, grp + 1)

        wait_idx_k(slot, grp)
        c0 = pl.multiple_of(grp * grp_keys, grp_keys)
        for m in range(gp):
            score_scr[:, pl.ds(c0 + m * PAGE_SIZE, PAGE_SIZE)] = scores_of(idx_buf[slot, m])
        for h in range(H_B):
            k_h = k_buf[slot, pl.ds(h, grp_keys, stride=H_B), :].astype(BF16)
            logit_scr[head_rows(h), pl.ds(c0, grp_keys)] = _dot_nt(qb[:, head_lanes(h)], k_h)
        return carry

    lax.fori_loop(0, n_grp, pass1, 0)
    for g in range(min(V_SLOTS - 1, n_grp)):
        start_v(g, g)

    score_scr[:, past:n_cols] = scores_of(kidxnew_ref[0])
    for h in range(H_B):
        k_h = knew_ref[0, :, head_lanes(h)].astype(BF16)
        logit_scr[head_rows(h), past:n_cols] = _dot_nt(qb[:, head_lanes(h)], k_h)

    trow = lax.broadcasted_iota(jnp.int32, (t_new, n_cols), 0)
    col = lax.broadcasted_iota(jnp.int32, (t_new, n_cols), 1)
    admissible = col <= past + trow
    key_scr[...] = _order_key(score_scr[...], admissible)
    _select_topk(key_scr, admissible, top_k, mask_scr, tri_scr, 1)
    sel = mask_scr[...] > 0.5
    for h in range(H_B):
        rs = head_rows(h)
        near = logit_scr[rs, near0:n_cols] * (DH_B ** -0.5) + btab_scr[h]
        near = jnp.where(sel[:, near0:n_cols], near, NEG_INF)
        mx = jnp.max(near, axis=-1, keepdims=True)
        if near0 > 0:
            far = logit_scr[rs, 0:near0] * (DH_B ** -0.5) + bias_ref[N_BUCKETS - 1, h]
            far = jnp.where(sel[:, 0:near0], far, NEG_INF)
            mx = jnp.maximum(mx, jnp.max(far, axis=-1, keepdims=True))
        pn = jnp.exp(near - mx)
        denom = jnp.sum(pn, axis=-1, keepdims=True)
        if near0 > 0:
            pf = jnp.exp(far - mx)
            denom = denom + jnp.sum(pf, axis=-1, keepdims=True)
            logit_scr[rs, 0:near0] = pf / denom
        logit_scr[rs, near0:n_cols] = pn / denom
        v_h = vnew_ref[0, :, head_lanes(h)].astype(BF16)
        acc_scr[rs, :] = _dot(logit_scr[rs, past:n_cols].astype(BF16), v_h)

    def pass2(grp, carry):
        slot = grp % V_SLOTS
        ahead = grp + V_SLOTS - 1

        @pl.when(ahead < n_grp)
        def _():
            start_v(ahead % V_SLOTS, ahead)

        wait_v(slot, grp)
        c0 = pl.multiple_of(grp * grp_keys, grp_keys)
        for h in range(H_B):
            v_h = v_buf[slot, pl.ds(h, grp_keys, stride=H_B), :].astype(BF16)
            p_h = logit_scr[head_rows(h), pl.ds(c0, grp_keys)].astype(BF16)
            acc_scr[head_rows(h), :] += _dot(p_h, v_h)
        return carry

    lax.fori_loop(0, n_grp, pass2, 0)
    for h in range(H_B):
        o_ref[0, :, head_lanes(h)] = acc_scr[head_rows(h), :]


def _attn_s(page_table, rel_bias, q_b, k_new, v_new, q_idx, kidx_new, misc, cache_k, cache_v, cache_idx_k,
            layer):
    b, t_new, _ = q_b.shape
    n_pages = page_table.shape[1]
    past = n_pages * PAGE_SIZE
    n_cols = past + PAGE_SIZE
    top_k = min(TOPK_MAX, (past + t_new) // 4)
    gp = math.gcd(n_pages, 16)
    hq = H_B * t_new
    near0 = max(0, (past - MAX_DIST) // PAGE_SIZE) * PAGE_SIZE

    qidx = q_idx.reshape(b, t_new, H_IDX, D_IDX).transpose(0, 2, 1, 3).reshape(b, H_IDX * t_new, D_IDX)
    widx = misc[:, :, MISC_WIDX:MISC_WIDX + H_IDX].transpose(0, 2, 1).reshape(b, H_IDX * t_new, 1)
    pad_page = lambda a: jnp.pad(a, ((0, 0), (0, PAGE_SIZE - t_new), (0, 0)))
    k_new, v_new, kidx_new_t = pad_page(k_new), pad_page(v_new), pad_page(kidx_new).transpose(0, 2, 1)
    ck = cache_k.reshape(cache_k.shape[0], cache_k.shape[1], PAGE_SIZE * H_B, DH_B)
    cv = cache_v.reshape(cache_v.shape[0], cache_v.shape[1], PAGE_SIZE * H_B, DH_B)
    cidx_t = cache_idx_k.transpose(0, 1, 3, 2)

    per_b = lambda s: pl.BlockSpec((1,) + s, lambda bi, pt: (bi, 0, 0))
    hbm = pl.BlockSpec(memory_space=pl.ANY)
    in_specs = [pl.BlockSpec(memory_space=pltpu.SMEM),
                per_b((t_new, HB)), per_b((H_IDX * t_new, D_IDX)), per_b((H_IDX * t_new, 1)),
                per_b((PAGE_SIZE, HB)), per_b((PAGE_SIZE, HB)), per_b((D_IDX, PAGE_SIZE)),
                hbm, hbm, hbm]
    grid_spec = pltpu.PrefetchScalarGridSpec(
        num_scalar_prefetch=1,
        grid=(b,),
        in_specs=in_specs,
        out_specs=pl.BlockSpec((1, t_new, HB), lambda bi, pt: (bi, 0, 0)),
        scratch_shapes=[pltpu.VMEM((t_new, n_cols), F32), pltpu.VMEM((hq, n_cols), F32),
                        pltpu.VMEM((t_new, n_cols), F32), pltpu.VMEM((t_new, n_cols), jnp.int32),
                        pltpu.VMEM((hq, DH_B), F32),
                        pltpu.VMEM((H_B, t_new, n_cols - near0), F32),
                        pltpu.VMEM((LANES, LANES), BF16),
                        pltpu.VMEM((2, gp, D_IDX, PAGE_SIZE), F32),
                        pltpu.VMEM((2, gp * PAGE_SIZE * H_B, DH_B), F32),
                        pltpu.VMEM((V_SLOTS, gp * PAGE_SIZE * H_B, DH_B), F32),
                        pltpu.SemaphoreType.DMA((3, V_SLOTS))])
    return pl.pallas_call(
        functools.partial(_attn_s_kernel, layer=layer, t_new=t_new, n_pages=n_pages, gp=gp, top_k=top_k),
        grid_spec=grid_spec,
        out_shape=jax.ShapeDtypeStruct((b, t_new, HB), F32),
        compiler_params=_cparams(1),
        name="attn_s",
    )(page_table, rel_bias, q_b, qidx, widx, k_new, v_new, kidx_new_t, cidx_t, ck, cv)


def _ffn_kernel(x_ref, oa_ref, ob_ref, ga_ref, gb_ref, fst_ref, wpa_ref, wpb_ref, wout_ref, n2_ref,
                wup_ref, cfw_ref, cfb_ref, wdown_ref, fg_ref, y_ref, fnew_ref, prev_scr, act_scr,
                *, tm, ch, d_ff, seq):
    t = pl.program_id(1)
    nt = pl.num_programs(1)
    npr = CONV_F - 1
    if seq is None:
        @pl.when(t == 0)
        def _():
            prev_scr[SUBLANES - npr:SUBLANES, :] = fst_ref[0]

    pa = _dot(oa_ref[0].astype(BF16), wpa_ref[...])
    pb = _dot(ob_ref[0].astype(BF16), wpb_ref[...])
    merged = jax.nn.sigmoid(ga_ref[0]) * pa + jax.nn.sigmoid(gb_ref[0]) * pb
    x1 = x_ref[0] + _dot(merged.astype(BF16), wout_ref[...])
    xn2 = _rms(x1, n2_ref[...]).astype(BF16)

    def conv_cols(c0):
        cols = slice(c0, c0 + ch)
        up = _dot(xn2, wup_ref[:, cols])
        if seq is None:
            prev = prev_scr[:, cols]
            s2, s1 = _shift_rows(up, 2, prev), _shift_rows(up, 1, prev)
            prev_scr[SUBLANES - npr:SUBLANES, cols] = up[tm - npr:tm, :]
        else:
            hist = fst_ref[0, :, cols]
            tpos = lax.broadcasted_iota(jnp.int32, up.shape, 0) & (seq - 1)
            s2 = jnp.where(tpos < 2, hist, pltpu.roll(up, 2, 0))
            s1 = jnp.where(tpos < 1, pltpu.roll(hist, tm - 1, 0), pltpu.roll(up, 1, 0))
            fnew_ref[:, :, cols] = up.reshape(tm // seq, seq, ch)[:, seq - npr:seq, :]
        out = s2 * cfw_ref[0:1, cols]
        out = out + s1 * cfw_ref[1:2, cols]
        out = out + up * cfw_ref[2:3, cols]
        return out + cfb_ref[:, cols]

    for c0 in range(0, d_ff, ch):
        gate = conv_cols(c0)
        val = conv_cols(d_ff + c0)
        act_scr[:, c0:c0 + ch] = (_silu(gate) * val).astype(BF16)
    y_ref[0] = _rms(x1 + _dot(act_scr[...], wdown_ref[...]), fg_ref[...])

    if seq is None:
        @pl.when(t == nt - 1)
        def _():
            fnew_ref[0] = prev_scr[SUBLANES - npr:SUBLANES, :]


def _ffn(x, o_a, o_b, gate_a, gate_b, ffn_state, wpa, wpb, wout, n2_row, wup, cfw, cfb_row, wdown, fg_row, tm):
    b, t, d = x.shape
    d_ff = wdown.shape[0]
    ch = 256
    npr = CONV_F - 1
    consts = (wpa, wpb, wout, n2_row, wup, cfw, cfb_row, wdown, fg_row)
    if t == SUBLANES and t > npr:
        seq, rows = t, b * t
        flat = lambda a: a.reshape(1, rows, a.shape[-1])
        x, o_a, o_b, gate_a, gate_b = map(flat, (x, o_a, o_b, gate_a, gate_b))
        hist = flat(jnp.pad(ffn_state, ((0, 0), (0, t - npr), (0, 0))))
        grid, tm = (1, 1), rows
        blk = lambda w: pl.BlockSpec((1, rows, w), lambda i, j: (0, 0, 0))
        state_in, state_out = blk(2 * d_ff), pl.BlockSpec((b, npr, 2 * d_ff), lambda i, j: (0, 0, 0))
    else:
        seq, hist = None, ffn_state
        grid = (b, t // tm)
        blk = lambda w: pl.BlockSpec((1, tm, w), lambda i, j: (i, j, 0))
        state_in = state_out = pl.BlockSpec((1, npr, 2 * d_ff), lambda i, j: (i, 0, 0))
    y, ffn_new = pl.pallas_call(
        functools.partial(_ffn_kernel, tm=tm, ch=ch, d_ff=d_ff, seq=seq),
        grid=grid,
        in_specs=[blk(d), blk(o_a.shape[-1]), blk(o_b.shape[-1]), blk(d), blk(d), state_in]
                 + [_const_spec(c.shape) for c in consts],
        out_specs=[blk(d), state_out],
        out_shape=[jax.ShapeDtypeStruct(x.shape, F32),
                   jax.ShapeDtypeStruct((b, npr, 2 * d_ff), F32)],
        scratch_shapes=[pltpu.VMEM((SUBLANES, 2 * d_ff), F32), pltpu.VMEM((tm, d_ff), BF16)],
        compiler_params=_cparams(2),
        name="ffn",
    )(x, o_a, o_b, gate_a, gate_b, hist, *consts)
    return y.reshape(b, t, d), ffn_new


def _misc_row(vals, col0):
    return jnp.zeros((1, LANES), F32).at[0, col0:col0 + vals.shape[0]].set(vals.astype(F32))


def _layer(x, conv_state, rec_state, ffn_state, attend, lw, proj_tm, gdn_rows, ffn_tm):
    b, t, d = x.shape
    (w_in_bf, conv_a_w, alog_row, dtb_row, ng_row, wpa, wpb, wout, n1_row, n2_row, wup, cfw, cfb_row,
     wdown, fg_row) = lw
    (qkv_a, z_a, q_b, k_b, v_b, q_idx, gate_a, gate_b, misc, k_idx) = _proj(
        x.reshape(b * t, d), n1_row, w_in_bf, proj_tm)
    r3 = lambda a: a.reshape(b, t, a.shape[-1])
    qkv_a, z_a, q_b, k_b, v_b, q_idx, gate_a, gate_b, misc, k_idx = map(
        r3, (qkv_a, z_a, q_b, k_b, v_b, q_idx, gate_a, gate_b, misc, k_idx))
    o_a, conv_new, rec_new = _gdn(qkv_a, misc, z_a, conv_state, rec_state, conv_a_w, alog_row, dtb_row,
                                  ng_row, gdn_rows)
    o_b = attend(q_b, k_b, v_b, q_idx, k_idx, misc)
    y, ffn_new = _ffn(x, o_a, o_b, gate_a, gate_b, ffn_state, wpa, wpb, wout, n2_row, wup, cfw, cfb_row,
                      wdown, fg_row, ffn_tm)
    kv_shape = (b, t, H_B, DH_B)
    return y, (k_b.reshape(kv_shape), v_b.reshape(kv_shape), k_idx, conv_new, rec_new, ffn_new)


def kernel(x_prompt, x_sample, cache_k, cache_v, cache_idx_k, state_a_conv, state_a_rec, state_ffn_conv,
           page_table, w_in, conv_a_w, a_log, dt_bias, norm_a_g, w_proj_a, w_proj_b, w_out, rel_bias,
           norm1_g, norm2_g, w_up, conv_f_w, conv_f_b, w_down, final_g):
    depth = w_in.shape[0]
    assert depth == 1, "the final norm is fused into the layer's last kernel"
    bp, tp, d = x_prompt.shape
    bs, ts, _ = x_sample.shape
    d_ff = w_down.shape[1]
    l = 0
    lw = (_prep_w_in(w_in[l], d), conv_a_w[l], _misc_row(a_log[l], MISC_A), _misc_row(dt_bias[l], MISC_A),
          norm_a_g[l].reshape(1, -1), w_proj_a[l].astype(BF16), w_proj_b[l].astype(BF16),
          w_out[l].astype(BF16), norm1_g[l].reshape(1, -1), norm2_g[l].reshape(1, -1),
          w_up[l].astype(BF16), conv_f_w[l], conv_f_b[l].reshape(1, -1), w_down[l].astype(BF16),
          final_g.reshape(1, -1))

    def attend_prompt(q_b, k_b, v_b, q_idx, k_idx, misc):
        return _attn_p(rel_bias, q_b, k_b, v_b, q_idx, misc)

    def attend_sample(q_b, k_b, v_b, q_idx, k_idx, misc):
        return _attn_s(page_table, rel_bias, q_b, k_b, v_b, q_idx, k_idx, misc, cache_k, cache_v,
                       cache_idx_k, l)

    zeros = lambda *s: jnp.zeros(s, x_prompt.dtype)
    y_p, st_p = _layer(x_prompt, zeros(bp, CONV_A - 1, QKV_A), zeros(bp, H_A, DK_A, DV_A),
                       zeros(bp, CONV_F - 1, 2 * d_ff), attend_prompt, lw,
                       proj_tm=256, gdn_rows=min(tp, 256), ffn_tm=min(tp, 256))
    y_s, st_s = _layer(x_sample, state_a_conv[l], state_a_rec[l], state_ffn_conv[l], attend_sample, lw,
                       proj_tm=min(bs * ts, 256), gdn_rows=ts, ffn_tm=ts)
    stack = lambda a: a[None]
    return (y_p, y_s) + tuple(map(stack, st_p)) + tuple(map(stack, st_s))
```

```python
import functools
import math

import jax
import jax.numpy as jnp
from jax import lax
from jax.experimental import pallas as pl
from jax.experimental.pallas import tpu as pltpu

F32 = jnp.float32
BF16 = jnp.bfloat16
HI = lax.Precision.HIGHEST

H_A, DK_A, DV_A, CONV_A, CHUNK_A = 4, 128, 128, 4, 64
H_B, DH_B, H_IDX, D_IDX = 4, 128, 8, 64
TOPK_MAX, N_BUCKETS, MAX_DIST = 256, 32, 2048
CONV_F, EPS, PAGE_SIZE = 3, 1e-6, 128
QKV_A = H_A * (2 * DK_A + DV_A)
HB = H_B * DH_B

LANES = 128
SUBLANES = 8
VMEM_LIMIT = 56 * 1024 * 1024

MISC_KIDX, MISC_BETA, MISC_A, MISC_WIDX = 0, D_IDX, D_IDX + H_A, D_IDX + 2 * H_A

GDN_MODE_QK = "bf"
GDN_MODE_INV = "bf"
GDN_MODE_STATE = "bf"

INT_MIN = -2 ** 31
COUNT_ACCS = 4
V_SLOTS = 6
NEG_INF = float("-inf")


def _cparams(n_axes):
    return pltpu.CompilerParams(dimension_semantics=("arbitrary",) * n_axes,
                                vmem_limit_bytes=VMEM_LIMIT)


def _const_spec(shape):
    nd = len(shape)
    return pl.BlockSpec(shape, lambda *_: (0,) * nd, pipeline_mode=pl.Buffered(1))


def _silu(x):
    return x * jax.nn.sigmoid(x)


def _dot(a, b, precision=None):
    return jnp.dot(a, b, preferred_element_type=F32, precision=precision)


def _dot_nt(a, b, precision=None):
    return lax.dot_general(a, b, (((1,), (1,)), ((), ())), preferred_element_type=F32,
                           precision=precision)


_DIMS = {"nn": (((1,), (0,)), ((), ())), "nt": (((1,), (1,)), ((), ())), "tn": (((0,), (0,)), ((), ()))}


def _mm(a, b, mode, form="nn"):
    if mode == "bf":
        return lax.dot_general(a.astype(BF16), b.astype(BF16), _DIMS[form], preferred_element_type=F32)
    return lax.dot_general(a, b, _DIMS[form], preferred_element_type=F32, precision=HI)


def _cumsum_rows(x):
    row = lax.broadcasted_iota(jnp.int32, x.shape, 0)
    s = 1
    while s < x.shape[0]:
        x = x + jnp.where(row >= s, pltpu.roll(x, s, 0), 0.0)
        s *= 2
    return x


def _rms(x, g):
    return x * lax.rsqrt(jnp.mean(x * x, axis=-1, keepdims=True) + EPS) * g


def _shift_rows(x, k, prev):
    n = prev.shape[0]
    out = pltpu.roll(x, k, 0)
    head = out[0:SUBLANES]
    row = lax.broadcasted_iota(jnp.int32, head.shape, 0)
    for r in range(k):
        head = jnp.where(row == r, prev[n - k + r:n - k + r + 1, :], head)
    return head if x.shape[0] == SUBLANES else jnp.concatenate([head, out[SUBLANES:]], axis=0)


def _proj_kernel(x_ref, g_ref, w_ref, qkva_ref, za_ref, qb_ref, kb_ref, vb_ref, qidx_ref,
                 ga_ref, gb_ref, misc_ref, kidx_ref):
    xn = _rms(x_ref[...], g_ref[...]).astype(BF16)
    tm = x_ref.shape[0]
    off = 0
    for o_ref in (qkva_ref, za_ref, qb_ref, kb_ref, vb_ref, qidx_ref, ga_ref, gb_ref, misc_ref):
        if o_ref is kb_ref or o_ref is vb_ref:
            res = _dot(xn, w_ref[:, off:off + HB])
            for h in range(H_B):
                o_ref[pl.ds(h, tm, stride=H_B), :] = res[:, h * DH_B:(h + 1) * DH_B]
            off += HB
        else:
            wd = o_ref.shape[-1]
            o_ref[...] = _dot(xn, w_ref[:, off:off + wd])
            off += wd
    kidx_ref[...] = misc_ref[:, MISC_KIDX:MISC_KIDX + D_IDX]


def _prep_w_in(w_in_l, d_model):
    sizes = (QKV_A, H_A, H_A, H_A * DV_A, 3 * HB, H_IDX * D_IDX, D_IDX, H_IDX, d_model, d_model)
    parts, start = [], 0
    for s in sizes:
        parts.append(w_in_l[:, start:start + s])
        start += s
    qkv_a, beta, a_raw, z_a, qkv_b, q_idx, k_idx, w_idx, gate_a, gate_b = parts
    pad = jnp.zeros((w_in_l.shape[0], LANES - D_IDX - 2 * H_A - H_IDX), w_in_l.dtype)
    misc = jnp.concatenate([k_idx, beta, a_raw, w_idx, pad], axis=1)
    return jnp.concatenate([qkv_a, z_a, qkv_b, q_idx, gate_a, gate_b, misc], axis=1).astype(BF16)


def _proj(x2d, g_row, w_bf, tm):
    n, d = x2d.shape
    shapes = ((1, QKV_A), (1, H_A * DV_A), (1, HB), (H_B, DH_B), (H_B, DH_B), (1, H_IDX * D_IDX), (1, d), (1, d),
              (1, LANES), (1, D_IDX))
    row = lambda i: (i, 0)
    return pl.pallas_call(
        _proj_kernel,
        grid=(n // tm,),
        in_specs=[pl.BlockSpec((tm, d), row), _const_spec(g_row.shape), _const_spec(w_bf.shape)],
        out_specs=[pl.BlockSpec((tm * r, wd), row) for r, wd in shapes],
        out_shape=[jax.ShapeDtypeStruct((n * r, wd), F32) for r, wd in shapes],
        compiler_params=_cparams(1),
        name="proj",
    )(x2d, g_row, w_bf)


def _gdn_kernel(qkv_ref, misc_ref, z_ref, cst_ref, rec0_ref, cw_ref, alog_ref, dtb_ref, ng_ref,
                o_ref, cnew_ref, rnew_ref, s_scr, prev_scr, *, rows, chunk):
    t = pl.program_id(1)
    nt = pl.num_programs(1)
    npr = CONV_A - 1

    @pl.when(t == 0)
    def _():
        s_scr[...] = rec0_ref[0]
        prev_scr[SUBLANES - npr:SUBLANES, :] = cst_ref[0]

    x = qkv_ref[0]
    prev = prev_scr[...]
    conv = _shift_rows(x, 3, prev) * cw_ref[0:1, :]
    conv = conv + _shift_rows(x, 2, prev) * cw_ref[1:2, :]
    conv = conv + _shift_rows(x, 1, prev) * cw_ref[2:3, :]
    conv = conv + x * cw_ref[3:4, :]
    prev_scr[SUBLANES - npr:SUBLANES, :] = x[rows - npr:rows, :]

    @pl.when(t == nt - 1)
    def _():
        cnew_ref[0] = x[rows - npr:rows, :]

    act = _silu(conv)
    misc = misc_ref[0]
    beta_full = jax.nn.sigmoid(misc)
    sp = misc + dtb_ref[...]
    softplus = jnp.maximum(sp, 0.0) + jnp.log1p(jnp.exp(-jnp.abs(sp)))
    g_full = -jnp.exp(alog_ref[...]) * softplus

    ri = lax.broadcasted_iota(jnp.int32, (chunk, chunk), 0)
    ci = lax.broadcasted_iota(jnp.int32, (chunk, chunk), 1)
    tril = ri >= ci
    strict = ri > ci
    eye = (ri == ci).astype(F32)
    lane = lax.broadcasted_iota(jnp.int32, (chunk, LANES), 1)
    n_levels = int(math.log2(chunk))
    hk = H_A * DK_A
    chunks = list(range(0, rows, chunk))
    pairs = [(c0, h) for c0 in chunks for h in range(H_A)]

    gcum = {c0: _cumsum_rows(g_full[c0:c0 + chunk, :]) for c0 in chunks}
    gcum_t = {c0: gcum[c0].T for c0 in chunks} if chunk >= 64 else None

    pre = {}
    for c0, h in pairs:
        q = act[c0:c0 + chunk, h * DK_A:(h + 1) * DK_A]
        k = act[c0:c0 + chunk, hk + h * DK_A:hk + (h + 1) * DK_A]
        v = act[c0:c0 + chunk, 2 * hk + h * DV_A:2 * hk + (h + 1) * DV_A]
        q = q * lax.rsqrt(jnp.sum(q * q, axis=-1, keepdims=True) + EPS) * (DK_A ** -0.5)
        k = k * lax.rsqrt(jnp.sum(k * k, axis=-1, keepdims=True) + EPS)
        beta = beta_full[c0:c0 + chunk, MISC_BETA + h:MISC_BETA + h + 1]
        gc = gcum[c0][:, MISC_A + h:MISC_A + h + 1]
        if gcum_t is not None:
            g_row = gcum_t[c0][MISC_A + h:MISC_A + h + 1, :]
        else:
            g_row = _mm((lane == MISC_A + h).astype(F32), gcum[c0], "hi", "nt")
        gamma = jnp.where(tril, jnp.exp(jnp.where(tril, gc - g_row, 0.0)), 0.0)
        kb = k * beta
        egc = jnp.exp(gc)
        g_last = gc[chunk - 1:chunk, :]
        pre[c0, h] = dict(
            xp=-jnp.where(strict, _mm(kb, k, GDN_MODE_QK, "nt") * gamma, 0.0),
            rhs=jnp.concatenate([kb * egc, v * beta], axis=-1),
            attn=jnp.where(tril, _mm(q, k, GDN_MODE_QK, "nt") * gamma, 0.0),
            q_dec=q * egc, k_dec=k * jnp.exp(g_last - gc), e_last=jnp.exp(g_last))
    for p in pre.values():
        p["inv"] = eye + p["xp"]
    for _ in range(n_levels - 1):
        for p in pre.values():
            p["xp"] = _mm(p["xp"], p["xp"], GDN_MODE_INV)
        for p in pre.values():
            p["inv"] = p["inv"] + _mm(p["inv"], p["xp"], GDN_MODE_INV)
    for p in pre.values():
        p["sol"] = _mm(p["inv"], p["rhs"], GDN_MODE_INV)

    for c0, h in pairs:
        p = pre[c0, h]
        s = s_scr[h]
        v_new = p["sol"][:, DK_A:] - _mm(p["sol"][:, :DK_A], s, GDN_MODE_STATE)
        o = _mm(p["q_dec"], s, GDN_MODE_STATE) + _mm(p["attn"], v_new, GDN_MODE_STATE)
        s_scr[h] = s * p["e_last"] + _mm(p["k_dec"], v_new, GDN_MODE_STATE, "tn")
        z = z_ref[0, c0:c0 + chunk, h * DV_A:(h + 1) * DV_A]
        o_ref[0, c0:c0 + chunk, h * DV_A:(h + 1) * DV_A] = _rms(o, ng_ref[...]) * _silu(z)

    @pl.when(t == nt - 1)
    def _():
        rnew_ref[0] = s_scr[...]


def _gdn(qkv_a, misc, z_a, conv_state, rec0, conv_w, alog_row, dtb_row, ng_row, rows):
    b, t, _ = qkv_a.shape
    chunk = math.gcd(t, CHUNK_A)
    blk = lambda w: pl.BlockSpec((1, rows, w), lambda i, j: (i, j, 0))
    per_b3 = lambda s: pl.BlockSpec((1,) + s, lambda i, j: (i, 0, 0))
    per_b4 = lambda s: pl.BlockSpec((1,) + s, lambda i, j: (i, 0, 0, 0))
    return pl.pallas_call(
        functools.partial(_gdn_kernel, rows=rows, chunk=chunk),
        grid=(b, t // rows),
        in_specs=[blk(QKV_A), blk(LANES), blk(H_A * DV_A),
                  per_b3((CONV_A - 1, QKV_A)), per_b4((H_A, DK_A, DV_A)),
                  _const_spec(conv_w.shape), _const_spec(alog_row.shape),
                  _const_spec(dtb_row.shape), _const_spec(ng_row.shape)],
        out_specs=[blk(H_A * DV_A), per_b3((CONV_A - 1, QKV_A)), per_b4((H_A, DK_A, DV_A))],
        out_shape=[jax.ShapeDtypeStruct((b, t, H_A * DV_A), F32),
                   jax.ShapeDtypeStruct((b, CONV_A - 1, QKV_A), F32),
                   jax.ShapeDtypeStruct((b, H_A, DK_A, DV_A), F32)],
        scratch_shapes=[pltpu.VMEM((H_A, DK_A, DV_A), F32), pltpu.VMEM((SUBLANES, QKV_A), F32)],
        compiler_params=_cparams(2),
        name="gdn",
    )(qkv_a, misc, z_a, conv_state, rec0, conv_w, alog_row, dtb_row, ng_row)


def _rel_bias_tile(dist, bias_ref):
    nd = jnp.maximum(dist, 0)
    max_exact = N_BUCKETS // 2
    nf = jnp.maximum(nd, max_exact).astype(F32)
    large = max_exact + (jnp.log(nf / max_exact) / math.log(MAX_DIST / max_exact)
                         * (N_BUCKETS - max_exact)).astype(jnp.int32)
    large = jnp.minimum(large, N_BUCKETS - 1)
    bucket = jnp.where(nd < max_exact, nd, large)
    outs = []
    for h in range(H_B):
        val = jnp.full(dist.shape, bias_ref[0, h], F32)
        for bk in range(1, N_BUCKETS):
            val = jnp.where(bucket == bk, bias_ref[bk, h], val)
        outs.append(val)
    return outs


def _order_key(score, admissible):
    bits = pltpu.bitcast(score, jnp.int32)
    key = jnp.where(bits < 0, bits ^ jnp.int32(0x7FFFFFFF), bits)
    key = jnp.where(bits == jnp.int32(INT_MIN), 0, key)
    return jnp.where(admissible, key, jnp.int32(INT_MIN))


def _count(m, ka):
    x = m.astype(F32)
    step = LANES if ka == 1 else SUBLANES
    take = (lambda j: x[:, j:j + step]) if ka == 1 else (lambda j: x[j:j + step, :])
    parts = [take(j) for j in range(0, x.shape[ka], step)]
    while len(parts) > 1:
        odd = [parts[-1]] if len(parts) % 2 else []
        parts = [a + b for a, b in zip(parts[0::2], parts[1::2])] + odd
    return jnp.sum(parts[0], axis=ka, keepdims=True)


def _kth_largest(key_ref, top_k, ka):
    shape = (key_ref.shape[0], 1) if ka == 1 else (1, key_ref.shape[1])

    step = LANES if ka == 1 else SUBLANES
    slab_shape = (key_ref.shape[0], step) if ka == 1 else (step, key_ref.shape[1])

    def body(i, theta):
        inc = lax.shift_left(jnp.int32(1), jnp.int32(31) - i)
        cand = theta + inc
        cand_b = jnp.broadcast_to(cand, slab_shape)
        accs = [None] * COUNT_ACCS
        for n, j in enumerate(range(0, key_ref.shape[ka], step)):
            slab = key_ref[:, j:j + step] if ka == 1 else key_ref[j:j + step, :]
            part = (slab >= cand_b).astype(F32)
            accs[n % COUNT_ACCS] = part if accs[n % COUNT_ACCS] is None else accs[n % COUNT_ACCS] + part
        total = functools.reduce(lambda a, b: a + b, [a for a in accs if a is not None])
        cnt = jnp.sum(total, axis=ka, keepdims=True)
        return jnp.where(cnt >= top_k, cand, theta)

    return lax.fori_loop(0, 32, body, jnp.full(shape, INT_MIN, jnp.int32))


def _select_topk(key_ref, admissible, top_k, mask_ref, tri_ref, ka):
    theta = _kth_largest(key_ref, top_k, ka)
    key = key_ref[...]
    gt = key > theta
    eq = key == theta
    need = top_k - _count(gt, ka)
    all_row = theta == jnp.int32(INT_MIN)
    simple = jnp.logical_or(_count(eq, ka) == need, all_row)
    n_bad = jnp.sum(jnp.where(simple, 0.0, 1.0))

    @pl.when(n_bad == 0.0)
    def _():
        mask_ref[...] = jnp.logical_and(key >= theta, admissible).astype(F32)

    @pl.when(n_bad != 0.0)
    def _():
        run = jnp.zeros(theta.shape, F32)
        for c0 in range(0, key.shape[ka], LANES):
            grp = (slice(None), slice(c0, c0 + LANES)) if ka == 1 else (slice(c0, c0 + LANES), slice(None))
            eq_c = eq[grp].astype(F32)
            if ka == 1:
                before = _dot(eq_c.astype(BF16), tri_ref[...]) + run
            else:
                before = _dot(tri_ref[...], eq_c.astype(BF16)) + run
            take = jnp.logical_and(eq[grp], before < need)
            m = jnp.logical_and(jnp.logical_or(gt[grp], take), admissible[grp])
            mask_ref[grp] = m.astype(F32)
            run = run + jnp.sum(eq_c, axis=ka, keepdims=True)


def _init_tri(tri_ref, ka):
    r = lax.broadcasted_iota(jnp.int32, (LANES, LANES), 0)
    c = lax.broadcasted_iota(jnp.int32, (LANES, LANES), 1)
    tri_ref[...] = ((r < c) if ka == 1 else (r > c)).astype(BF16)


def _attn_p_kernel(bias_ref, qb_ref, kb_ref, vb_ref, qidx_ref, miscq_ref, misck_ref, o_ref,
                   k_scr, vt_scr, kidx_scr, btab_scr, mask_scr, key_scr, tri_scr, *, tq, top_k, n_groups):
    b = pl.program_id(0)
    i = pl.program_id(1)
    nq = pl.num_programs(1)

    @pl.when(jnp.logical_and(b == 0, i == 0))
    def _():
        _init_tri(tri_scr, 0)
        kk = lax.broadcasted_iota(jnp.int32, (tq, tq), 0)
        qq = lax.broadcasted_iota(jnp.int32, (tq, tq), 1)
        for m in range(2 * nq - 1):
            dd = nq - 1 - m
            if dd >= 0:
                tiles = _rel_bias_tile(dd * tq + qq - kk, bias_ref)
                for h in range(H_B):
                    btab_scr[h, m * tq:(m + 1) * tq, :] = tiles[h]
            else:
                for h in range(H_B):
                    btab_scr[h, m * tq:(m + 1) * tq, :] = jnp.zeros((tq, tq), F32)

    @pl.when(i == 0)
    def _():
        t_keys = k_scr.shape[0]
        for h in range(H_B):
            hd = slice(h * DH_B, (h + 1) * DH_B)
            k_scr[:, hd] = kb_ref[0, pl.ds(h, t_keys, stride=H_B), :].astype(BF16)
            vt_scr[hd, :] = vb_ref[0, pl.ds(h, t_keys, stride=H_B), :].T.astype(BF16)
        kidx_scr[...] = misck_ref[0, :, MISC_KIDX:MISC_KIDX + D_IDX].astype(BF16)

    def attend(n_keys):
        qidx = qidx_ref[0]
        misc_t = miscq_ref[0].T
        kidx = kidx_scr[0:n_keys, :]
        score = jnp.zeros((n_keys, tq), F32)
        for h in range(H_IDX):
            dots = _dot_nt(kidx, qidx[:, h * D_IDX:(h + 1) * D_IDX].astype(BF16))
            w = misc_t[MISC_WIDX + h:MISC_WIDX + h + 1, :] * (H_IDX ** -0.5) * (D_IDX ** -0.5)
            score = score + jnp.maximum(dots, 0.0) * w

        kpos = lax.broadcasted_iota(jnp.int32, (n_keys, tq), 0)
        qpos = i * tq + lax.broadcasted_iota(jnp.int32, (n_keys, tq), 1)
        admissible = kpos <= qpos
        key_ref = key_scr.at[0:n_keys, :]
        key_ref[...] = _order_key(score, admissible)
        mask_ref = mask_scr.at[0:n_keys, :]
        _select_topk(key_ref, admissible, top_k, mask_ref, tri_scr, 0)

        sel = mask_ref[...] > 0.5
        qb = qb_ref[0]
        boff = pl.multiple_of((nq - 1 - i) * tq, tq)
        for h in range(H_B):
            hd = slice(h * DH_B, (h + 1) * DH_B)
            logits = _dot_nt(k_scr[0:n_keys, hd], qb[:, hd].astype(BF16)) * (DH_B ** -0.5)
            logits = logits + btab_scr[h, pl.ds(boff, n_keys), :]
            logits = jnp.where(sel, logits, NEG_INF)
            m = jnp.max(logits, axis=0, keepdims=True)
            p = jnp.exp(logits - m)
            denom = jnp.sum(p, axis=0, keepdims=True)
            out_t = _dot(vt_scr[hd, 0:n_keys], p.astype(BF16)) / denom
            o_ref[0, :, hd] = out_t.T

    per_group = nq // n_groups
    for grp in range(n_groups):
        pl.when(i // per_group == grp)(functools.partial(attend, (grp + 1) * per_group * tq))


def _attn_p(rel_bias, q_b, k_b, v_b, q_idx, misc, tq=128):
    b, t, _ = q_b.shape
    nq = t // tq
    top_k = min(TOPK_MAX, t // 4)
    n_groups = math.gcd(nq, 8)
    qblk = lambda w: pl.BlockSpec((1, tq, w), lambda bi, i: (bi, i, 0))
    full = lambda w: pl.BlockSpec((1, t, w), lambda bi, i: (bi, 0, 0))
    kv = pl.BlockSpec((1, t * H_B, DH_B), lambda bi, i: (bi, 0, 0))
    return pl.pallas_call(
        functools.partial(_attn_p_kernel, tq=tq, top_k=top_k, n_groups=n_groups),
        grid=(b, nq),
        in_specs=[pl.BlockSpec(memory_space=pltpu.SMEM),
                  qblk(HB), kv, kv, qblk(H_IDX * D_IDX), qblk(LANES), full(LANES)],
        out_specs=qblk(HB),
        out_shape=jax.ShapeDtypeStruct((b, t, HB), F32),
        scratch_shapes=[pltpu.VMEM((t, HB), BF16), pltpu.VMEM((HB, t), BF16),
                        pltpu.VMEM((t, D_IDX), BF16),
                        pltpu.VMEM((H_B, (2 * nq - 1) * tq, tq), F32),
                        pltpu.VMEM((t, tq), F32), pltpu.VMEM((t, tq), jnp.int32),
                        pltpu.VMEM((LANES, LANES), BF16)],
        compiler_params=_cparams(2),
        name="attn_p",
    )(rel_bias, q_b, k_b, v_b, q_idx, misc, misc)


def _attn_s_kernel(pt_ref, bias_ref, qb_ref, qidx_ref, widx_ref, knew_ref, vnew_ref, kidxnew_ref,
                   cidx_hbm, ck_hbm, cv_hbm, o_ref,
                   score_scr, logit_scr, mask_scr, key_scr, acc_scr, btab_scr, tri_scr, idx_buf, k_buf, v_buf, sems,
                   *, layer, t_new, n_pages, gp, top_k):
    b = pl.program_id(0)
    n_grp = n_pages // gp
    past = n_pages * PAGE_SIZE
    n_cols = past + PAGE_SIZE
    near0 = max(0, (past - MAX_DIST) // PAGE_SIZE) * PAGE_SIZE
    n_near = n_cols - near0
    pg_rows = PAGE_SIZE * H_B
    grp_keys = gp * PAGE_SIZE
    k_ahead = min(2, n_grp)

    @pl.when(b == 0)
    def _():
        _init_tri(tri_scr, 1)
        trow = lax.broadcasted_iota(jnp.int32, (t_new, n_near), 0)
        col = near0 + lax.broadcasted_iota(jnp.int32, (t_new, n_near), 1)
        tiles = _rel_bias_tile(past + trow - col, bias_ref)
        for h in range(H_B):
            btab_scr[h] = tiles[h]

    def group_copies(hbm, buf, sem_row, slot, grp, seq=None):
        out = []
        for m in range(gp):
            page = pt_ref[b if seq is None else seq, grp * gp + m]
            dst = buf.at[slot, m] if buf is idx_buf else buf.at[slot, pl.ds(m * pg_rows, pg_rows)]
            out.append(pltpu.make_async_copy(hbm.at[layer, page], dst, sems.at[sem_row, slot]))
        return out

    def start_idx_k(slot, grp, seq=None):
        for c in (group_copies(cidx_hbm, idx_buf, 0, slot, grp, seq)
                  + group_copies(ck_hbm, k_buf, 1, slot, grp, seq)):
            c.start()

    def wait_idx_k(slot, grp):
        for c in group_copies(cidx_hbm, idx_buf, 0, slot, grp) + group_copies(ck_hbm, k_buf, 1, slot, grp):
            c.wait()

    def start_v(slot, grp):
        for c in group_copies(cv_hbm, v_buf, 2, slot, grp):
            c.start()

    def wait_v(slot, grp):
        for c in group_copies(cv_hbm, v_buf, 2, slot, grp):
            c.wait()

    qb = qb_ref[0].astype(BF16)
    qidx = qidx_ref[0].astype(BF16)
    widx = widx_ref[0] * (H_IDX ** -0.5)
    head_rows = lambda h: slice(h * t_new, (h + 1) * t_new)
    head_lanes = lambda h: slice(h * DH_B, (h + 1) * DH_B)

    def scores_of(kidx_t):
        dots = _dot(qidx, kidx_t.astype(BF16)) * (D_IDX ** -0.5)
        contrib = jnp.maximum(dots, 0.0) * widx
        s = contrib[0:t_new]
        for h in range(1, H_IDX):
            s = s + contrib[h * t_new:(h + 1) * t_new]
        return s

    @pl.when(b == 0)
    def _():
        for g in range(k_ahead):
            start_idx_k(g, g)

    def pass1(grp, carry):
        slot = grp % 2

        @pl.when(jnp.logical_and(grp + 1 < n_grp, grp + 1 >= k_ahead))
        def _():
            start_idx_k(1 - slot, grp + 1)

        wait_idx_k(slot, grp)
        c0 = pl.multiple_of(grp * grp_keys, grp_keys)
        for m in range(gp):
            score_scr[:, pl.ds(c0 + m * PAGE_SIZE, PAGE_SIZE)] = scores_of(idx_buf[slot, m])
        for h in range(H_B):
            k_h = k_buf[slot, pl.ds(h, grp_keys, stride=H_B), :].astype(BF16)
            logit_scr[head_rows(h), pl.ds(c0, grp_keys)] = _dot_nt(qb[:, head_lanes(h)], k_h)
        return carry

    lax.fori_loop(0, n_grp, pass1, 0)
    for g in range(min(V_SLOTS - 1, n_grp)):
        start_v(g, g)

    score_scr[:, past:n_cols] = scores_of(kidxnew_ref[0])
    for h in range(H_B):
        k_h = knew_ref[0, pl.ds(h, PAGE_SIZE, stride=H_B), :].astype(BF16)
        logit_scr[head_rows(h), past:n_cols] = _dot_nt(qb[:, head_lanes(h)], k_h)

    trow = lax.broadcasted_iota(jnp.int32, (t_new, n_cols), 0)
    col = lax.broadcasted_iota(jnp.int32, (t_new, n_cols), 1)
    admissible = col <= past + trow
    key_scr[...] = _order_key(score_scr[...], admissible)
    _select_topk(key_scr, admissible, top_k, mask_scr, tri_scr, 1)
    sel = mask_scr[...] > 0.5
    for h in range(H_B):
        rs = head_rows(h)
        near = logit_scr[rs, near0:n_cols] * (DH_B ** -0.5) + btab_scr[h]
        near = jnp.where(sel[:, near0:n_cols], near, NEG_INF)
        mx = jnp.max(near, axis=-1, keepdims=True)
        if near0 > 0:
            far = logit_scr[rs, 0:near0] * (DH_B ** -0.5) + bias_ref[N_BUCKETS - 1, h]
            far = jnp.where(sel[:, 0:near0], far, NEG_INF)
            mx = jnp.maximum(mx, jnp.max(far, axis=-1, keepdims=True))
        pn = jnp.exp(near - mx)
        denom = jnp.sum(pn, axis=-1, keepdims=True)
        if near0 > 0:
            pf = jnp.exp(far - mx)
            denom = denom + jnp.sum(pf, axis=-1, keepdims=True)
            logit_scr[rs, 0:near0] = pf / denom
        logit_scr[rs, near0:n_cols] = pn / denom
        v_h = vnew_ref[0, pl.ds(h, PAGE_SIZE, stride=H_B), :].astype(BF16)
        acc_scr[rs, :] = _dot(logit_scr[rs, past:n_cols].astype(BF16), v_h)

    @pl.when(b + 1 < pl.num_programs(0))
    def _():
        for g in range(k_ahead):
            start_idx_k(g, g, b + 1)

    def pass2(grp, carry):
        slot = grp % V_SLOTS
        ahead = grp + V_SLOTS - 1

        @pl.when(ahead < n_grp)
        def _():
            start_v(ahead % V_SLOTS, ahead)

        wait_v(slot, grp)
        c0 = pl.multiple_of(grp * grp_keys, grp_keys)
        for h in range(H_B):
            v_h = v_buf[slot, pl.ds(h, grp_keys, stride=H_B), :].astype(BF16)
            p_h = logit_scr[head_rows(h), pl.ds(c0, grp_keys)].astype(BF16)
            acc_scr[head_rows(h), :] += _dot(p_h, v_h)
        return carry

    lax.fori_loop(0, n_grp, pass2, 0)
    for h in range(H_B):
        o_ref[0, :, head_lanes(h)] = acc_scr[head_rows(h), :]


def _attn_s(page_table, rel_bias, q_b, k_new, v_new, q_idx, kidx_new, misc, cache_k, cache_v, cache_idx_k,
            layer):
    b, t_new, _ = q_b.shape
    n_pages = page_table.shape[1]
    past = n_pages * PAGE_SIZE
    n_cols = past + PAGE_SIZE
    top_k = min(TOPK_MAX, (past + t_new) // 4)
    gp = math.gcd(n_pages, 16)
    hq = H_B * t_new
    near0 = max(0, (past - MAX_DIST) // PAGE_SIZE) * PAGE_SIZE

    qidx = q_idx.reshape(b, t_new, H_IDX, D_IDX).transpose(0, 2, 1, 3).reshape(b, H_IDX * t_new, D_IDX)
    widx = misc[:, :, MISC_WIDX:MISC_WIDX + H_IDX].transpose(0, 2, 1).reshape(b, H_IDX * t_new, 1)
    pad_page = lambda a, r: jnp.pad(a, ((0, 0), (0, (PAGE_SIZE - t_new) * r), (0, 0)))
    k_new, v_new = pad_page(k_new, H_B), pad_page(v_new, H_B)
    kidx_new_t = pad_page(kidx_new, 1).transpose(0, 2, 1)
    ck = cache_k.reshape(cache_k.shape[0], cache_k.shape[1], PAGE_SIZE * H_B, DH_B)
    cv = cache_v.reshape(cache_v.shape[0], cache_v.shape[1], PAGE_SIZE * H_B, DH_B)
    cidx_t = cache_idx_k.transpose(0, 1, 3, 2)

    per_b = lambda s: pl.BlockSpec((1,) + s, lambda bi, pt: (bi, 0, 0))
    hbm = pl.BlockSpec(memory_space=pl.ANY)
    in_specs = [pl.BlockSpec(memory_space=pltpu.SMEM),
                per_b((t_new, HB)), per_b((H_IDX * t_new, D_IDX)), per_b((H_IDX * t_new, 1)),
                per_b((PAGE_SIZE * H_B, DH_B)), per_b((PAGE_SIZE * H_B, DH_B)), per_b((D_IDX, PAGE_SIZE)),
                hbm, hbm, hbm]
    grid_spec = pltpu.PrefetchScalarGridSpec(
        num_scalar_prefetch=1,
        grid=(b,),
        in_specs=in_specs,
        out_specs=pl.BlockSpec((1, t_new, HB), lambda bi, pt: (bi, 0, 0)),
        scratch_shapes=[pltpu.VMEM((t_new, n_cols), F32), pltpu.VMEM((hq, n_cols), F32),
                        pltpu.VMEM((t_new, n_cols), F32), pltpu.VMEM((t_new, n_cols), jnp.int32),
                        pltpu.VMEM((hq, DH_B), F32),
                        pltpu.VMEM((H_B, t_new, n_cols - near0), F32),
                        pltpu.VMEM((LANES, LANES), BF16),
                        pltpu.VMEM((2, gp, D_IDX, PAGE_SIZE), F32),
                        pltpu.VMEM((2, gp * PAGE_SIZE * H_B, DH_B), F32),
                        pltpu.VMEM((V_SLOTS, gp * PAGE_SIZE * H_B, DH_B), F32),
                        pltpu.SemaphoreType.DMA((3, V_SLOTS))])
    return pl.pallas_call(
        functools.partial(_attn_s_kernel, layer=layer, t_new=t_new, n_pages=n_pages, gp=gp, top_k=top_k),
        grid_spec=grid_spec,
        out_shape=jax.ShapeDtypeStruct((b, t_new, HB), F32),
        compiler_params=_cparams(1),
        name="attn_s",
    )(page_table, rel_bias, q_b, qidx, widx, k_new, v_new, kidx_new_t, cidx_t, ck, cv)


def _ffn_kernel(x_ref, oa_ref, ob_ref, ga_ref, gb_ref, fst_ref, wpa_ref, wpb_ref, wout_ref, n2_ref,
                wup_ref, cfw_ref, cfb_ref, wdown_ref, fg_ref, y_ref, fnew_ref, prev_scr, act_scr,
                *, tm, ch, d_ff, seq):
    t = pl.program_id(1)
    nt = pl.num_programs(1)
    npr = CONV_F - 1
    if seq is None:
        @pl.when(t == 0)
        def _():
            prev_scr[SUBLANES - npr:SUBLANES, :] = fst_ref[0]

    pa = _dot(oa_ref[0].astype(BF16), wpa_ref[...])
    pb = _dot(ob_ref[0].astype(BF16), wpb_ref[...])
    merged = jax.nn.sigmoid(ga_ref[0]) * pa + jax.nn.sigmoid(gb_ref[0]) * pb
    x1 = x_ref[0] + _dot(merged.astype(BF16), wout_ref[...])
    xn2 = _rms(x1, n2_ref[...]).astype(BF16)

    def conv_cols(c0):
        cols = slice(c0, c0 + ch)
        up = _dot(xn2, wup_ref[:, cols])
        if seq is None:
            prev = prev_scr[:, cols]
            s2, s1 = _shift_rows(up, 2, prev), _shift_rows(up, 1, prev)
            prev_scr[SUBLANES - npr:SUBLANES, cols] = up[tm - npr:tm, :]
        else:
            hist = fst_ref[0, :, cols]
            tpos = lax.broadcasted_iota(jnp.int32, up.shape, 0) & (seq - 1)
            s2 = jnp.where(tpos < 2, hist, pltpu.roll(up, 2, 0))
            s1 = jnp.where(tpos < 1, pltpu.roll(hist, tm - 1, 0), pltpu.roll(up, 1, 0))
            fnew_ref[:, :, cols] = up.reshape(tm // seq, seq, ch)[:, seq - npr:seq, :]
        out = s2 * cfw_ref[0:1, cols]
        out = out + s1 * cfw_ref[1:2, cols]
        out = out + up * cfw_ref[2:3, cols]
        return out + cfb_ref[:, cols]

    for c0 in range(0, d_ff, ch):
        gate = conv_cols(c0)
        val = conv_cols(d_ff + c0)
        act_scr[:, c0:c0 + ch] = (_silu(gate) * val).astype(BF16)
    y_ref[0] = _rms(x1 + _dot(act_scr[...], wdown_ref[...]), fg_ref[...])

    if seq is None:
        @pl.when(t == nt - 1)
        def _():
            fnew_ref[0] = prev_scr[SUBLANES - npr:SUBLANES, :]


def _ffn(x, o_a, o_b, gate_a, gate_b, ffn_state, wpa, wpb, wout, n2_row, wup, cfw, cfb_row, wdown, fg_row, tm):
    b, t, d = x.shape
    d_ff = wdown.shape[0]
    ch = 256
    npr = CONV_F - 1
    consts = (wpa, wpb, wout, n2_row, wup, cfw, cfb_row, wdown, fg_row)
    if t == SUBLANES and t > npr:
        seq, rows = t, b * t
        flat = lambda a: a.reshape(1, rows, a.shape[-1])
        x, o_a, o_b, gate_a, gate_b = map(flat, (x, o_a, o_b, gate_a, gate_b))
        hist = flat(jnp.pad(ffn_state, ((0, 0), (0, t - npr), (0, 0))))
        grid, tm = (1, 1), rows
        blk = lambda w: pl.BlockSpec((1, rows, w), lambda i, j: (0, 0, 0))
        state_in, state_out = blk(2 * d_ff), pl.BlockSpec((b, npr, 2 * d_ff), lambda i, j: (0, 0, 0))
    else:
        seq, hist = None, ffn_state
        grid = (b, t // tm)
        blk = lambda w: pl.BlockSpec((1, tm, w), lambda i, j: (i, j, 0))
        state_in = state_out = pl.BlockSpec((1, npr, 2 * d_ff), lambda i, j: (i, 0, 0))
    y, ffn_new = pl.pallas_call(
        functools.partial(_ffn_kernel, tm=tm, ch=ch, d_ff=d_ff, seq=seq),
        grid=grid,
        in_specs=[blk(d), blk(o_a.shape[-1]), blk(o_b.shape[-1]), blk(d), blk(d), state_in]
                 + [_const_spec(c.shape) for c in consts],
        out_specs=[blk(d), state_out],
        out_shape=[jax.ShapeDtypeStruct(x.shape, F32),
                   jax.ShapeDtypeStruct((b, npr, 2 * d_ff), F32)],
        scratch_shapes=[pltpu.VMEM((SUBLANES, 2 * d_ff), F32), pltpu.VMEM((tm, d_ff), BF16)],
        compiler_params=_cparams(2),
        name="ffn",
    )(x, o_a, o_b, gate_a, gate_b, hist, *consts)
    return y.reshape(b, t, d), ffn_new


def _misc_row(vals, col0):
    return jnp.zeros((1, LANES), F32).at[0, col0:col0 + vals.shape[0]].set(vals.astype(F32))


def _layer(x, conv_state, rec_state, ffn_state, attend, lw, proj_tm, gdn_rows, ffn_tm):
    b, t, d = x.shape
    (w_in_bf, conv_a_w, alog_row, dtb_row, ng_row, wpa, wpb, wout, n1_row, n2_row, wup, cfw, cfb_row,
     wdown, fg_row) = lw
    (qkv_a, z_a, q_b, k_b, v_b, q_idx, gate_a, gate_b, misc, k_idx) = _proj(
        x.reshape(b * t, d), n1_row, w_in_bf, proj_tm)
    r3 = lambda a: a.reshape(b, -1, a.shape[-1])
    qkv_a, z_a, q_b, k_b, v_b, q_idx, gate_a, gate_b, misc, k_idx = map(
        r3, (qkv_a, z_a, q_b, k_b, v_b, q_idx, gate_a, gate_b, misc, k_idx))
    o_a, conv_new, rec_new = _gdn(qkv_a, misc, z_a, conv_state, rec_state, conv_a_w, alog_row, dtb_row,
                                  ng_row, gdn_rows)
    o_b = attend(q_b, k_b, v_b, q_idx, k_idx, misc)
    y, ffn_new = _ffn(x, o_a, o_b, gate_a, gate_b, ffn_state, wpa, wpb, wout, n2_row, wup, cfw, cfb_row,
                      wdown, fg_row, ffn_tm)
    kv_shape = (b, t, H_B, DH_B)
    return y, (k_b.reshape(kv_shape), v_b.reshape(kv_shape), k_idx, conv_new, rec_new, ffn_new)


def kernel(x_prompt, x_sample, cache_k, cache_v, cache_idx_k, state_a_conv, state_a_rec, state_ffn_conv,
           page_table, w_in, conv_a_w, a_log, dt_bias, norm_a_g, w_proj_a, w_proj_b, w_out, rel_bias,
           norm1_g, norm2_g, w_up, conv_f_w, conv_f_b, w_down, final_g):
    depth = w_in.shape[0]
    assert depth == 1, "the final norm is fused into the layer's last kernel"
    bp, tp, d = x_prompt.shape
    bs, ts, _ = x_sample.shape
    d_ff = w_down.shape[1]
    l = 0
    lw = (_prep_w_in(w_in[l], d), conv_a_w[l], _misc_row(a_log[l], MISC_A), _misc_row(dt_bias[l], MISC_A),
          norm_a_g[l].reshape(1, -1), w_proj_a[l].astype(BF16), w_proj_b[l].astype(BF16),
          w_out[l].astype(BF16), norm1_g[l].reshape(1, -1), norm2_g[l].reshape(1, -1),
          w_up[l].astype(BF16), conv_f_w[l], conv_f_b[l].reshape(1, -1), w_down[l].astype(BF16),
          final_g.reshape(1, -1))

    def attend_prompt(q_b, k_b, v_b, q_idx, k_idx, misc):
        return _attn_p(rel_bias, q_b, k_b, v_b, q_idx, misc)

    def attend_sample(q_b, k_b, v_b, q_idx, k_idx, misc):
        return _attn_s(page_table, rel_bias, q_b, k_b, v_b, q_idx, k_idx, misc, cache_k, cache_v,
                       cache_idx_k, l)

    zeros = lambda *s: jnp.zeros(s, x_prompt.dtype)
    y_p, st_p = _layer(x_prompt, zeros(bp, CONV_A - 1, QKV_A), zeros(bp, H_A, DK_A, DV_A),
                       zeros(bp, CONV_F - 1, 2 * d_ff), attend_prompt, lw,
                       proj_tm=256, gdn_rows=min(tp, 256), ffn_tm=min(tp, 256))
    y_s, st_s = _layer(x_sample, state_a_conv[l], state_a_rec[l], state_ffn_conv[l], attend_sample, lw,
                       proj_tm=min(bs * ts, 256), gdn_rows=ts, ffn_tm=ts)
    stack = lambda a: a[None]
    return (y_p, y_s) + tuple(map(stack, st_p)) + tuple(map(stack, st_s))
```

```python
import functools
import math

import jax
import jax.numpy as jnp
from jax import lax
from jax.experimental import pallas as pl
from jax.experimental.pallas import tpu as pltpu

F32 = jnp.float32
BF16 = jnp.bfloat16
HI = lax.Precision.HIGHEST

H_A, DK_A, DV_A, CONV_A, CHUNK_A = 4, 128, 128, 4, 64
H_B, DH_B, H_IDX, D_IDX = 4, 128, 8, 64
TOPK_MAX, N_BUCKETS, MAX_DIST = 256, 32, 2048
CONV_F, EPS, PAGE_SIZE = 3, 1e-6, 128
QKV_A = H_A * (2 * DK_A + DV_A)
HB = H_B * DH_B

LANES = 128
SUBLANES = 8
VMEM_LIMIT = 56 * 1024 * 1024

MISC_KIDX, MISC_BETA, MISC_A, MISC_WIDX = 0, D_IDX, D_IDX + H_A, D_IDX + 2 * H_A

GDN_MODE_QK = "bf"
GDN_MODE_INV = "bf"
GDN_MODE_STATE = "bf"

INT_MIN = -2 ** 31
COUNT_ACCS = 4
V_SLOTS = 6
NEG_INF = float("-inf")


def _cparams(n_axes):
    return pltpu.CompilerParams(dimension_semantics=("arbitrary",) * n_axes,
                                vmem_limit_bytes=VMEM_LIMIT)


def _const_spec(shape):
    nd = len(shape)
    return pl.BlockSpec(shape, lambda *_: (0,) * nd, pipeline_mode=pl.Buffered(1))


def _silu(x):
    return x * jax.nn.sigmoid(x)


def _dot(a, b, precision=None):
    return jnp.dot(a, b, preferred_element_type=F32, precision=precision)


def _dot_nt(a, b, precision=None):
    return lax.dot_general(a, b, (((1,), (1,)), ((), ())), preferred_element_type=F32,
                           precision=precision)


_DIMS = {"nn": (((1,), (0,)), ((), ())), "nt": (((1,), (1,)), ((), ())), "tn": (((0,), (0,)), ((), ()))}


def _mm(a, b, mode, form="nn"):
    if mode == "bf":
        return lax.dot_general(a.astype(BF16), b.astype(BF16), _DIMS[form], preferred_element_type=F32)
    return lax.dot_general(a, b, _DIMS[form], preferred_element_type=F32, precision=HI)


def _cumsum_rows(x):
    row = lax.broadcasted_iota(jnp.int32, x.shape, 0)
    s = 1
    while s < x.shape[0]:
        x = x + jnp.where(row >= s, pltpu.roll(x, s, 0), 0.0)
        s *= 2
    return x


def _rms(x, g):
    return x * lax.rsqrt(jnp.mean(x * x, axis=-1, keepdims=True) + EPS) * g


def _shift_rows(x, k, prev):
    n = prev.shape[0]
    out = pltpu.roll(x, k, 0)
    head = out[0:SUBLANES]
    row = lax.broadcasted_iota(jnp.int32, head.shape, 0)
    for r in range(k):
        head = jnp.where(row == r, prev[n - k + r:n - k + r + 1, :], head)
    return head if x.shape[0] == SUBLANES else jnp.concatenate([head, out[SUBLANES:]], axis=0)


def _proj_kernel(x_ref, g_ref, w_ref, qkva_ref, za_ref, qb_ref, kb_ref, vb_ref, qidx_ref,
                 ga_ref, gb_ref, misc_ref, kidx_ref):
    xn = _rms(x_ref[...], g_ref[...]).astype(BF16)
    tm = x_ref.shape[0]
    off = 0
    for o_ref in (qkva_ref, za_ref, qb_ref, kb_ref, vb_ref, qidx_ref, ga_ref, gb_ref, misc_ref):
        if o_ref is kb_ref or o_ref is vb_ref:
            res = _dot(xn, w_ref[:, off:off + HB])
            for h in range(H_B):
                o_ref[pl.ds(h, tm, stride=H_B), :] = res[:, h * DH_B:(h + 1) * DH_B]
            off += HB
        else:
            wd = o_ref.shape[-1]
            o_ref[...] = _dot(xn, w_ref[:, off:off + wd])
            off += wd
    kidx_ref[...] = misc_ref[:, MISC_KIDX:MISC_KIDX + D_IDX]


def _prep_w_in(w_in_l, d_model):
    sizes = (QKV_A, H_A, H_A, H_A * DV_A, 3 * HB, H_IDX * D_IDX, D_IDX, H_IDX, d_model, d_model)
    parts, start = [], 0
    for s in sizes:
        parts.append(w_in_l[:, start:start + s])
        start += s
    qkv_a, beta, a_raw, z_a, qkv_b, q_idx, k_idx, w_idx, gate_a, gate_b = parts
    pad = jnp.zeros((w_in_l.shape[0], LANES - D_IDX - 2 * H_A - H_IDX), w_in_l.dtype)
    misc = jnp.concatenate([k_idx, beta, a_raw, w_idx, pad], axis=1)
    return jnp.concatenate([qkv_a, z_a, qkv_b, q_idx, gate_a, gate_b, misc], axis=1).astype(BF16)


def _proj(x2d, g_row, w_bf, tm):
    n, d = x2d.shape
    shapes = ((1, QKV_A), (1, H_A * DV_A), (1, HB), (H_B, DH_B), (H_B, DH_B), (1, H_IDX * D_IDX), (1, d), (1, d),
              (1, LANES), (1, D_IDX))
    row = lambda i: (i, 0)
    return pl.pallas_call(
        _proj_kernel,
        grid=(n // tm,),
        in_specs=[pl.BlockSpec((tm, d), row), _const_spec(g_row.shape), _const_spec(w_bf.shape)],
        out_specs=[pl.BlockSpec((tm * r, wd), row) for r, wd in shapes],
        out_shape=[jax.ShapeDtypeStruct((n * r, wd), F32) for r, wd in shapes],
        compiler_params=_cparams(1),
        name="proj",
    )(x2d, g_row, w_bf)


def _gdn_kernel(qkv_ref, misc_ref, z_ref, cst_ref, rec0_ref, cw_ref, alog_ref, dtb_ref, ng_ref,
                o_ref, cnew_ref, rnew_ref, s_scr, prev_scr, *, rows, chunk):
    t = pl.program_id(1)
    nt = pl.num_programs(1)
    npr = CONV_A - 1

    @pl.when(t == 0)
    def _():
        s_scr[...] = rec0_ref[0]
        prev_scr[SUBLANES - npr:SUBLANES, :] = cst_ref[0]

    x = qkv_ref[0]
    prev = prev_scr[...]
    conv = _shift_rows(x, 3, prev) * cw_ref[0:1, :]
    conv = conv + _shift_rows(x, 2, prev) * cw_ref[1:2, :]
    conv = conv + _shift_rows(x, 1, prev) * cw_ref[2:3, :]
    conv = conv + x * cw_ref[3:4, :]
    prev_scr[SUBLANES - npr:SUBLANES, :] = x[rows - npr:rows, :]

    @pl.when(t == nt - 1)
    def _():
        cnew_ref[0] = x[rows - npr:rows, :]

    act = _silu(conv)
    misc = misc_ref[0]
    beta_full = jax.nn.sigmoid(misc)
    sp = misc + dtb_ref[...]
    softplus = jnp.maximum(sp, 0.0) + jnp.log1p(jnp.exp(-jnp.abs(sp)))
    g_full = -jnp.exp(alog_ref[...]) * softplus

    ri = lax.broadcasted_iota(jnp.int32, (chunk, chunk), 0)
    ci = lax.broadcasted_iota(jnp.int32, (chunk, chunk), 1)
    tril = ri >= ci
    strict = ri > ci
    eye = (ri == ci).astype(F32)
    lane = lax.broadcasted_iota(jnp.int32, (chunk, LANES), 1)
    n_levels = int(math.log2(chunk))
    hk = H_A * DK_A
    chunks = list(range(0, rows, chunk))
    pairs = [(c0, h) for c0 in chunks for h in range(H_A)]

    gcum = {c0: _cumsum_rows(g_full[c0:c0 + chunk, :]) for c0 in chunks}
    gcum_t = {c0: gcum[c0].T for c0 in chunks} if chunk >= 64 else None

    pre = {}
    for c0, h in pairs:
        q = act[c0:c0 + chunk, h * DK_A:(h + 1) * DK_A]
        k = act[c0:c0 + chunk, hk + h * DK_A:hk + (h + 1) * DK_A]
        v = act[c0:c0 + chunk, 2 * hk + h * DV_A:2 * hk + (h + 1) * DV_A]
        q = q * lax.rsqrt(jnp.sum(q * q, axis=-1, keepdims=True) + EPS) * (DK_A ** -0.5)
        k = k * lax.rsqrt(jnp.sum(k * k, axis=-1, keepdims=True) + EPS)
        beta = beta_full[c0:c0 + chunk, MISC_BETA + h:MISC_BETA + h + 1]
        gc = gcum[c0][:, MISC_A + h:MISC_A + h + 1]
        if gcum_t is not None:
            g_row = gcum_t[c0][MISC_A + h:MISC_A + h + 1, :]
        else:
            g_row = _mm((lane == MISC_A + h).astype(F32), gcum[c0], "hi", "nt")
        gamma = jnp.where(tril, jnp.exp(jnp.where(tril, gc - g_row, 0.0)), 0.0)
        kb = k * beta
        egc = jnp.exp(gc)
        g_last = gc[chunk - 1:chunk, :]
        pre[c0, h] = dict(
            xp=-jnp.where(strict, _mm(kb, k, GDN_MODE_QK, "nt") * gamma, 0.0),
            rhs=jnp.concatenate([kb * egc, v * beta], axis=-1),
            attn=jnp.where(tril, _mm(q, k, GDN_MODE_QK, "nt") * gamma, 0.0),
            q_dec=q * egc, k_dec=k * jnp.exp(g_last - gc), e_last=jnp.exp(g_last))
    for p in pre.values():
        p["inv"] = eye + p["xp"]
    for _ in range(n_levels - 1):
        for p in pre.values():
            p["xp"] = _mm(p["xp"], p["xp"], GDN_MODE_INV)
        for p in pre.values():
            p["inv"] = p["inv"] + _mm(p["inv"], p["xp"], GDN_MODE_INV)
    for p in pre.values():
        p["sol"] = _mm(p["inv"], p["rhs"], GDN_MODE_INV)

    for c0, h in pairs:
        p = pre[c0, h]
        s = s_scr[h]
        v_new = p["sol"][:, DK_A:] - _mm(p["sol"][:, :DK_A], s, GDN_MODE_STATE)
        o = _mm(p["q_dec"], s, GDN_MODE_STATE) + _mm(p["attn"], v_new, GDN_MODE_STATE)
        s_scr[h] = s * p["e_last"] + _mm(p["k_dec"], v_new, GDN_MODE_STATE, "tn")
        z = z_ref[0, c0:c0 + chunk, h * DV_A:(h + 1) * DV_A]
        o_ref[0, c0:c0 + chunk, h * DV_A:(h + 1) * DV_A] = _rms(o, ng_ref[...]) * _silu(z)

    @pl.when(t == nt - 1)
    def _():
        rnew_ref[0] = s_scr[...]


def _gdn(qkv_a, misc, z_a, conv_state, rec0, conv_w, alog_row, dtb_row, ng_row, rows):
    b, t, _ = qkv_a.shape
    chunk = math.gcd(t, CHUNK_A)
    blk = lambda w: pl.BlockSpec((1, rows, w), lambda i, j: (i, j, 0))
    per_b3 = lambda s: pl.BlockSpec((1,) + s, lambda i, j: (i, 0, 0))
    per_b4 = lambda s: pl.BlockSpec((1,) + s, lambda i, j: (i, 0, 0, 0))
    return pl.pallas_call(
        functools.partial(_gdn_kernel, rows=rows, chunk=chunk),
        grid=(b, t // rows),
        in_specs=[blk(QKV_A), blk(LANES), blk(H_A * DV_A),
                  per_b3((CONV_A - 1, QKV_A)), per_b4((H_A, DK_A, DV_A)),
                  _const_spec(conv_w.shape), _const_spec(alog_row.shape),
                  _const_spec(dtb_row.shape), _const_spec(ng_row.shape)],
        out_specs=[blk(H_A * DV_A), per_b3((CONV_A - 1, QKV_A)), per_b4((H_A, DK_A, DV_A))],
        out_shape=[jax.ShapeDtypeStruct((b, t, H_A * DV_A), F32),
                   jax.ShapeDtypeStruct((b, CONV_A - 1, QKV_A), F32),
                   jax.ShapeDtypeStruct((b, H_A, DK_A, DV_A), F32)],
        scratch_shapes=[pltpu.VMEM((H_A, DK_A, DV_A), F32), pltpu.VMEM((SUBLANES, QKV_A), F32)],
        compiler_params=_cparams(2),
        name="gdn",
    )(qkv_a, misc, z_a, conv_state, rec0, conv_w, alog_row, dtb_row, ng_row)


def _rel_bias_tile(dist, bias_ref):
    nd = jnp.maximum(dist, 0)
    max_exact = N_BUCKETS // 2
    nf = jnp.maximum(nd, max_exact).astype(F32)
    large = max_exact + (jnp.log(nf / max_exact) / math.log(MAX_DIST / max_exact)
                         * (N_BUCKETS - max_exact)).astype(jnp.int32)
    large = jnp.minimum(large, N_BUCKETS - 1)
    bucket = jnp.where(nd < max_exact, nd, large)
    outs = []
    for h in range(H_B):
        val = jnp.full(dist.shape, bias_ref[0, h], F32)
        for bk in range(1, N_BUCKETS):
            val = jnp.where(bucket == bk, bias_ref[bk, h], val)
        outs.append(val)
    return outs


def _order_key(score, admissible):
    bits = pltpu.bitcast(score, jnp.int32)
    key = jnp.where(bits < 0, bits ^ jnp.int32(0x7FFFFFFF), bits)
    key = jnp.where(bits == jnp.int32(INT_MIN), 0, key)
    return jnp.where(admissible, key, jnp.int32(INT_MIN))


def _count(m, ka):
    x = m.astype(F32)
    step = LANES if ka == 1 else SUBLANES
    take = (lambda j: x[:, j:j + step]) if ka == 1 else (lambda j: x[j:j + step, :])
    parts = [take(j) for j in range(0, x.shape[ka], step)]
    while len(parts) > 1:
        odd = [parts[-1]] if len(parts) % 2 else []
        parts = [a + b for a, b in zip(parts[0::2], parts[1::2])] + odd
    return jnp.sum(parts[0], axis=ka, keepdims=True)


def _kth_largest(key_ref, top_k, ka):
    shape = (key_ref.shape[0], 1) if ka == 1 else (1, key_ref.shape[1])

    step = LANES if ka == 1 else SUBLANES
    slab_shape = (key_ref.shape[0], step) if ka == 1 else (step, key_ref.shape[1])

    def body(i, theta):
        inc = lax.shift_left(jnp.int32(1), jnp.int32(31) - i)
        cand = theta + inc
        cand_b = jnp.broadcast_to(cand, slab_shape)
        accs = [None] * COUNT_ACCS
        for n, j in enumerate(range(0, key_ref.shape[ka], step)):
            slab = key_ref[:, j:j + step] if ka == 1 else key_ref[j:j + step, :]
            part = (slab >= cand_b).astype(F32)
            accs[n % COUNT_ACCS] = part if accs[n % COUNT_ACCS] is None else accs[n % COUNT_ACCS] + part
        total = functools.reduce(lambda a, b: a + b, [a for a in accs if a is not None])
        cnt = jnp.sum(total, axis=ka, keepdims=True)
        return jnp.where(cnt >= top_k, cand, theta)

    return lax.fori_loop(0, 32, body, jnp.full(shape, INT_MIN, jnp.int32))


def _select_topk(key_ref, admissible, top_k, mask_ref, tri_ref, ka):
    theta = _kth_largest(key_ref, top_k, ka)
    key = key_ref[...]
    ge = key >= theta
    simple = jnp.logical_or(_count(ge, ka) == top_k, theta == jnp.int32(INT_MIN))
    n_bad = jnp.sum(jnp.where(simple, 0.0, 1.0))

    @pl.when(n_bad == 0.0)
    def _():
        mask_ref[...] = jnp.where(jnp.logical_and(ge, admissible), 0.0, NEG_INF)

    @pl.when(n_bad != 0.0)
    def _():
        gt = key > theta
        eq = key == theta
        need = top_k - _count(gt, ka)
        run = jnp.zeros(theta.shape, F32)
        for c0 in range(0, key.shape[ka], LANES):
            grp = (slice(None), slice(c0, c0 + LANES)) if ka == 1 else (slice(c0, c0 + LANES), slice(None))
            eq_c = eq[grp].astype(F32)
            if ka == 1:
                before = _dot(eq_c.astype(BF16), tri_ref[...]) + run
            else:
                before = _dot(tri_ref[...], eq_c.astype(BF16)) + run
            take = jnp.logical_and(eq[grp], before < need)
            m = jnp.logical_and(jnp.logical_or(gt[grp], take), admissible[grp])
            mask_ref[grp] = jnp.where(m, 0.0, NEG_INF)
            run = run + jnp.sum(eq_c, axis=ka, keepdims=True)


def _init_tri(tri_ref, ka):
    r = lax.broadcasted_iota(jnp.int32, (LANES, LANES), 0)
    c = lax.broadcasted_iota(jnp.int32, (LANES, LANES), 1)
    tri_ref[...] = ((r < c) if ka == 1 else (r > c)).astype(BF16)


def _attn_p_kernel(bias_ref, qb_ref, kb_ref, vb_ref, qidx_ref, miscq_ref, misck_ref, o_ref,
                   k_scr, vt_scr, kidx_scr, btab_scr, mask_scr, key_scr, tri_scr, *, tq, top_k, n_groups):
    b = pl.program_id(0)
    i = pl.program_id(1)
    nq = pl.num_programs(1)

    @pl.when(jnp.logical_and(b == 0, i == 0))
    def _():
        _init_tri(tri_scr, 0)
        kk = lax.broadcasted_iota(jnp.int32, (tq, tq), 0)
        qq = lax.broadcasted_iota(jnp.int32, (tq, tq), 1)
        def fill(dd, carry):
            tiles = _rel_bias_tile(dd * tq + qq - kk, bias_ref)
            r0 = pl.multiple_of((nq - 1 - dd) * tq, tq)
            for h in range(H_B):
                btab_scr[h, pl.ds(r0, tq), :] = tiles[h]
            return carry

        lax.fori_loop(0, nq, fill, 0)
        if nq > 1:
            for h in range(H_B):
                btab_scr[h, nq * tq:(2 * nq - 1) * tq, :] = jnp.zeros(((nq - 1) * tq, tq), F32)

    @pl.when(i == 0)
    def _():
        t_keys = k_scr.shape[0]
        for h in range(H_B):
            hd = slice(h * DH_B, (h + 1) * DH_B)
            k_scr[:, hd] = kb_ref[0, pl.ds(h, t_keys, stride=H_B), :].astype(BF16)
            vt_scr[hd, :] = vb_ref[0, pl.ds(h, t_keys, stride=H_B), :].T.astype(BF16)
        kidx_scr[...] = misck_ref[0, :, MISC_KIDX:MISC_KIDX + D_IDX].astype(BF16)

    def attend(n_keys):
        qidx = qidx_ref[0]
        misc_t = miscq_ref[0].T
        kidx = kidx_scr[0:n_keys, :]
        score = jnp.zeros((n_keys, tq), F32)
        for h in range(H_IDX):
            dots = _dot_nt(kidx, qidx[:, h * D_IDX:(h + 1) * D_IDX].astype(BF16))
            w = misc_t[MISC_WIDX + h:MISC_WIDX + h + 1, :] * (H_IDX ** -0.5) * (D_IDX ** -0.5)
            score = score + jnp.maximum(dots, 0.0) * w

        kpos = lax.broadcasted_iota(jnp.int32, (n_keys, tq), 0)
        qpos = i * tq + lax.broadcasted_iota(jnp.int32, (n_keys, tq), 1)
        admissible = kpos <= qpos
        key_ref = key_scr.at[0:n_keys, :]
        key_ref[...] = _order_key(score, admissible)
        mask_ref = mask_scr.at[0:n_keys, :]
        _select_topk(key_ref, admissible, top_k, mask_ref, tri_scr, 0)

        qb = qb_ref[0]
        boff = pl.multiple_of((nq - 1 - i) * tq, tq)
        for h in range(H_B):
            hd = slice(h * DH_B, (h + 1) * DH_B)
            logits = _dot_nt(k_scr[0:n_keys, hd], qb[:, hd].astype(BF16)) * (DH_B ** -0.5)
            logits = logits + btab_scr[h, pl.ds(boff, n_keys), :] + mask_ref[...]
            m = jnp.max(logits, axis=0, keepdims=True)
            p = jnp.exp(logits - m)
            denom = jnp.sum(p, axis=0, keepdims=True)
            out_t = _dot(vt_scr[hd, 0:n_keys], p.astype(BF16)) / denom
            o_ref[0, :, hd] = out_t.T

    per_group = nq // n_groups
    for grp in range(n_groups):
        pl.when(i // per_group == grp)(functools.partial(attend, (grp + 1) * per_group * tq))


def _attn_p(rel_bias, q_b, k_b, v_b, q_idx, misc, tq=128):
    b, t, _ = q_b.shape
    nq = t // tq
    top_k = min(TOPK_MAX, t // 4)
    n_groups = math.gcd(nq, 4)
    qblk = lambda w: pl.BlockSpec((1, tq, w), lambda bi, i: (bi, i, 0))
    full = lambda w: pl.BlockSpec((1, t, w), lambda bi, i: (bi, 0, 0))
    kv = pl.BlockSpec((1, t * H_B, DH_B), lambda bi, i: (bi, 0, 0))
    return pl.pallas_call(
        functools.partial(_attn_p_kernel, tq=tq, top_k=top_k, n_groups=n_groups),
        grid=(b, nq),
        in_specs=[pl.BlockSpec(memory_space=pltpu.SMEM),
                  qblk(HB), kv, kv, qblk(H_IDX * D_IDX), qblk(LANES), full(LANES)],
        out_specs=qblk(HB),
        out_shape=jax.ShapeDtypeStruct((b, t, HB), F32),
        scratch_shapes=[pltpu.VMEM((t, HB), BF16), pltpu.VMEM((HB, t), BF16),
                        pltpu.VMEM((t, D_IDX), BF16),
                        pltpu.VMEM((H_B, (2 * nq - 1) * tq, tq), F32),
                        pltpu.VMEM((t, tq), F32), pltpu.VMEM((t, tq), jnp.int32),
                        pltpu.VMEM((LANES, LANES), BF16)],
        compiler_params=_cparams(2),
        name="attn_p",
    )(rel_bias, q_b, k_b, v_b, q_idx, misc, misc)


def _attn_s_kernel(pt_ref, bias_ref, qb_ref, qidx_ref, widx_ref, knew_ref, vnew_ref, kidxnew_ref,
                   cidx_hbm, ck_hbm, cv_hbm, o_ref,
                   score_scr, logit_scr, mask_scr, key_scr, acc_scr, btab_scr, tri_scr, idx_buf, k_buf, v_buf, sems,
                   *, layer, t_new, n_pages, gp, top_k):
    b = pl.program_id(0)
    n_grp = n_pages // gp
    past = n_pages * PAGE_SIZE
    n_cols = past + PAGE_SIZE
    near0 = max(0, (past - MAX_DIST) // PAGE_SIZE) * PAGE_SIZE
    n_near = n_cols - near0
    pg_rows = PAGE_SIZE * H_B
    grp_keys = gp * PAGE_SIZE
    k_ahead = min(2, n_grp)

    @pl.when(b == 0)
    def _():
        _init_tri(tri_scr, 1)
        trow = lax.broadcasted_iota(jnp.int32, (t_new, n_near), 0)
        col = near0 + lax.broadcasted_iota(jnp.int32, (t_new, n_near), 1)
        tiles = _rel_bias_tile(past + trow - col, bias_ref)
        for h in range(H_B):
            btab_scr[h] = tiles[h]

    def group_copies(hbm, buf, sem_row, slot, grp, seq=None):
        out = []
        for m in range(gp):
            page = pt_ref[b if seq is None else seq, grp * gp + m]
            dst = buf.at[slot, m] if buf is idx_buf else buf.at[slot, pl.ds(m * pg_rows, pg_rows)]
            out.append(pltpu.make_async_copy(hbm.at[layer, page], dst, sems.at[sem_row, slot]))
        return out

    def start_idx_k(slot, grp, seq=None):
        for c in (group_copies(cidx_hbm, idx_buf, 0, slot, grp, seq)
                  + group_copies(ck_hbm, k_buf, 1, slot, grp, seq)):
            c.start()

    def wait_idx_k(slot, grp):
        for c in group_copies(cidx_hbm, idx_buf, 0, slot, grp) + group_copies(ck_hbm, k_buf, 1, slot, grp):
            c.wait()

    def start_v(slot, grp):
        for c in group_copies(cv_hbm, v_buf, 2, slot, grp):
            c.start()

    def wait_v(slot, grp):
        for c in group_copies(cv_hbm, v_buf, 2, slot, grp):
            c.wait()

    qb = qb_ref[0].astype(BF16)
    qidx = qidx_ref[0].astype(BF16)
    widx = widx_ref[0] * (H_IDX ** -0.5)
    head_rows = lambda h: slice(h * t_new, (h + 1) * t_new)
    head_lanes = lambda h: slice(h * DH_B, (h + 1) * DH_B)

    def scores_of(kidx_t):
        dots = _dot(qidx, kidx_t.astype(BF16)) * (D_IDX ** -0.5)
        contrib = jnp.maximum(dots, 0.0) * widx
        s = contrib[0:t_new]
        for h in range(1, H_IDX):
            s = s + contrib[h * t_new:(h + 1) * t_new]
        return s

    @pl.when(b == 0)
    def _():
        for g in range(k_ahead):
            start_idx_k(g, g)

    def pass1(grp, carry):
        slot = grp % 2

        @pl.when(jnp.logical_and(grp + 1 < n_grp, grp + 1 >= k_ahead))
        def _():
            start_idx_k(1 - slot, grp + 1)

        wait_idx_k(slot, grp)
        c0 = pl.multiple_of(grp * grp_keys, grp_keys)
        for m in range(gp):
            score_scr[:, pl.ds(c0 + m * PAGE_SIZE, PAGE_SIZE)] = scores_of(idx_buf[slot, m])
        for h in range(H_B):
            k_h = k_buf[slot, pl.ds(h, grp_keys, stride=H_B), :].astype(BF16)
            logit_scr[head_rows(h), pl.ds(c0, grp_keys)] = _dot_nt(qb[:, head_lanes(h)], k_h)
        return carry

    lax.fori_loop(0, n_grp, pass1, 0)
    for g in range(min(V_SLOTS - 1, n_grp)):
        start_v(g, g)

    score_scr[:, past:n_cols] = scores_of(kidxnew_ref[0])
    for h in range(H_B):
        k_h = knew_ref[0, pl.ds(h, PAGE_SIZE, stride=H_B), :].astype(BF16)
        logit_scr[head_rows(h), past:n_cols] = _dot_nt(qb[:, head_lanes(h)], k_h)

    trow = lax.broadcasted_iota(jnp.int32, (t_new, n_cols), 0)
    col = lax.broadcasted_iota(jnp.int32, (t_new, n_cols), 1)
    admissible = col <= past + trow
    key_scr[...] = _order_key(score_scr[...], admissible)
    _select_topk(key_scr, admissible, top_k, mask_scr, tri_scr, 1)
    for h in range(H_B):
        rs = head_rows(h)
        near = logit_scr[rs, near0:n_cols] * (DH_B ** -0.5) + btab_scr[h] + mask_scr[:, near0:n_cols]
        mx = jnp.max(near, axis=-1, keepdims=True)
        if near0 > 0:
            far = logit_scr[rs, 0:near0] * (DH_B ** -0.5) + bias_ref[N_BUCKETS - 1, h] + mask_scr[:, 0:near0]
            mx = jnp.maximum(mx, jnp.max(far, axis=-1, keepdims=True))
        pn = jnp.exp(near - mx)
        denom = jnp.sum(pn, axis=-1, keepdims=True)
        if near0 > 0:
            pf = jnp.exp(far - mx)
            denom = denom + jnp.sum(pf, axis=-1, keepdims=True)
            logit_scr[rs, 0:near0] = pf / denom
        logit_scr[rs, near0:n_cols] = pn / denom
        v_h = vnew_ref[0, pl.ds(h, PAGE_SIZE, stride=H_B), :].astype(BF16)
        acc_scr[rs, :] = _dot(logit_scr[rs, past:n_cols].astype(BF16), v_h)

    @pl.when(b + 1 < pl.num_programs(0))
    def _():
        for g in range(k_ahead):
            start_idx_k(g, g, b + 1)

    def pass2(grp, carry):
        slot = grp % V_SLOTS
        ahead = grp + V_SLOTS - 1

        @pl.when(ahead < n_grp)
        def _():
            start_v(ahead % V_SLOTS, ahead)

        wait_v(slot, grp)
        c0 = pl.multiple_of(grp * grp_keys, grp_keys)
        for h in range(H_B):
            v_h = v_buf[slot, pl.ds(h, grp_keys, stride=H_B), :].astype(BF16)
            p_h = logit_scr[head_rows(h), pl.ds(c0, grp_keys)].astype(BF16)
            acc_scr[head_rows(h), :] += _dot(p_h, v_h)
        return carry

    lax.fori_loop(0, n_grp, pass2, 0)
    for h in range(H_B):
        o_ref[0, :, head_lanes(h)] = acc_scr[head_rows(h), :]


def _attn_s(page_table, rel_bias, q_b, k_new, v_new, q_idx, kidx_new, misc, cache_k, cache_v, cache_idx_k,
            layer):
    b, t_new, _ = q_b.shape
    n_pages = page_table.shape[1]
    past = n_pages * PAGE_SIZE
    n_cols = past + PAGE_SIZE
    top_k = min(TOPK_MAX, (past + t_new) // 4)
    gp = math.gcd(n_pages, 16)
    hq = H_B * t_new
    near0 = max(0, (past - MAX_DIST) // PAGE_SIZE) * PAGE_SIZE

    qidx = q_idx.reshape(b, t_new, H_IDX, D_IDX).transpose(0, 2, 1, 3).reshape(b, H_IDX * t_new, D_IDX)
    widx = misc[:, :, MISC_WIDX:MISC_WIDX + H_IDX].transpose(0, 2, 1).reshape(b, H_IDX * t_new, 1)
    pad_page = lambda a, r: jnp.pad(a, ((0, 0), (0, (PAGE_SIZE - t_new) * r), (0, 0)))
    k_new, v_new = pad_page(k_new, H_B), pad_page(v_new, H_B)
    kidx_new_t = pad_page(kidx_new, 1).transpose(0, 2, 1)
    ck = cache_k.reshape(cache_k.shape[0], cache_k.shape[1], PAGE_SIZE * H_B, DH_B)
    cv = cache_v.reshape(cache_v.shape[0], cache_v.shape[1], PAGE_SIZE * H_B, DH_B)
    cidx_t = cache_idx_k.transpose(0, 1, 3, 2)

    per_b = lambda s: pl.BlockSpec((1,) + s, lambda bi, pt: (bi, 0, 0))
    hbm = pl.BlockSpec(memory_space=pl.ANY)
    in_specs = [pl.BlockSpec(memory_space=pltpu.SMEM),
                per_b((t_new, HB)), per_b((H_IDX * t_new, D_IDX)), per_b((H_IDX * t_new, 1)),
                per_b((PAGE_SIZE * H_B, DH_B)), per_b((PAGE_SIZE * H_B, DH_B)), per_b((D_IDX, PAGE_SIZE)),
                hbm, hbm, hbm]
    grid_spec = pltpu.PrefetchScalarGridSpec(
        num_scalar_prefetch=1,
        grid=(b,),
        in_specs=in_specs,
        out_specs=pl.BlockSpec((1, t_new, HB), lambda bi, pt: (bi, 0, 0)),
        scratch_shapes=[pltpu.VMEM((t_new, n_cols), F32), pltpu.VMEM((hq, n_cols), F32),
                        pltpu.VMEM((t_new, n_cols), F32), pltpu.VMEM((t_new, n_cols), jnp.int32),
                        pltpu.VMEM((hq, DH_B), F32),
                        pltpu.VMEM((H_B, t_new, n_cols - near0), F32),
                        pltpu.VMEM((LANES, LANES), BF16),
                        pltpu.VMEM((2, gp, D_IDX, PAGE_SIZE), F32),
                        pltpu.VMEM((2, gp * PAGE_SIZE * H_B, DH_B), F32),
                        pltpu.VMEM((V_SLOTS, gp * PAGE_SIZE * H_B, DH_B), F32),
                        pltpu.SemaphoreType.DMA((3, V_SLOTS))])
    return pl.pallas_call(
        functools.partial(_attn_s_kernel, layer=layer, t_new=t_new, n_pages=n_pages, gp=gp, top_k=top_k),
        grid_spec=grid_spec,
        out_shape=jax.ShapeDtypeStruct((b, t_new, HB), F32),
        compiler_params=_cparams(1),
        name="attn_s",
    )(page_table, rel_bias, q_b, qidx, widx, k_new, v_new, kidx_new_t, cidx_t, ck, cv)


def _ffn_kernel(x_ref, oa_ref, ob_ref, ga_ref, gb_ref, fst_ref, wpa_ref, wpb_ref, wout_ref, n2_ref,
                wup_ref, cfw_ref, cfb_ref, wdown_ref, fg_ref, y_ref, fnew_ref, prev_scr, act_scr,
                *, tm, ch, d_ff, seq):
    t = pl.program_id(1)
    nt = pl.num_programs(1)
    npr = CONV_F - 1
    if seq is None:
        @pl.when(t == 0)
        def _():
            prev_scr[SUBLANES - npr:SUBLANES, :] = fst_ref[0]

    pa = _dot(oa_ref[0].astype(BF16), wpa_ref[...])
    pb = _dot(ob_ref[0].astype(BF16), wpb_ref[...])
    merged = jax.nn.sigmoid(ga_ref[0]) * pa + jax.nn.sigmoid(gb_ref[0]) * pb
    x1 = x_ref[0] + _dot(merged.astype(BF16), wout_ref[...])
    xn2 = _rms(x1, n2_ref[...]).astype(BF16)

    def conv_cols(c0):
        cols = slice(c0, c0 + ch)
        up = _dot(xn2, wup_ref[:, cols])
        if seq is None:
            prev = prev_scr[:, cols]
            s2, s1 = _shift_rows(up, 2, prev), _shift_rows(up, 1, prev)
            prev_scr[SUBLANES - npr:SUBLANES, cols] = up[tm - npr:tm, :]
        else:
            hist = fst_ref[0, :, cols]
            tpos = lax.broadcasted_iota(jnp.int32, up.shape, 0) & (seq - 1)
            s2 = jnp.where(tpos < 2, hist, pltpu.roll(up, 2, 0))
            s1 = jnp.where(tpos < 1, pltpu.roll(hist, tm - 1, 0), pltpu.roll(up, 1, 0))
            fnew_ref[:, :, cols] = up.reshape(tm // seq, seq, ch)[:, seq - npr:seq, :]
        out = s2 * cfw_ref[0:1, cols]
        out = out + s1 * cfw_ref[1:2, cols]
        out = out + up * cfw_ref[2:3, cols]
        return out + cfb_ref[:, cols]

    for c0 in range(0, d_ff, ch):
        gate = conv_cols(c0)
        val = conv_cols(d_ff + c0)
        act_scr[:, c0:c0 + ch] = (_silu(gate) * val).astype(BF16)
    y_ref[0] = _rms(x1 + _dot(act_scr[...], wdown_ref[...]), fg_ref[...])

    if seq is None:
        @pl.when(t == nt - 1)
        def _():
            fnew_ref[0] = prev_scr[SUBLANES - npr:SUBLANES, :]


def _ffn(x, o_a, o_b, gate_a, gate_b, ffn_state, wpa, wpb, wout, n2_row, wup, cfw, cfb_row, wdown, fg_row, tm):
    b, t, d = x.shape
    d_ff = wdown.shape[0]
    ch = 256
    npr = CONV_F - 1
    consts = (wpa, wpb, wout, n2_row, wup, cfw, cfb_row, wdown, fg_row)
    if t == SUBLANES and t > npr:
        seq, rows = t, b * t
        flat = lambda a: a.reshape(1, rows, a.shape[-1])
        x, o_a, o_b, gate_a, gate_b = map(flat, (x, o_a, o_b, gate_a, gate_b))
        hist = flat(jnp.pad(ffn_state, ((0, 0), (0, t - npr), (0, 0))))
        grid, tm = (1, 1), rows
        blk = lambda w: pl.BlockSpec((1, rows, w), lambda i, j: (0, 0, 0))
        state_in, state_out = blk(2 * d_ff), pl.BlockSpec((b, npr, 2 * d_ff), lambda i, j: (0, 0, 0))
    else:
        seq, hist = None, ffn_state
        grid = (b, t // tm)
        blk = lambda w: pl.BlockSpec((1, tm, w), lambda i, j: (i, j, 0))
        state_in = state_out = pl.BlockSpec((1, npr, 2 * d_ff), lambda i, j: (i, 0, 0))
    y, ffn_new = pl.pallas_call(
        functools.partial(_ffn_kernel, tm=tm, ch=ch, d_ff=d_ff, seq=seq),
        grid=grid,
        in_specs=[blk(d), blk(o_a.shape[-1]), blk(o_b.shape[-1]), blk(d), blk(d), state_in]
                 + [_const_spec(c.shape) for c in consts],
        out_specs=[blk(d), state_out],
        out_shape=[jax.ShapeDtypeStruct(x.shape, F32),
                   jax.ShapeDtypeStruct((b, npr, 2 * d_ff), F32)],
        scratch_shapes=[pltpu.VMEM((SUBLANES, 2 * d_ff), F32), pltpu.VMEM((tm, d_ff), BF16)],
        compiler_params=_cparams(2),
        name="ffn",
    )(x, o_a, o_b, gate_a, gate_b, hist, *consts)
    return y.reshape(b, t, d), ffn_new


def _misc_row(vals, col0):
    return jnp.zeros((1, LANES), F32).at[0, col0:col0 + vals.shape[0]].set(vals.astype(F32))


def _layer(x, conv_state, rec_state, ffn_state, attend, lw, proj_tm, gdn_rows, ffn_tm):
    b, t, d = x.shape
    (w_in_bf, conv_a_w, alog_row, dtb_row, ng_row, wpa, wpb, wout, n1_row, n2_row, wup, cfw, cfb_row,
     wdown, fg_row) = lw
    (qkv_a, z_a, q_b, k_b, v_b, q_idx, gate_a, gate_b, misc, k_idx) = _proj(
        x.reshape(b * t, d), n1_row, w_in_bf, proj_tm)
    r3 = lambda a: a.reshape(b, -1, a.shape[-1])
    qkv_a, z_a, q_b, k_b, v_b, q_idx, gate_a, gate_b, misc, k_idx = map(
        r3, (qkv_a, z_a, q_b, k_b, v_b, q_idx, gate_a, gate_b, misc, k_idx))
    o_a, conv_new, rec_new = _gdn(qkv_a, misc, z_a, conv_state, rec_state, conv_a_w, alog_row, dtb_row,
                                  ng_row, gdn_rows)
    o_b = attend(q_b, k_b, v_b, q_idx, k_idx, misc)
    y, ffn_new = _ffn(x, o_a, o_b, gate_a, gate_b, ffn_state, wpa, wpb, wout, n2_row, wup, cfw, cfb_row,
                      wdown, fg_row, ffn_tm)
    kv_shape = (b, t, H_B, DH_B)
    return y, (k_b.reshape(kv_shape), v_b.reshape(kv_shape), k_idx, conv_new, rec_new, ffn_new)


def kernel(x_prompt, x_sample, cache_k, cache_v, cache_idx_k, state_a_conv, state_a_rec, state_ffn_conv,
           page_table, w_in, conv_a_w, a_log, dt_bias, norm_a_g, w_proj_a, w_proj_b, w_out, rel_bias,
           norm1_g, norm2_g, w_up, conv_f_w, conv_f_b, w_down, final_g):
    depth = w_in.shape[0]
    assert depth == 1, "the final norm is fused into the layer's last kernel"
    bp, tp, d = x_prompt.shape
    bs, ts, _ = x_sample.shape
    d_ff = w_down.shape[1]
    l = 0
    lw = (_prep_w_in(w_in[l], d), conv_a_w[l], _misc_row(a_log[l], MISC_A), _misc_row(dt_bias[l], MISC_A),
          norm_a_g[l].reshape(1, -1), w_proj_a[l].astype(BF16), w_proj_b[l].astype(BF16),
          w_out[l].astype(BF16), norm1_g[l].reshape(1, -1), norm2_g[l].reshape(1, -1),
          w_up[l].astype(BF16), conv_f_w[l], conv_f_b[l].reshape(1, -1), w_down[l].astype(BF16),
          final_g.reshape(1, -1))

    def attend_prompt(q_b, k_b, v_b, q_idx, k_idx, misc):
        return _attn_p(rel_bias, q_b, k_b, v_b, q_idx, misc)

    def attend_sample(q_b, k_b, v_b, q_idx, k_idx, misc):
        return _attn_s(page_table, rel_bias, q_b, k_b, v_b, q_idx, k_idx, misc, cache_k, cache_v,
                       cache_idx_k, l)

    zeros = lambda *s: jnp.zeros(s, x_prompt.dtype)
    y_p, st_p = _layer(x_prompt, zeros(bp, CONV_A - 1, QKV_A), zeros(bp, H_A, DK_A, DV_A),
                       zeros(bp, CONV_F - 1, 2 * d_ff), attend_prompt, lw,
                       proj_tm=256, gdn_rows=min(tp, 256), ffn_tm=min(tp, 256))
    y_s, st_s = _layer(x_sample, state_a_conv[l], state_a_rec[l], state_ffn_conv[l], attend_sample, lw,
                       proj_tm=min(bs * ts, 256), gdn_rows=ts, ffn_tm=ts)
    stack = lambda a: a[None]
    return (y_p, y_s) + tuple(map(stack, st_p)) + tuple(map(stack, st_s))
```

```python
import functools
import math

import jax
import jax.numpy as jnp
from jax import lax
from jax.experimental import pallas as pl
from jax.experimental.pallas import tpu as pltpu

F32 = jnp.float32
BF16 = jnp.bfloat16
HI = lax.Precision.HIGHEST

H_A, DK_A, DV_A, CONV_A, CHUNK_A = 4, 128, 128, 4, 64
H_B, DH_B, H_IDX, D_IDX = 4, 128, 8, 64
TOPK_MAX, N_BUCKETS, MAX_DIST = 256, 32, 2048
CONV_F, EPS, PAGE_SIZE = 3, 1e-6, 128
QKV_A = H_A * (2 * DK_A + DV_A)
HB = H_B * DH_B

LANES = 128
SUBLANES = 8
VMEM_LIMIT = 56 * 1024 * 1024
ROW_TILE = 256

MISC_KIDX, MISC_BETA, MISC_A, MISC_WIDX = 0, D_IDX, D_IDX + H_A, D_IDX + 2 * H_A

GDN_MM = "bf"

INT_MIN = -2 ** 31
COUNT_ACCS = 4
K_SLOTS = 3
V_SLOTS = 5
NEG_INF = float("-inf")


def _cparams(n_axes):
    return pltpu.CompilerParams(dimension_semantics=("arbitrary",) * n_axes,
                                vmem_limit_bytes=VMEM_LIMIT)


def _const_spec(shape):
    nd = len(shape)
    return pl.BlockSpec(shape, lambda *_: (0,) * nd, pipeline_mode=pl.Buffered(1))


def _silu(x):
    return x * jax.nn.sigmoid(x)


def _dot(a, b, precision=None):
    return jnp.dot(a, b, preferred_element_type=F32, precision=precision)


def _dot_nt(a, b, precision=None):
    return lax.dot_general(a, b, (((1,), (1,)), ((), ())), preferred_element_type=F32,
                           precision=precision)


_DIMS = {"nn": (((1,), (0,)), ((), ())), "nt": (((1,), (1,)), ((), ())), "tn": (((0,), (0,)), ((), ()))}


def _mm(a, b, mode, form="nn"):
    if mode == "bf":
        return lax.dot_general(a.astype(BF16), b.astype(BF16), _DIMS[form], preferred_element_type=F32)
    return lax.dot_general(a, b, _DIMS[form], preferred_element_type=F32, precision=HI)


def _cumsum_rows(x):
    row = lax.broadcasted_iota(jnp.int32, x.shape, 0)
    s = 1
    while s < x.shape[0]:
        x = x + jnp.where(row >= s, pltpu.roll(x, s, 0), 0.0)
        s *= 2
    return x


def _rms(x, g):
    return x * lax.rsqrt(jnp.mean(x * x, axis=-1, keepdims=True) + EPS) * g


def _shift_rows(x, k, prev):
    n = prev.shape[0]
    out = pltpu.roll(x, k, 0)
    head = out[0:SUBLANES]
    row = lax.broadcasted_iota(jnp.int32, head.shape, 0)
    for r in range(k):
        head = jnp.where(row == r, prev[n - k + r:n - k + r + 1, :], head)
    return head if x.shape[0] == SUBLANES else jnp.concatenate([head, out[SUBLANES:]], axis=0)


def _proj_kernel(x_ref, g_ref, w_ref, qkva_ref, za_ref, qb_ref, kb_ref, vb_ref, qidx_ref,
                 ga_ref, gb_ref, misc_ref, kidx_ref):
    xn = _rms(x_ref[...], g_ref[...]).astype(BF16)
    tm = x_ref.shape[0]
    off = 0
    for o_ref in (qkva_ref, za_ref, qb_ref, kb_ref, vb_ref, qidx_ref, ga_ref, gb_ref, misc_ref):
        if o_ref is kb_ref or o_ref is vb_ref:
            res = _dot(xn, w_ref[:, off:off + HB])
            for h in range(H_B):
                o_ref[pl.ds(h, tm, stride=H_B), :] = res[:, h * DH_B:(h + 1) * DH_B]
            off += HB
        else:
            wd = o_ref.shape[-1]
            o_ref[...] = _dot(xn, w_ref[:, off:off + wd])
            off += wd
    kidx_ref[...] = misc_ref[:, MISC_KIDX:MISC_KIDX + D_IDX]


def _prep_w_in(w_in_l, d_model):
    sizes = (QKV_A, H_A, H_A, H_A * DV_A, 3 * HB, H_IDX * D_IDX, D_IDX, H_IDX, d_model, d_model)
    parts, start = [], 0
    for s in sizes:
        parts.append(w_in_l[:, start:start + s])
        start += s
    qkv_a, beta, a_raw, z_a, qkv_b, q_idx, k_idx, w_idx, gate_a, gate_b = parts
    pad = jnp.zeros((w_in_l.shape[0], LANES - D_IDX - 2 * H_A - H_IDX), w_in_l.dtype)
    misc = jnp.concatenate([k_idx, beta, a_raw, w_idx, pad], axis=1)
    return jnp.concatenate([qkv_a, z_a, qkv_b, q_idx, gate_a, gate_b, misc], axis=1).astype(BF16)


def _proj(x2d, g_row, w_bf, tm):
    n, d = x2d.shape
    shapes = ((1, QKV_A), (1, H_A * DV_A), (1, HB), (H_B, DH_B), (H_B, DH_B), (1, H_IDX * D_IDX), (1, d), (1, d),
              (1, LANES), (1, D_IDX))
    row = lambda i: (i, 0)
    return pl.pallas_call(
        _proj_kernel,
        grid=(n // tm,),
        in_specs=[pl.BlockSpec((tm, d), row), _const_spec(g_row.shape), _const_spec(w_bf.shape)],
        out_specs=[pl.BlockSpec((tm * r, wd), row) for r, wd in shapes],
        out_shape=[jax.ShapeDtypeStruct((n * r, wd), F32) for r, wd in shapes],
        compiler_params=_cparams(1),
        name="proj",
    )(x2d, g_row, w_bf)


def _gdn_kernel(qkv_ref, misc_ref, z_ref, cst_ref, rec0_ref, cw_ref, alog_ref, dtb_ref, ng_ref,
                o_ref, cnew_ref, rnew_ref, s_scr, prev_scr, *, rows, chunk):
    t = pl.program_id(1)
    nt = pl.num_programs(1)
    npr = CONV_A - 1

    @pl.when(t == 0)
    def _():
        s_scr[...] = rec0_ref[0]
        prev_scr[SUBLANES - npr:SUBLANES, :] = cst_ref[0]

    x = qkv_ref[0]
    prev = prev_scr[...]
    conv = _shift_rows(x, 3, prev) * cw_ref[0:1, :]
    conv = conv + _shift_rows(x, 2, prev) * cw_ref[1:2, :]
    conv = conv + _shift_rows(x, 1, prev) * cw_ref[2:3, :]
    conv = conv + x * cw_ref[3:4, :]
    prev_scr[SUBLANES - npr:SUBLANES, :] = x[rows - npr:rows, :]

    @pl.when(t == nt - 1)
    def _():
        cnew_ref[0] = x[rows - npr:rows, :]

    act = _silu(conv)
    misc = misc_ref[0]
    beta_full = jax.nn.sigmoid(misc)
    sp = misc + dtb_ref[...]
    softplus = jnp.maximum(sp, 0.0) + jnp.log1p(jnp.exp(-jnp.abs(sp)))
    g_full = -jnp.exp(alog_ref[...]) * softplus

    ri = lax.broadcasted_iota(jnp.int32, (chunk, chunk), 0)
    ci = lax.broadcasted_iota(jnp.int32, (chunk, chunk), 1)
    tril = ri >= ci
    strict = ri > ci
    eye = (ri == ci).astype(F32)
    lane = lax.broadcasted_iota(jnp.int32, (chunk, LANES), 1)
    n_levels = int(math.log2(chunk))
    hk = H_A * DK_A
    chunks = list(range(0, rows, chunk))
    pairs = [(c0, h) for c0 in chunks for h in range(H_A)]

    gcum = {c0: _cumsum_rows(g_full[c0:c0 + chunk, :]) for c0 in chunks}
    gcum_t = {c0: gcum[c0].T for c0 in chunks} if chunk >= 64 else None

    pre = {}
    for c0, h in pairs:
        q = act[c0:c0 + chunk, h * DK_A:(h + 1) * DK_A]
        k = act[c0:c0 + chunk, hk + h * DK_A:hk + (h + 1) * DK_A]
        v = act[c0:c0 + chunk, 2 * hk + h * DV_A:2 * hk + (h + 1) * DV_A]
        q = q * lax.rsqrt(jnp.sum(q * q, axis=-1, keepdims=True) + EPS) * (DK_A ** -0.5)
        k = k * lax.rsqrt(jnp.sum(k * k, axis=-1, keepdims=True) + EPS)
        beta = beta_full[c0:c0 + chunk, MISC_BETA + h:MISC_BETA + h + 1]
        gc = gcum[c0][:, MISC_A + h:MISC_A + h + 1]
        if gcum_t is not None:
            g_row = gcum_t[c0][MISC_A + h:MISC_A + h + 1, :]
        else:
            g_row = _mm((lane == MISC_A + h).astype(F32), gcum[c0], "hi", "nt")
        gamma = jnp.where(tril, jnp.exp(jnp.where(tril, gc - g_row, 0.0)), 0.0)
        kb = k * beta
        egc = jnp.exp(gc)
        g_last = gc[chunk - 1:chunk, :]
        pre[c0, h] = dict(
            xp=-jnp.where(strict, _mm(kb, k, GDN_MM, "nt") * gamma, 0.0),
            rhs=jnp.concatenate([kb * egc, v * beta], axis=-1),
            attn=jnp.where(tril, _mm(q, k, GDN_MM, "nt") * gamma, 0.0),
            q_dec=q * egc, k_dec=k * jnp.exp(g_last - gc), e_last=jnp.exp(g_last))
    for p in pre.values():
        p["inv"] = eye + p["xp"]
    for _ in range(n_levels - 1):
        for p in pre.values():
            p["xp"] = _mm(p["xp"], p["xp"], GDN_MM)
        for p in pre.values():
            p["inv"] = p["inv"] + _mm(p["inv"], p["xp"], GDN_MM)
    for p in pre.values():
        p["sol"] = _mm(p["inv"], p["rhs"], GDN_MM)

    for c0, h in pairs:
        p = pre[c0, h]
        s = s_scr[h]
        v_new = p["sol"][:, DK_A:] - _mm(p["sol"][:, :DK_A], s, GDN_MM)
        o = _mm(p["q_dec"], s, GDN_MM) + _mm(p["attn"], v_new, GDN_MM)
        s_scr[h] = s * p["e_last"] + _mm(p["k_dec"], v_new, GDN_MM, "tn")
        z = z_ref[0, c0:c0 + chunk, h * DV_A:(h + 1) * DV_A]
        o_ref[0, c0:c0 + chunk, h * DV_A:(h + 1) * DV_A] = _rms(o, ng_ref[...]) * _silu(z)

    @pl.when(t == nt - 1)
    def _():
        rnew_ref[0] = s_scr[...]


def _gdn(qkv_a, misc, z_a, conv_state, rec0, conv_w, alog_row, dtb_row, ng_row, rows):
    b, t, _ = qkv_a.shape
    chunk = math.gcd(t, CHUNK_A)
    blk = lambda w: pl.BlockSpec((1, rows, w), lambda i, j: (i, j, 0))
    per_b3 = lambda s: pl.BlockSpec((1,) + s, lambda i, j: (i, 0, 0))
    per_b4 = lambda s: pl.BlockSpec((1,) + s, lambda i, j: (i, 0, 0, 0))
    return pl.pallas_call(
        functools.partial(_gdn_kernel, rows=rows, chunk=chunk),
        grid=(b, t // rows),
        in_specs=[blk(QKV_A), blk(LANES), blk(H_A * DV_A),
                  per_b3((CONV_A - 1, QKV_A)), per_b4((H_A, DK_A, DV_A)),
                  _const_spec(conv_w.shape), _const_spec(alog_row.shape),
                  _const_spec(dtb_row.shape), _const_spec(ng_row.shape)],
        out_specs=[blk(H_A * DV_A), per_b3((CONV_A - 1, QKV_A)), per_b4((H_A, DK_A, DV_A))],
        out_shape=[jax.ShapeDtypeStruct((b, t, H_A * DV_A), F32),
                   jax.ShapeDtypeStruct((b, CONV_A - 1, QKV_A), F32),
                   jax.ShapeDtypeStruct((b, H_A, DK_A, DV_A), F32)],
        scratch_shapes=[pltpu.VMEM((H_A, DK_A, DV_A), F32), pltpu.VMEM((SUBLANES, QKV_A), F32)],
        compiler_params=_cparams(2),
        name="gdn",
    )(qkv_a, misc, z_a, conv_state, rec0, conv_w, alog_row, dtb_row, ng_row)


def _rel_bias_tile(dist, bias_ref):
    nd = jnp.maximum(dist, 0)
    max_exact = N_BUCKETS // 2
    nf = jnp.maximum(nd, max_exact).astype(F32)
    large = max_exact + (jnp.log(nf / max_exact) / math.log(MAX_DIST / max_exact)
                         * (N_BUCKETS - max_exact)).astype(jnp.int32)
    large = jnp.minimum(large, N_BUCKETS - 1)
    bucket = jnp.where(nd < max_exact, nd, large)
    outs = []
    for h in range(H_B):
        val = jnp.full(dist.shape, bias_ref[0, h], F32)
        for bk in range(1, N_BUCKETS):
            val = jnp.where(bucket == bk, bias_ref[bk, h], val)
        outs.append(val)
    return outs


def _order_key(score, admissible):
    bits = pltpu.bitcast(score, jnp.int32)
    key = jnp.where(bits < 0, bits ^ jnp.int32(0x7FFFFFFF), bits)
    key = jnp.where(bits == jnp.int32(INT_MIN), 0, key)
    return jnp.where(admissible, key, jnp.int32(INT_MIN))


def _count(m, ka):
    x = m.astype(F32)
    step = LANES if ka == 1 else SUBLANES
    take = (lambda j: x[:, j:j + step]) if ka == 1 else (lambda j: x[j:j + step, :])
    parts = [take(j) for j in range(0, x.shape[ka], step)]
    while len(parts) > 1:
        odd = [parts[-1]] if len(parts) % 2 else []
        parts = [a + b for a, b in zip(parts[0::2], parts[1::2])] + odd
    return jnp.sum(parts[0], axis=ka, keepdims=True)


def _kth_largest(key_ref, top_k, ka):
    shape = (key_ref.shape[0], 1) if ka == 1 else (1, key_ref.shape[1])

    step = LANES if ka == 1 else SUBLANES
    slab_shape = (key_ref.shape[0], step) if ka == 1 else (step, key_ref.shape[1])

    def body(i, theta):
        inc = lax.shift_left(jnp.int32(1), jnp.int32(31) - i)
        cand = theta + inc
        cand_b = jnp.broadcast_to(cand, slab_shape)
        accs = [None] * COUNT_ACCS
        for n, j in enumerate(range(0, key_ref.shape[ka], step)):
            slab = key_ref[:, j:j + step] if ka == 1 else key_ref[j:j + step, :]
            part = (slab >= cand_b).astype(F32)
            accs[n % COUNT_ACCS] = part if accs[n % COUNT_ACCS] is None else accs[n % COUNT_ACCS] + part
        total = functools.reduce(lambda a, b: a + b, [a for a in accs if a is not None])
        cnt = jnp.sum(total, axis=ka, keepdims=True)
        return jnp.where(cnt >= top_k, cand, theta)

    return lax.fori_loop(0, 32, body, jnp.full(shape, INT_MIN, jnp.int32))


def _select_topk(key_ref, admissible, top_k, mask_ref, tri_ref, ka):
    theta = _kth_largest(key_ref, top_k, ka)
    key = key_ref[...]
    ge = key >= theta
    simple = jnp.logical_or(_count(ge, ka) == top_k, theta == jnp.int32(INT_MIN))
    n_bad = jnp.sum(jnp.where(simple, 0.0, 1.0))

    @pl.when(n_bad == 0.0)
    def _():
        mask_ref[...] = jnp.where(jnp.logical_and(ge, admissible), 0.0, NEG_INF)

    @pl.when(n_bad != 0.0)
    def _():
        gt = key > theta
        eq = key == theta
        need = top_k - _count(gt, ka)
        run = jnp.zeros(theta.shape, F32)
        for c0 in range(0, key.shape[ka], LANES):
            grp = (slice(None), slice(c0, c0 + LANES)) if ka == 1 else (slice(c0, c0 + LANES), slice(None))
            eq_c = eq[grp].astype(F32)
            if ka == 1:
                before = _dot(eq_c.astype(BF16), tri_ref[...]) + run
            else:
                before = _dot(tri_ref[...], eq_c.astype(BF16)) + run
            take = jnp.logical_and(eq[grp], before < need)
            m = jnp.logical_and(jnp.logical_or(gt[grp], take), admissible[grp])
            mask_ref[grp] = jnp.where(m, 0.0, NEG_INF)
            run = run + jnp.sum(eq_c, axis=ka, keepdims=True)


def _init_tri(tri_ref, ka):
    r = lax.broadcasted_iota(jnp.int32, (LANES, LANES), 0)
    c = lax.broadcasted_iota(jnp.int32, (LANES, LANES), 1)
    tri_ref[...] = ((r < c) if ka == 1 else (r > c)).astype(BF16)


def _attn_p_kernel(bias_ref, qb_ref, kb_ref, vb_ref, qidx_ref, miscq_ref, misck_ref, o_ref,
                   k_scr, vt_scr, kidx_scr, btab_scr, mask_scr, key_scr, tri_scr, *, tq, top_k, n_groups):
    b = pl.program_id(0)
    i = pl.program_id(1)
    nq = pl.num_programs(1)

    @pl.when(jnp.logical_and(b == 0, i == 0))
    def _():
        _init_tri(tri_scr, 0)
        kk = lax.broadcasted_iota(jnp.int32, (tq, tq), 0)
        qq = lax.broadcasted_iota(jnp.int32, (tq, tq), 1)
        def fill(dd, carry):
            tiles = _rel_bias_tile(dd * tq + qq - kk, bias_ref)
            r0 = pl.multiple_of((nq - 1 - dd) * tq, tq)
            for h in range(H_B):
                btab_scr[h, pl.ds(r0, tq), :] = tiles[h]
            return carry

        lax.fori_loop(0, nq, fill, 0)
        if nq > 1:
            for h in range(H_B):
                btab_scr[h, nq * tq:(2 * nq - 1) * tq, :] = jnp.zeros(((nq - 1) * tq, tq), F32)

    @pl.when(i == 0)
    def _():
        t_keys = k_scr.shape[0]
        for h in range(H_B):
            hd = slice(h * DH_B, (h + 1) * DH_B)
            k_scr[:, hd] = kb_ref[0, pl.ds(h, t_keys, stride=H_B), :].astype(BF16)
            vt_scr[hd, :] = vb_ref[0, pl.ds(h, t_keys, stride=H_B), :].T.astype(BF16)
        kidx_scr[...] = misck_ref[0, :, MISC_KIDX:MISC_KIDX + D_IDX].astype(BF16)

    def attend(n_keys):
        qidx = qidx_ref[0]
        misc_t = miscq_ref[0].T
        kidx = kidx_scr[0:n_keys, :]
        score = jnp.zeros((n_keys, tq), F32)
        for h in range(H_IDX):
            dots = _dot_nt(kidx, qidx[:, h * D_IDX:(h + 1) * D_IDX].astype(BF16))
            w = misc_t[MISC_WIDX + h:MISC_WIDX + h + 1, :] * (H_IDX ** -0.5) * (D_IDX ** -0.5)
            score = score + jnp.maximum(dots, 0.0) * w

        kpos = lax.broadcasted_iota(jnp.int32, (n_keys, tq), 0)
        qpos = i * tq + lax.broadcasted_iota(jnp.int32, (n_keys, tq), 1)
        admissible = kpos <= qpos
        key_ref = key_scr.at[0:n_keys, :]
        key_ref[...] = _order_key(score, admissible)
        mask_ref = mask_scr.at[0:n_keys, :]
        _select_topk(key_ref, admissible, top_k, mask_ref, tri_scr, 0)

        qb = qb_ref[0]
        boff = pl.multiple_of((nq - 1 - i) * tq, tq)
        for h in range(H_B):
            hd = slice(h * DH_B, (h + 1) * DH_B)
            logits = _dot_nt(k_scr[0:n_keys, hd], qb[:, hd].astype(BF16)) * (DH_B ** -0.5)
            logits = logits + btab_scr[h, pl.ds(boff, n_keys), :] + mask_ref[...]
            m = jnp.max(logits, axis=0, keepdims=True)
            p = jnp.exp(logits - m)
            denom = jnp.sum(p, axis=0, keepdims=True)
            out_t = _dot(vt_scr[hd, 0:n_keys], p.astype(BF16)) / denom
            o_ref[0, :, hd] = out_t.T

    per_group = nq // n_groups
    for grp in range(n_groups):
        pl.when(i // per_group == grp)(functools.partial(attend, (grp + 1) * per_group * tq))


def _attn_p(rel_bias, q_b, k_b, v_b, q_idx, misc, tq=128):
    b, t, _ = q_b.shape
    nq = t // tq
    top_k = min(TOPK_MAX, t // 4)
    n_groups = math.gcd(nq, 4)
    qblk = lambda w: pl.BlockSpec((1, tq, w), lambda bi, i: (bi, i, 0))
    full = lambda w: pl.BlockSpec((1, t, w), lambda bi, i: (bi, 0, 0))
    kv = pl.BlockSpec((1, t * H_B, DH_B), lambda bi, i: (bi, 0, 0))
    return pl.pallas_call(
        functools.partial(_attn_p_kernel, tq=tq, top_k=top_k, n_groups=n_groups),
        grid=(b, nq),
        in_specs=[pl.BlockSpec(memory_space=pltpu.SMEM),
                  qblk(HB), kv, kv, qblk(H_IDX * D_IDX), qblk(LANES), full(LANES)],
        out_specs=qblk(HB),
        out_shape=jax.ShapeDtypeStruct((b, t, HB), F32),
        scratch_shapes=[pltpu.VMEM((t, HB), BF16), pltpu.VMEM((HB, t), BF16),
                        pltpu.VMEM((t, D_IDX), BF16),
                        pltpu.VMEM((H_B, (2 * nq - 1) * tq, tq), F32),
                        pltpu.VMEM((t, tq), F32), pltpu.VMEM((t, tq), jnp.int32),
                        pltpu.VMEM((LANES, LANES), BF16)],
        compiler_params=_cparams(2),
        name="attn_p",
    )(rel_bias, q_b, k_b, v_b, q_idx, misc, misc)


def _attn_s_kernel(pt_ref, bias_ref, qb_ref, qidx_ref, widx_ref, knew_ref, vnew_ref, kidxnew_ref,
                   cidx_hbm, ck_hbm, cv_hbm, o_ref,
                   score_scr, logit_scr, mask_scr, key_scr, acc_scr, btab_scr, tri_scr, idx_buf, k_buf, v_buf, sems,
                   *, layer, t_new, n_pages, gp, top_k):
    b = pl.program_id(0)
    n_grp = n_pages // gp
    past = n_pages * PAGE_SIZE
    n_cols = past + PAGE_SIZE
    near0 = max(0, (past - MAX_DIST) // PAGE_SIZE) * PAGE_SIZE
    n_near = n_cols - near0
    pg_rows = PAGE_SIZE * H_B
    grp_keys = gp * PAGE_SIZE
    k_ahead = min(K_SLOTS - 1, n_grp)

    @pl.when(b == 0)
    def _():
        _init_tri(tri_scr, 1)
        trow = lax.broadcasted_iota(jnp.int32, (t_new, n_near), 0)
        col = near0 + lax.broadcasted_iota(jnp.int32, (t_new, n_near), 1)
        tiles = _rel_bias_tile(past + trow - col, bias_ref)
        for h in range(H_B):
            btab_scr[h] = tiles[h]

    def group_copies(hbm, buf, sem_row, slot, grp, seq=None):
        out = []
        for m in range(gp):
            page = pt_ref[b if seq is None else seq, grp * gp + m]
            dst = buf.at[slot, m] if buf is idx_buf else buf.at[slot, pl.ds(m * pg_rows, pg_rows)]
            out.append(pltpu.make_async_copy(hbm.at[layer, page], dst, sems.at[sem_row, slot]))
        return out

    def start_idx_k(slot, grp, seq=None):
        for c in (group_copies(cidx_hbm, idx_buf, 0, slot, grp, seq)
                  + group_copies(ck_hbm, k_buf, 1, slot, grp, seq)):
            c.start()

    def wait_idx_k(slot, grp):
        for c in group_copies(cidx_hbm, idx_buf, 0, slot, grp) + group_copies(ck_hbm, k_buf, 1, slot, grp):
            c.wait()

    def start_v(slot, grp):
        for c in group_copies(cv_hbm, v_buf, 2, slot, grp):
            c.start()

    def wait_v(slot, grp):
        for c in group_copies(cv_hbm, v_buf, 2, slot, grp):
            c.wait()

    qb = qb_ref[0].astype(BF16)
    qidx = qidx_ref[0].astype(BF16)
    widx = widx_ref[0] * (H_IDX ** -0.5)
    head_rows = lambda h: slice(h * t_new, (h + 1) * t_new)
    head_lanes = lambda h: slice(h * DH_B, (h + 1) * DH_B)

    def scores_of(kidx_t):
        dots = _dot(qidx, kidx_t.astype(BF16)) * (D_IDX ** -0.5)
        contrib = jnp.maximum(dots, 0.0) * widx
        s = contrib[0:t_new]
        for h in range(1, H_IDX):
            s = s + contrib[h * t_new:(h + 1) * t_new]
        return s

    @pl.when(b == 0)
    def _():
        for g in range(k_ahead):
            start_idx_k(g, g)

    def pass1(grp, carry):
        slot = grp % K_SLOTS
        ahead = grp + K_SLOTS - 1

        @pl.when(ahead < n_grp)
        def _():
            start_idx_k(ahead % K_SLOTS, ahead)

        wait_idx_k(slot, grp)
        c0 = pl.multiple_of(grp * grp_keys, grp_keys)
        for m in range(gp):
            score_scr[:, pl.ds(c0 + m * PAGE_SIZE, PAGE_SIZE)] = scores_of(idx_buf[slot, m])
        for h in range(H_B):
            k_h = k_buf[slot, pl.ds(h, grp_keys, stride=H_B), :].astype(BF16)
            logit_scr[head_rows(h), pl.ds(c0, grp_keys)] = _dot_nt(qb[:, head_lanes(h)], k_h)
        return carry

    lax.fori_loop(0, n_grp, pass1, 0)
    for g in range(min(V_SLOTS - 1, n_grp)):
        start_v(g, g)

    score_scr[:, past:n_cols] = scores_of(kidxnew_ref[0])
    for h in range(H_B):
        k_h = knew_ref[0, pl.ds(h, PAGE_SIZE, stride=H_B), :].astype(BF16)
        logit_scr[head_rows(h), past:n_cols] = _dot_nt(qb[:, head_lanes(h)], k_h)

    trow = lax.broadcasted_iota(jnp.int32, (t_new, n_cols), 0)
    col = lax.broadcasted_iota(jnp.int32, (t_new, n_cols), 1)
    admissible = col <= past + trow
    key_scr[...] = _order_key(score_scr[...], admissible)
    _select_topk(key_scr, admissible, top_k, mask_scr, tri_scr, 1)
    for h in range(H_B):
        rs = head_rows(h)
        near = logit_scr[rs, near0:n_cols] * (DH_B ** -0.5) + btab_scr[h] + mask_scr[:, near0:n_cols]
        mx = jnp.max(near, axis=-1, keepdims=True)
        if near0 > 0:
            far = logit_scr[rs, 0:near0] * (DH_B ** -0.5) + bias_ref[N_BUCKETS - 1, h] + mask_scr[:, 0:near0]
            mx = jnp.maximum(mx, jnp.max(far, axis=-1, keepdims=True))
        pn = jnp.exp(near - mx)
        denom = jnp.sum(pn, axis=-1, keepdims=True)
        if near0 > 0:
            pf = jnp.exp(far - mx)
            denom = denom + jnp.sum(pf, axis=-1, keepdims=True)
            logit_scr[rs, 0:near0] = pf / denom
        logit_scr[rs, near0:n_cols] = pn / denom
        v_h = vnew_ref[0, pl.ds(h, PAGE_SIZE, stride=H_B), :].astype(BF16)
        acc_scr[rs, :] = _dot(logit_scr[rs, past:n_cols].astype(BF16), v_h)

    @pl.when(b + 1 < pl.num_programs(0))
    def _():
        for g in range(k_ahead):
            start_idx_k(g, g, b + 1)

    def pass2(grp, carry):
        slot = grp % V_SLOTS
        ahead = grp + V_SLOTS - 1

        @pl.when(ahead < n_grp)
        def _():
            start_v(ahead % V_SLOTS, ahead)

        wait_v(slot, grp)
        c0 = pl.multiple_of(grp * grp_keys, grp_keys)
        for h in range(H_B):
            v_h = v_buf[slot, pl.ds(h, grp_keys, stride=H_B), :].astype(BF16)
            p_h = logit_scr[head_rows(h), pl.ds(c0, grp_keys)].astype(BF16)
            acc_scr[head_rows(h), :] += _dot(p_h, v_h)
        return carry

    lax.fori_loop(0, n_grp, pass2, 0)
    for h in range(H_B):
        o_ref[0, :, head_lanes(h)] = acc_scr[head_rows(h), :]


def _attn_s(page_table, rel_bias, q_b, k_new, v_new, q_idx, kidx_new, misc, cache_k, cache_v, cache_idx_k,
            layer):
    b, t_new, _ = q_b.shape
    n_pages = page_table.shape[1]
    past = n_pages * PAGE_SIZE
    n_cols = past + PAGE_SIZE
    top_k = min(TOPK_MAX, (past + t_new) // 4)
    gp = math.gcd(n_pages, 16)
    hq = H_B * t_new
    near0 = max(0, (past - MAX_DIST) // PAGE_SIZE) * PAGE_SIZE

    qidx = q_idx.reshape(b, t_new, H_IDX, D_IDX).transpose(0, 2, 1, 3).reshape(b, H_IDX * t_new, D_IDX)
    widx = misc[:, :, MISC_WIDX:MISC_WIDX + H_IDX].transpose(0, 2, 1).reshape(b, H_IDX * t_new, 1)
    pad_page = lambda a, r: jnp.pad(a, ((0, 0), (0, (PAGE_SIZE - t_new) * r), (0, 0)))
    k_new, v_new = pad_page(k_new, H_B), pad_page(v_new, H_B)
    kidx_new_t = pad_page(kidx_new, 1).transpose(0, 2, 1)
    ck = cache_k.reshape(cache_k.shape[0], cache_k.shape[1], PAGE_SIZE * H_B, DH_B)
    cv = cache_v.reshape(cache_v.shape[0], cache_v.shape[1], PAGE_SIZE * H_B, DH_B)
    cidx_t = cache_idx_k.transpose(0, 1, 3, 2)

    per_b = lambda s: pl.BlockSpec((1,) + s, lambda bi, pt: (bi, 0, 0))
    hbm = pl.BlockSpec(memory_space=pl.ANY)
    in_specs = [pl.BlockSpec(memory_space=pltpu.SMEM),
                per_b((t_new, HB)), per_b((H_IDX * t_new, D_IDX)), per_b((H_IDX * t_new, 1)),
                per_b((PAGE_SIZE * H_B, DH_B)), per_b((PAGE_SIZE * H_B, DH_B)), per_b((D_IDX, PAGE_SIZE)),
                hbm, hbm, hbm]
    grid_spec = pltpu.PrefetchScalarGridSpec(
        num_scalar_prefetch=1,
        grid=(b,),
        in_specs=in_specs,
        out_specs=pl.BlockSpec((1, t_new, HB), lambda bi, pt: (bi, 0, 0)),
        scratch_shapes=[pltpu.VMEM((t_new, n_cols), F32), pltpu.VMEM((hq, n_cols), F32),
                        pltpu.VMEM((t_new, n_cols), F32), pltpu.VMEM((t_new, n_cols), jnp.int32),
                        pltpu.VMEM((hq, DH_B), F32),
                        pltpu.VMEM((H_B, t_new, n_cols - near0), F32),
                        pltpu.VMEM((LANES, LANES), BF16),
                        pltpu.VMEM((K_SLOTS, gp, D_IDX, PAGE_SIZE), F32),
                        pltpu.VMEM((K_SLOTS, gp * PAGE_SIZE * H_B, DH_B), F32),
                        pltpu.VMEM((V_SLOTS, gp * PAGE_SIZE * H_B, DH_B), F32),
                        pltpu.SemaphoreType.DMA((3, max(K_SLOTS, V_SLOTS)))])
    return pl.pallas_call(
        functools.partial(_attn_s_kernel, layer=layer, t_new=t_new, n_pages=n_pages, gp=gp, top_k=top_k),
        grid_spec=grid_spec,
        out_shape=jax.ShapeDtypeStruct((b, t_new, HB), F32),
        compiler_params=_cparams(1),
        name="attn_s",
    )(page_table, rel_bias, q_b, qidx, widx, k_new, v_new, kidx_new_t, cidx_t, ck, cv)


def _ffn_kernel(x_ref, oa_ref, ob_ref, ga_ref, gb_ref, fst_ref, wpa_ref, wpb_ref, wout_ref, n2_ref,
                wup_ref, cfw_ref, cfb_ref, wdown_ref, fg_ref, y_ref, fnew_ref, prev_scr, act_scr,
                *, tm, ch, d_ff, seq):
    t = pl.program_id(1)
    nt = pl.num_programs(1)
    npr = CONV_F - 1
    if seq is None:
        @pl.when(t == 0)
        def _():
            prev_scr[SUBLANES - npr:SUBLANES, :] = fst_ref[0]

    pa = _dot(oa_ref[0].astype(BF16), wpa_ref[...])
    pb = _dot(ob_ref[0].astype(BF16), wpb_ref[...])
    merged = jax.nn.sigmoid(ga_ref[0]) * pa + jax.nn.sigmoid(gb_ref[0]) * pb
    x1 = x_ref[0] + _dot(merged.astype(BF16), wout_ref[...])
    xn2 = _rms(x1, n2_ref[...]).astype(BF16)

    def conv_cols(c0):
        cols = slice(c0, c0 + ch)
        up = _dot(xn2, wup_ref[:, cols])
        if seq is None:
            prev = prev_scr[:, cols]
            s2, s1 = _shift_rows(up, 2, prev), _shift_rows(up, 1, prev)
            prev_scr[SUBLANES - npr:SUBLANES, cols] = up[tm - npr:tm, :]
        else:
            hist = fst_ref[0, :, cols]
            tpos = lax.broadcasted_iota(jnp.int32, up.shape, 0) & (seq - 1)
            s2 = jnp.where(tpos < 2, hist, pltpu.roll(up, 2, 0))
            s1 = jnp.where(tpos < 1, pltpu.roll(hist, tm - 1, 0), pltpu.roll(up, 1, 0))
            fnew_ref[:, :, cols] = up.reshape(tm // seq, seq, ch)[:, seq - npr:seq, :]
        out = s2 * cfw_ref[0:1, cols]
        out = out + s1 * cfw_ref[1:2, cols]
        out = out + up * cfw_ref[2:3, cols]
        return out + cfb_ref[:, cols]

    for c0 in range(0, d_ff, ch):
        gate = conv_cols(c0)
        val = conv_cols(d_ff + c0)
        act_scr[:, c0:c0 + ch] = (_silu(gate) * val).astype(BF16)
    y_ref[0] = _rms(x1 + _dot(act_scr[...], wdown_ref[...]), fg_ref[...])

    if seq is None:
        @pl.when(t == nt - 1)
        def _():
            fnew_ref[0] = prev_scr[SUBLANES - npr:SUBLANES, :]


def _ffn(x, o_a, o_b, gate_a, gate_b, ffn_state, wpa, wpb, wout, n2_row, wup, cfw, cfb_row, wdown, fg_row, tm):
    b, t, d = x.shape
    d_ff = wdown.shape[0]
    ch = 256
    npr = CONV_F - 1
    consts = (wpa, wpb, wout, n2_row, wup, cfw, cfb_row, wdown, fg_row)
    if t == SUBLANES and t > npr:
        seq, rows = t, b * t
        flat = lambda a: a.reshape(1, rows, a.shape[-1])
        x, o_a, o_b, gate_a, gate_b = map(flat, (x, o_a, o_b, gate_a, gate_b))
        hist = flat(jnp.pad(ffn_state, ((0, 0), (0, t - npr), (0, 0))))
        grid, tm = (1, 1), rows
        blk = lambda w: pl.BlockSpec((1, rows, w), lambda i, j: (0, 0, 0))
        state_in, state_out = blk(2 * d_ff), pl.BlockSpec((b, npr, 2 * d_ff), lambda i, j: (0, 0, 0))
    else:
        seq, hist = None, ffn_state
        grid = (b, t // tm)
        blk = lambda w: pl.BlockSpec((1, tm, w), lambda i, j: (i, j, 0))
        state_in = state_out = pl.BlockSpec((1, npr, 2 * d_ff), lambda i, j: (i, 0, 0))
    y, ffn_new = pl.pallas_call(
        functools.partial(_ffn_kernel, tm=tm, ch=ch, d_ff=d_ff, seq=seq),
        grid=grid,
        in_specs=[blk(d), blk(o_a.shape[-1]), blk(o_b.shape[-1]), blk(d), blk(d), state_in]
                 + [_const_spec(c.shape) for c in consts],
        out_specs=[blk(d), state_out],
        out_shape=[jax.ShapeDtypeStruct(x.shape, F32),
                   jax.ShapeDtypeStruct((b, npr, 2 * d_ff), F32)],
        scratch_shapes=[pltpu.VMEM((SUBLANES, 2 * d_ff), F32), pltpu.VMEM((tm, d_ff), BF16)],
        compiler_params=_cparams(2),
        name="ffn",
    )(x, o_a, o_b, gate_a, gate_b, hist, *consts)
    return y.reshape(b, t, d), ffn_new


def _misc_row(vals, col0):
    return jnp.zeros((1, LANES), F32).at[0, col0:col0 + vals.shape[0]].set(vals.astype(F32))


def _row_tiles(b, t):
    return dict(proj_tm=min(b * t, ROW_TILE), gdn_rows=min(t, ROW_TILE), ffn_tm=min(t, ROW_TILE))


def _layer(x, conv_state, rec_state, ffn_state, attend, lw, proj_tm, gdn_rows, ffn_tm):
    b, t, d = x.shape
    (w_in_bf, conv_a_w, alog_row, dtb_row, ng_row, wpa, wpb, wout, n1_row, n2_row, wup, cfw, cfb_row,
     wdown, fg_row) = lw
    (qkv_a, z_a, q_b, k_b, v_b, q_idx, gate_a, gate_b, misc, k_idx) = _proj(
        x.reshape(b * t, d), n1_row, w_in_bf, proj_tm)
    r3 = lambda a: a.reshape(b, -1, a.shape[-1])
    qkv_a, z_a, q_b, k_b, v_b, q_idx, gate_a, gate_b, misc, k_idx = map(
        r3, (qkv_a, z_a, q_b, k_b, v_b, q_idx, gate_a, gate_b, misc, k_idx))
    o_a, conv_new, rec_new = _gdn(qkv_a, misc, z_a, conv_state, rec_state, conv_a_w, alog_row, dtb_row,
                                  ng_row, gdn_rows)
    o_b = attend(q_b, k_b, v_b, q_idx, k_idx, misc)
    y, ffn_new = _ffn(x, o_a, o_b, gate_a, gate_b, ffn_state, wpa, wpb, wout, n2_row, wup, cfw, cfb_row,
                      wdown, fg_row, ffn_tm)
    kv_shape = (b, t, H_B, DH_B)
    return y, (k_b.reshape(kv_shape), v_b.reshape(kv_shape), k_idx, conv_new, rec_new, ffn_new)


def kernel(x_prompt, x_sample, cache_k, cache_v, cache_idx_k, state_a_conv, state_a_rec, state_ffn_conv,
           page_table, w_in, conv_a_w, a_log, dt_bias, norm_a_g, w_proj_a, w_proj_b, w_out, rel_bias,
           norm1_g, norm2_g, w_up, conv_f_w, conv_f_b, w_down, final_g):
    depth = w_in.shape[0]
    assert depth == 1, "the final norm is fused into the layer's last kernel"
    bp, tp, d = x_prompt.shape
    bs, ts, _ = x_sample.shape
    d_ff = w_down.shape[1]
    l = 0
    lw = (_prep_w_in(w_in[l], d), conv_a_w[l], _misc_row(a_log[l], MISC_A), _misc_row(dt_bias[l], MISC_A),
          norm_a_g[l].reshape(1, -1), w_proj_a[l].astype(BF16), w_proj_b[l].astype(BF16),
          w_out[l].astype(BF16), norm1_g[l].reshape(1, -1), norm2_g[l].reshape(1, -1),
          w_up[l].astype(BF16), conv_f_w[l], conv_f_b[l].reshape(1, -1), w_down[l].astype(BF16),
          final_g.reshape(1, -1))

    def attend_prompt(q_b, k_b, v_b, q_idx, k_idx, misc):
        return _attn_p(rel_bias, q_b, k_b, v_b, q_idx, misc)

    def attend_sample(q_b, k_b, v_b, q_idx, k_idx, misc):
        return _attn_s(page_table, rel_bias, q_b, k_b, v_b, q_idx, k_idx, misc, cache_k, cache_v,
                       cache_idx_k, l)

    zeros = lambda *s: jnp.zeros(s, x_prompt.dtype)
    y_p, st_p = _layer(x_prompt, zeros(bp, CONV_A - 1, QKV_A), zeros(bp, H_A, DK_A, DV_A),
                       zeros(bp, CONV_F - 1, 2 * d_ff), attend_prompt, lw, **_row_tiles(bp, tp))
    y_s, st_s = _layer(x_sample, state_a_conv[l], state_a_rec[l], state_ffn_conv[l], attend_sample, lw,
                       **_row_tiles(bs, ts))
    stack = lambda a: a[None]
    return (y_p, y_s) + tuple(map(stack, st_p)) + tuple(map(stack, st_s))
```

```python
import functools
import math

import jax
import jax.numpy as jnp
from jax import lax
from jax.experimental import pallas as pl
from jax.experimental.pallas import tpu as pltpu

F32 = jnp.float32
BF16 = jnp.bfloat16
HI = lax.Precision.HIGHEST

H_A, DK_A, DV_A, CONV_A, CHUNK_A = 4, 128, 128, 4, 64
H_B, DH_B, H_IDX, D_IDX = 4, 128, 8, 64
TOPK_MAX, N_BUCKETS, MAX_DIST = 256, 32, 2048
CONV_F, EPS, PAGE_SIZE = 3, 1e-6, 128
QKV_A = H_A * (2 * DK_A + DV_A)
HB = H_B * DH_B

LANES = 128
SUBLANES = 8
VMEM_LIMIT = 56 * 1024 * 1024
ROW_TILE = 256

MISC_KIDX, MISC_BETA, MISC_A, MISC_WIDX = 0, D_IDX, D_IDX + H_A, D_IDX + 2 * H_A

GDN_MM = "bf"

INT_MIN = -2 ** 31
COUNT_ACCS = 4
K_SLOTS = 3
V_SLOTS = 5
NEG_INF = float("-inf")


def _cparams(n_axes):
    return pltpu.CompilerParams(dimension_semantics=("arbitrary",) * n_axes,
                                vmem_limit_bytes=VMEM_LIMIT)


def _const_spec(shape):
    nd = len(shape)
    return pl.BlockSpec(shape, lambda *_: (0,) * nd, pipeline_mode=pl.Buffered(1))


def _silu(x):
    return x * jax.nn.sigmoid(x)


def _dot(a, b, precision=None):
    return jnp.dot(a, b, preferred_element_type=F32, precision=precision)


def _dot_nt(a, b, precision=None):
    return lax.dot_general(a, b, (((1,), (1,)), ((), ())), preferred_element_type=F32,
                           precision=precision)


_DIMS = {"nn": (((1,), (0,)), ((), ())), "nt": (((1,), (1,)), ((), ())), "tn": (((0,), (0,)), ((), ()))}


def _mm(a, b, mode, form="nn"):
    if mode == "bf":
        return lax.dot_general(a.astype(BF16), b.astype(BF16), _DIMS[form], preferred_element_type=F32)
    return lax.dot_general(a, b, _DIMS[form], preferred_element_type=F32, precision=HI)


def _cumsum_rows(x):
    row = lax.broadcasted_iota(jnp.int32, x.shape, 0)
    s = 1
    while s < x.shape[0]:
        x = x + jnp.where(row >= s, pltpu.roll(x, s, 0), 0.0)
        s *= 2
    return x


def _rms(x, g):
    return x * lax.rsqrt(jnp.mean(x * x, axis=-1, keepdims=True) + EPS) * g


def _shift_rows(x, k, prev):
    n = prev.shape[0]
    out = pltpu.roll(x, k, 0)
    head = out[0:SUBLANES]
    row = lax.broadcasted_iota(jnp.int32, head.shape, 0)
    for r in range(k):
        head = jnp.where(row == r, prev[n - k + r:n - k + r + 1, :], head)
    return head if x.shape[0] == SUBLANES else jnp.concatenate([head, out[SUBLANES:]], axis=0)


def _proj_kernel(x_ref, g_ref, w_ref, qkva_ref, za_ref, qb_ref, kb_ref, vb_ref, qidx_ref,
                 ga_ref, gb_ref, misc_ref, kidx_ref):
    xn = _rms(x_ref[...], g_ref[...]).astype(BF16)
    tm = x_ref.shape[0]
    off = 0
    for o_ref in (qkva_ref, za_ref, qb_ref, kb_ref, vb_ref, qidx_ref, ga_ref, gb_ref, misc_ref):
        if o_ref is kb_ref or o_ref is vb_ref:
            res = _dot(xn, w_ref[:, off:off + HB])
            for h in range(H_B):
                o_ref[pl.ds(h, tm, stride=H_B), :] = res[:, h * DH_B:(h + 1) * DH_B]
            off += HB
        else:
            wd = o_ref.shape[-1]
            o_ref[...] = _dot(xn, w_ref[:, off:off + wd])
            off += wd
    kidx_ref[...] = misc_ref[:, MISC_KIDX:MISC_KIDX + D_IDX]


def _prep_w_in(w_in_l, d_model):
    sizes = (QKV_A, H_A, H_A, H_A * DV_A, 3 * HB, H_IDX * D_IDX, D_IDX, H_IDX, d_model, d_model)
    parts, start = [], 0
    for s in sizes:
        parts.append(w_in_l[:, start:start + s])
        start += s
    qkv_a, beta, a_raw, z_a, qkv_b, q_idx, k_idx, w_idx, gate_a, gate_b = parts
    pad = jnp.zeros((w_in_l.shape[0], LANES - D_IDX - 2 * H_A - H_IDX), w_in_l.dtype)
    misc = jnp.concatenate([k_idx, beta, a_raw, w_idx, pad], axis=1)
    return jnp.concatenate([qkv_a, z_a, qkv_b, q_idx, gate_a, gate_b, misc], axis=1).astype(BF16)


def _proj(x2d, g_row, w_bf, tm):
    n, d = x2d.shape
    shapes = ((1, QKV_A), (1, H_A * DV_A), (1, HB), (H_B, DH_B), (H_B, DH_B), (1, H_IDX * D_IDX), (1, d), (1, d),
              (1, LANES), (1, D_IDX))
    row = lambda i: (i, 0)
    return pl.pallas_call(
        _proj_kernel,
        grid=(n // tm,),
        in_specs=[pl.BlockSpec((tm, d), row), _const_spec(g_row.shape), _const_spec(w_bf.shape)],
        out_specs=[pl.BlockSpec((tm * r, wd), row) for r, wd in shapes],
        out_shape=[jax.ShapeDtypeStruct((n * r, wd), F32) for r, wd in shapes],
        compiler_params=_cparams(1),
        name="proj",
    )(x2d, g_row, w_bf)


def _gdn_kernel(qkv_ref, misc_ref, z_ref, cst_ref, rec0_ref, cw_ref, alog_ref, dtb_ref, ng_ref,
                o_ref, cnew_ref, rnew_ref, s_scr, prev_scr, *, rows, chunk):
    t = pl.program_id(1)
    nt = pl.num_programs(1)
    npr = CONV_A - 1

    @pl.when(t == 0)
    def _():
        s_scr[...] = rec0_ref[0]
        prev_scr[SUBLANES - npr:SUBLANES, :] = cst_ref[0]

    x = qkv_ref[0]
    prev = prev_scr[...]
    conv = _shift_rows(x, 3, prev) * cw_ref[0:1, :]
    conv = conv + _shift_rows(x, 2, prev) * cw_ref[1:2, :]
    conv = conv + _shift_rows(x, 1, prev) * cw_ref[2:3, :]
    conv = conv + x * cw_ref[3:4, :]
    prev_scr[SUBLANES - npr:SUBLANES, :] = x[rows - npr:rows, :]

    @pl.when(t == nt - 1)
    def _():
        cnew_ref[0] = x[rows - npr:rows, :]

    act = _silu(conv)
    misc = misc_ref[0]
    beta_full = jax.nn.sigmoid(misc)
    sp = misc + dtb_ref[...]
    softplus = jnp.maximum(sp, 0.0) + jnp.log1p(jnp.exp(-jnp.abs(sp)))
    g_full = -jnp.exp(alog_ref[...]) * softplus

    ri = lax.broadcasted_iota(jnp.int32, (chunk, chunk), 0)
    ci = lax.broadcasted_iota(jnp.int32, (chunk, chunk), 1)
    tril = ri >= ci
    strict = ri > ci
    eye = (ri == ci).astype(F32)
    lane = lax.broadcasted_iota(jnp.int32, (chunk, LANES), 1)
    n_levels = int(math.log2(chunk))
    hk = H_A * DK_A
    chunks = list(range(0, rows, chunk))
    pairs = [(c0, h) for c0 in chunks for h in range(H_A)]

    gcum = {c0: _cumsum_rows(g_full[c0:c0 + chunk, :]) for c0 in chunks}
    gcum_t = {c0: gcum[c0].T for c0 in chunks} if chunk >= 64 else None

    pre = {}
    for c0, h in pairs:
        q = act[c0:c0 + chunk, h * DK_A:(h + 1) * DK_A]
        k = act[c0:c0 + chunk, hk + h * DK_A:hk + (h + 1) * DK_A]
        v = act[c0:c0 + chunk, 2 * hk + h * DV_A:2 * hk + (h + 1) * DV_A]
        q = q * lax.rsqrt(jnp.sum(q * q, axis=-1, keepdims=True) + EPS) * (DK_A ** -0.5)
        k = k * lax.rsqrt(jnp.sum(k * k, axis=-1, keepdims=True) + EPS)
        beta = beta_full[c0:c0 + chunk, MISC_BETA + h:MISC_BETA + h + 1]
        gc = gcum[c0][:, MISC_A + h:MISC_A + h + 1]
        if gcum_t is not None:
            g_row = gcum_t[c0][MISC_A + h:MISC_A + h + 1, :]
        else:
            g_row = _mm((lane == MISC_A + h).astype(F32), gcum[c0], "hi", "nt")
        gamma = jnp.where(tril, jnp.exp(jnp.where(tril, gc - g_row, 0.0)), 0.0)
        kb = k * beta
        egc = jnp.exp(gc)
        g_last = gc[chunk - 1:chunk, :]
        pre[c0, h] = dict(
            xp=-jnp.where(strict, _mm(kb, k, GDN_MM, "nt") * gamma, 0.0),
            rhs=jnp.concatenate([kb * egc, v * beta], axis=-1),
            attn=jnp.where(tril, _mm(q, k, GDN_MM, "nt") * gamma, 0.0),
            q_dec=q * egc, k_dec=k * jnp.exp(g_last - gc), e_last=jnp.exp(g_last))
    for p in pre.values():
        p["inv"] = eye + p["xp"]
    for _ in range(n_levels - 1):
        for p in pre.values():
            p["xp"] = _mm(p["xp"], p["xp"], GDN_MM)
        for p in pre.values():
            p["inv"] = p["inv"] + _mm(p["inv"], p["xp"], GDN_MM)
    for p in pre.values():
        p["sol"] = _mm(p["inv"], p["rhs"], GDN_MM)

    for c0, h in pairs:
        p = pre[c0, h]
        s = s_scr[h]
        v_new = p["sol"][:, DK_A:] - _mm(p["sol"][:, :DK_A], s, GDN_MM)
        o = _mm(p["q_dec"], s, GDN_MM) + _mm(p["attn"], v_new, GDN_MM)
        s_scr[h] = s * p["e_last"] + _mm(p["k_dec"], v_new, GDN_MM, "tn")
        z = z_ref[0, c0:c0 + chunk, h * DV_A:(h + 1) * DV_A]
        o_ref[0, c0:c0 + chunk, h * DV_A:(h + 1) * DV_A] = _rms(o, ng_ref[...]) * _silu(z)

    @pl.when(t == nt - 1)
    def _():
        rnew_ref[0] = s_scr[...]


def _gdn(qkv_a, misc, z_a, conv_state, rec0, conv_w, alog_row, dtb_row, ng_row, rows):
    b, t, _ = qkv_a.shape
    chunk = math.gcd(t, CHUNK_A)
    blk = lambda w: pl.BlockSpec((1, rows, w), lambda i, j: (i, j, 0))
    per_b3 = lambda s: pl.BlockSpec((1,) + s, lambda i, j: (i, 0, 0))
    per_b4 = lambda s: pl.BlockSpec((1,) + s, lambda i, j: (i, 0, 0, 0))
    return pl.pallas_call(
        functools.partial(_gdn_kernel, rows=rows, chunk=chunk),
        grid=(b, t // rows),
        in_specs=[blk(QKV_A), blk(LANES), blk(H_A * DV_A),
                  per_b3((CONV_A - 1, QKV_A)), per_b4((H_A, DK_A, DV_A)),
                  _const_spec(conv_w.shape), _const_spec(alog_row.shape),
                  _const_spec(dtb_row.shape), _const_spec(ng_row.shape)],
        out_specs=[blk(H_A * DV_A), per_b3((CONV_A - 1, QKV_A)), per_b4((H_A, DK_A, DV_A))],
        out_shape=[jax.ShapeDtypeStruct((b, t, H_A * DV_A), F32),
                   jax.ShapeDtypeStruct((b, CONV_A - 1, QKV_A), F32),
                   jax.ShapeDtypeStruct((b, H_A, DK_A, DV_A), F32)],
        scratch_shapes=[pltpu.VMEM((H_A, DK_A, DV_A), F32), pltpu.VMEM((SUBLANES, QKV_A), F32)],
        compiler_params=_cparams(2),
        name="gdn",
    )(qkv_a, misc, z_a, conv_state, rec0, conv_w, alog_row, dtb_row, ng_row)


def _rel_bias_tile(dist, bias_ref):
    nd = jnp.maximum(dist, 0)
    max_exact = N_BUCKETS // 2
    nf = jnp.maximum(nd, max_exact).astype(F32)
    large = max_exact + (jnp.log(nf / max_exact) / math.log(MAX_DIST / max_exact)
                         * (N_BUCKETS - max_exact)).astype(jnp.int32)
    large = jnp.minimum(large, N_BUCKETS - 1)
    bucket = jnp.where(nd < max_exact, nd, large)
    outs = []
    for h in range(H_B):
        val = jnp.full(dist.shape, bias_ref[0, h], F32)
        for bk in range(1, N_BUCKETS):
            val = jnp.where(bucket == bk, bias_ref[bk, h], val)
        outs.append(val)
    return outs


def _order_key(score, admissible):
    bits = pltpu.bitcast(score, jnp.int32)
    key = jnp.where(bits < 0, bits ^ jnp.int32(0x7FFFFFFF), bits)
    key = jnp.where(bits == jnp.int32(INT_MIN), 0, key)
    return jnp.where(admissible, key, jnp.int32(INT_MIN))


def _count(m, ka):
    x = m.astype(F32)
    step = LANES if ka == 1 else SUBLANES
    take = (lambda j: x[:, j:j + step]) if ka == 1 else (lambda j: x[j:j + step, :])
    parts = [take(j) for j in range(0, x.shape[ka], step)]
    while len(parts) > 1:
        odd = [parts[-1]] if len(parts) % 2 else []
        parts = [a + b for a, b in zip(parts[0::2], parts[1::2])] + odd
    return jnp.sum(parts[0], axis=ka, keepdims=True)


def _kth_largest(key_ref, top_k, ka):
    shape = (key_ref.shape[0], 1) if ka == 1 else (1, key_ref.shape[1])

    step = LANES if ka == 1 else SUBLANES
    slab_shape = (key_ref.shape[0], step) if ka == 1 else (step, key_ref.shape[1])

    def body(i, theta):
        inc = lax.shift_left(jnp.int32(1), jnp.int32(31) - i)
        cand = theta + inc
        cand_b = jnp.broadcast_to(cand, slab_shape)
        accs = [None] * COUNT_ACCS
        for n, j in enumerate(range(0, key_ref.shape[ka], step)):
            slab = key_ref[:, j:j + step] if ka == 1 else key_ref[j:j + step, :]
            part = (slab >= cand_b).astype(F32)
            accs[n % COUNT_ACCS] = part if accs[n % COUNT_ACCS] is None else accs[n % COUNT_ACCS] + part
        total = functools.reduce(lambda a, b: a + b, [a for a in accs if a is not None])
        cnt = jnp.sum(total, axis=ka, keepdims=True)
        return jnp.where(cnt >= top_k, cand, theta)

    return lax.fori_loop(0, 32, body, jnp.full(shape, INT_MIN, jnp.int32))


def _kth_largest_halves(key_ref, half_ref, top_k):
    i16 = jnp.int16
    n_keys, n_q = key_ref.shape
    rows16 = 2 * SUBLANES

    def search(src_ref):
        def body(i, u):
            cand_u = u | lax.shift_left(jnp.int32(1), jnp.int32(15) - i)
            cand = jnp.broadcast_to((cand_u - 32768).astype(i16), (rows16, n_q))
            accs = [None] * COUNT_ACCS
            for n, j in enumerate(range(0, n_keys, rows16)):
                part = jnp.where(src_ref[j:j + rows16, :] >= cand, i16(1), i16(0))
                accs[n % COUNT_ACCS] = part if accs[n % COUNT_ACCS] is None else accs[n % COUNT_ACCS] + part
            total = functools.reduce(lambda a, b: a + b, [a for a in accs if a is not None])
            cnt = jnp.sum(total.astype(jnp.int32), axis=0, keepdims=True)
            return jnp.where(cnt >= top_k, cand_u, u)

        return lax.fori_loop(0, 16, body, jnp.zeros((1, n_q), jnp.int32))

    key = key_ref[...]
    hi = lax.shift_right_arithmetic(key, jnp.int32(16))
    half_ref[...] = hi.astype(i16)
    t_hi = search(half_ref) - 32768
    lo = (key & jnp.int32(0xFFFF)) - 32768
    half_ref[...] = jnp.where(hi == t_hi, lo, jnp.where(hi > t_hi, 32767, -32768)).astype(i16)
    t_lo = search(half_ref)
    return lax.shift_left(t_hi, jnp.int32(16)) | t_lo


def _select_topk(key_ref, admissible, top_k, mask_ref, tri_ref, ka, half_ref=None):
    if half_ref is None:
        theta = _kth_largest(key_ref, top_k, ka)
    else:
        theta = _kth_largest_halves(key_ref, half_ref, top_k)
    key = key_ref[...]
    ge = key >= theta
    simple = jnp.logical_or(_count(ge, ka) == top_k, theta == jnp.int32(INT_MIN))
    n_bad = jnp.sum(jnp.where(simple, 0.0, 1.0))

    @pl.when(n_bad == 0.0)
    def _():
        mask_ref[...] = jnp.where(jnp.logical_and(ge, admissible), 0.0, NEG_INF)

    @pl.when(n_bad != 0.0)
    def _():
        gt = key > theta
        eq = key == theta
        need = top_k - _count(gt, ka)
        run = jnp.zeros(theta.shape, F32)
        for c0 in range(0, key.shape[ka], LANES):
            grp = (slice(None), slice(c0, c0 + LANES)) if ka == 1 else (slice(c0, c0 + LANES), slice(None))
            eq_c = eq[grp].astype(F32)
            if ka == 1:
                before = _dot(eq_c.astype(BF16), tri_ref[...]) + run
            else:
                before = _dot(tri_ref[...], eq_c.astype(BF16)) + run
            take = jnp.logical_and(eq[grp], before < need)
            m = jnp.logical_and(jnp.logical_or(gt[grp], take), admissible[grp])
            mask_ref[grp] = jnp.where(m, 0.0, NEG_INF)
            run = run + jnp.sum(eq_c, axis=ka, keepdims=True)


def _init_tri(tri_ref, ka):
    r = lax.broadcasted_iota(jnp.int32, (LANES, LANES), 0)
    c = lax.broadcasted_iota(jnp.int32, (LANES, LANES), 1)
    tri_ref[...] = ((r < c) if ka == 1 else (r > c)).astype(BF16)


def _attn_p_kernel(bias_ref, qb_ref, kb_ref, vb_ref, qidx_ref, miscq_ref, misck_ref, o_ref,
                   k_scr, vt_scr, kidx_scr, btab_scr, mask_scr, key_scr, half_scr, tri_scr, *, tq, top_k, n_groups):
    b = pl.program_id(0)
    i = pl.program_id(1)
    nq = pl.num_programs(1)

    @pl.when(jnp.logical_and(b == 0, i == 0))
    def _():
        _init_tri(tri_scr, 0)
        kk = lax.broadcasted_iota(jnp.int32, (tq, tq), 0)
        qq = lax.broadcasted_iota(jnp.int32, (tq, tq), 1)
        def fill(dd, carry):
            tiles = _rel_bias_tile(dd * tq + qq - kk, bias_ref)
            r0 = pl.multiple_of((nq - 1 - dd) * tq, tq)
            for h in range(H_B):
                btab_scr[h, pl.ds(r0, tq), :] = tiles[h]
            return carry

        lax.fori_loop(0, nq, fill, 0)
        if nq > 1:
            for h in range(H_B):
                btab_scr[h, nq * tq:(2 * nq - 1) * tq, :] = jnp.zeros(((nq - 1) * tq, tq), F32)

    @pl.when(i == 0)
    def _():
        t_keys = k_scr.shape[0]
        for h in range(H_B):
            hd = slice(h * DH_B, (h + 1) * DH_B)
            k_scr[:, hd] = kb_ref[0, pl.ds(h, t_keys, stride=H_B), :].astype(BF16)
            vt_scr[hd, :] = vb_ref[0, pl.ds(h, t_keys, stride=H_B), :].T.astype(BF16)
        kidx_scr[...] = misck_ref[0, :, MISC_KIDX:MISC_KIDX + D_IDX].astype(BF16)

    def attend(n_keys):
        qidx = qidx_ref[0]
        misc_t = miscq_ref[0].T
        kidx = kidx_scr[0:n_keys, :]
        score = jnp.zeros((n_keys, tq), F32)
        for h in range(H_IDX):
            dots = _dot_nt(kidx, qidx[:, h * D_IDX:(h + 1) * D_IDX].astype(BF16))
            w = misc_t[MISC_WIDX + h:MISC_WIDX + h + 1, :] * (H_IDX ** -0.5) * (D_IDX ** -0.5)
            score = score + jnp.maximum(dots, 0.0) * w

        kpos = lax.broadcasted_iota(jnp.int32, (n_keys, tq), 0)
        qpos = i * tq + lax.broadcasted_iota(jnp.int32, (n_keys, tq), 1)
        admissible = kpos <= qpos
        key_ref = key_scr.at[0:n_keys, :]
        key_ref[...] = _order_key(score, admissible)
        mask_ref = mask_scr.at[0:n_keys, :]
        _select_topk(key_ref, admissible, top_k, mask_ref, tri_scr, 0, half_scr.at[0:n_keys, :])

        qb = qb_ref[0]
        boff = pl.multiple_of((nq - 1 - i) * tq, tq)
        for h in range(H_B):
            hd = slice(h * DH_B, (h + 1) * DH_B)
            logits = _dot_nt(k_scr[0:n_keys, hd], qb[:, hd].astype(BF16)) * (DH_B ** -0.5)
            logits = logits + btab_scr[h, pl.ds(boff, n_keys), :] + mask_ref[...]
            m = jnp.max(logits, axis=0, keepdims=True)
            p = jnp.exp(logits - m)
            denom = jnp.sum(p, axis=0, keepdims=True)
            out_t = _dot(vt_scr[hd, 0:n_keys], p.astype(BF16)) / denom
            o_ref[0, :, hd] = out_t.T

    per_group = nq // n_groups
    for grp in range(n_groups):
        pl.when(i // per_group == grp)(functools.partial(attend, (grp + 1) * per_group * tq))


def _attn_p(rel_bias, q_b, k_b, v_b, q_idx, misc, tq=128):
    b, t, _ = q_b.shape
    nq = t // tq
    top_k = min(TOPK_MAX, t // 4)
    n_groups = math.gcd(nq, 4)
    qblk = lambda w: pl.BlockSpec((1, tq, w), lambda bi, i: (bi, i, 0))
    full = lambda w: pl.BlockSpec((1, t, w), lambda bi, i: (bi, 0, 0))
    kv = pl.BlockSpec((1, t * H_B, DH_B), lambda bi, i: (bi, 0, 0))
    return pl.pallas_call(
        functools.partial(_attn_p_kernel, tq=tq, top_k=top_k, n_groups=n_groups),
        grid=(b, nq),
        in_specs=[pl.BlockSpec(memory_space=pltpu.SMEM),
                  qblk(HB), kv, kv, qblk(H_IDX * D_IDX), qblk(LANES), full(LANES)],
        out_specs=qblk(HB),
        out_shape=jax.ShapeDtypeStruct((b, t, HB), F32),
        scratch_shapes=[pltpu.VMEM((t, HB), BF16), pltpu.VMEM((HB, t), BF16),
                        pltpu.VMEM((t, D_IDX), BF16),
                        pltpu.VMEM((H_B, (2 * nq - 1) * tq, tq), F32),
                        pltpu.VMEM((t, tq), F32), pltpu.VMEM((t, tq), jnp.int32),
                        pltpu.VMEM((t, tq), jnp.int16), pltpu.VMEM((LANES, LANES), BF16)],
        compiler_params=_cparams(2),
        name="attn_p",
    )(rel_bias, q_b, k_b, v_b, q_idx, misc, misc)


def _attn_s_kernel(pt_ref, bias_ref, qb_ref, qidx_ref, widx_ref, knew_ref, vnew_ref, kidxnew_ref,
                   cidx_hbm, ck_hbm, cv_hbm, o_ref,
                   score_scr, logit_scr, mask_scr, key_scr, acc_scr, btab_scr, tri_scr, idx_buf, k_buf, v_buf, sems,
                   *, layer, t_new, n_pages, gp, top_k):
    b = pl.program_id(0)
    n_grp = n_pages // gp
    past = n_pages * PAGE_SIZE
    n_cols = past + PAGE_SIZE
    near0 = max(0, (past - MAX_DIST) // PAGE_SIZE) * PAGE_SIZE
    n_near = n_cols - near0
    pg_rows = PAGE_SIZE * H_B
    grp_keys = gp * PAGE_SIZE
    k_ahead = min(K_SLOTS - 1, n_grp)

    @pl.when(b == 0)
    def _():
        _init_tri(tri_scr, 1)
        trow = lax.broadcasted_iota(jnp.int32, (t_new, n_near), 0)
        col = near0 + lax.broadcasted_iota(jnp.int32, (t_new, n_near), 1)
        tiles = _rel_bias_tile(past + trow - col, bias_ref)
        for h in range(H_B):
            btab_scr[h] = tiles[h]

    def group_copies(hbm, buf, sem_row, slot, grp, seq=None):
        out = []
        for m in range(gp):
            page = pt_ref[b if seq is None else seq, grp * gp + m]
            dst = buf.at[slot, m] if buf is idx_buf else buf.at[slot, pl.ds(m * pg_rows, pg_rows)]
            out.append(pltpu.make_async_copy(hbm.at[layer, page], dst, sems.at[sem_row, slot]))
        return out

    def start_idx_k(slot, grp, seq=None):
        for c in (group_copies(cidx_hbm, idx_buf, 0, slot, grp, seq)
                  + group_copies(ck_hbm, k_buf, 1, slot, grp, seq)):
            c.start()

    def wait_idx_k(slot, grp):
        for c in group_copies(cidx_hbm, idx_buf, 0, slot, grp) + group_copies(ck_hbm, k_buf, 1, slot, grp):
            c.wait()

    def start_v(slot, grp):
        for c in group_copies(cv_hbm, v_buf, 2, slot, grp):
            c.start()

    def wait_v(slot, grp):
        for c in group_copies(cv_hbm, v_buf, 2, slot, grp):
            c.wait()

    qb = qb_ref[0].astype(BF16)
    qidx = qidx_ref[0].astype(BF16)
    widx = widx_ref[0] * (H_IDX ** -0.5)
    head_rows = lambda h: slice(h * t_new, (h + 1) * t_new)
    head_lanes = lambda h: slice(h * DH_B, (h + 1) * DH_B)

    def scores_of(kidx_t):
        dots = _dot(qidx, kidx_t.astype(BF16)) * (D_IDX ** -0.5)
        contrib = jnp.maximum(dots, 0.0) * widx
        s = contrib[0:t_new]
        for h in range(1, H_IDX):
            s = s + contrib[h * t_new:(h + 1) * t_new]
        return s

    @pl.when(b == 0)
    def _():
        for g in range(k_ahead):
            start_idx_k(g, g)

    def pass1(grp, carry):
        slot = grp % K_SLOTS
        ahead = grp + K_SLOTS - 1

        @pl.when(ahead < n_grp)
        def _():
            start_idx_k(ahead % K_SLOTS, ahead)

        wait_idx_k(slot, grp)
        c0 = pl.multiple_of(grp * grp_keys, grp_keys)
        for m in range(gp):
            score_scr[:, pl.ds(c0 + m * PAGE_SIZE, PAGE_SIZE)] = scores_of(idx_buf[slot, m])
        for h in range(H_B):
            k_h = k_buf[slot, pl.ds(h, grp_keys, stride=H_B), :].astype(BF16)
            logit_scr[head_rows(h), pl.ds(c0, grp_keys)] = _dot_nt(qb[:, head_lanes(h)], k_h)
        return carry

    lax.fori_loop(0, n_grp, pass1, 0)
    for g in range(min(V_SLOTS - 1, n_grp)):
        start_v(g, g)

    score_scr[:, past:n_cols] = scores_of(kidxnew_ref[0])
    for h in range(H_B):
        k_h = knew_ref[0, pl.ds(h, PAGE_SIZE, stride=H_B), :].astype(BF16)
        logit_scr[head_rows(h), past:n_cols] = _dot_nt(qb[:, head_lanes(h)], k_h)

    trow = lax.broadcasted_iota(jnp.int32, (t_new, n_cols), 0)
    col = lax.broadcasted_iota(jnp.int32, (t_new, n_cols), 1)
    admissible = col <= past + trow
    key_scr[...] = _order_key(score_scr[...], admissible)
    _select_topk(key_scr, admissible, top_k, mask_scr, tri_scr, 1)
    for h in range(H_B):
        rs = head_rows(h)
        near = logit_scr[rs, near0:n_cols] * (DH_B ** -0.5) + btab_scr[h] + mask_scr[:, near0:n_cols]
        mx = jnp.max(near, axis=-1, keepdims=True)
        if near0 > 0:
            far = logit_scr[rs, 0:near0] * (DH_B ** -0.5) + bias_ref[N_BUCKETS - 1, h] + mask_scr[:, 0:near0]
            mx = jnp.maximum(mx, jnp.max(far, axis=-1, keepdims=True))
        pn = jnp.exp(near - mx)
        denom = jnp.sum(pn, axis=-1, keepdims=True)
        if near0 > 0:
            pf = jnp.exp(far - mx)
            denom = denom + jnp.sum(pf, axis=-1, keepdims=True)
            logit_scr[rs, 0:near0] = pf / denom
        logit_scr[rs, near0:n_cols] = pn / denom
        v_h = vnew_ref[0, pl.ds(h, PAGE_SIZE, stride=H_B), :].astype(BF16)
        acc_scr[rs, :] = _dot(logit_scr[rs, past:n_cols].astype(BF16), v_h)

    @pl.when(b + 1 < pl.num_programs(0))
    def _():
        for g in range(k_ahead):
            start_idx_k(g, g, b + 1)

    def pass2(grp, carry):
        slot = grp % V_SLOTS
        ahead = grp + V_SLOTS - 1

        @pl.when(ahead < n_grp)
        def _():
            start_v(ahead % V_SLOTS, ahead)

        wait_v(slot, grp)
        c0 = pl.multiple_of(grp * grp_keys, grp_keys)
        for h in range(H_B):
            v_h = v_buf[slot, pl.ds(h, grp_keys, stride=H_B), :].astype(BF16)
            p_h = logit_scr[head_rows(h), pl.ds(c0, grp_keys)].astype(BF16)
            acc_scr[head_rows(h), :] += _dot(p_h, v_h)
        return carry

    lax.fori_loop(0, n_grp, pass2, 0)
    for h in range(H_B):
        o_ref[0, :, head_lanes(h)] = acc_scr[head_rows(h), :]


def _attn_s(page_table, rel_bias, q_b, k_new, v_new, q_idx, kidx_new, misc, cache_k, cache_v, cache_idx_k,
            layer):
    b, t_new, _ = q_b.shape
    n_pages = page_table.shape[1]
    past = n_pages * PAGE_SIZE
    n_cols = past + PAGE_SIZE
    top_k = min(TOPK_MAX, (past + t_new) // 4)
    gp = math.gcd(n_pages, 16)
    hq = H_B * t_new
    near0 = max(0, (past - MAX_DIST) // PAGE_SIZE) * PAGE_SIZE

    qidx = q_idx.reshape(b, t_new, H_IDX, D_IDX).transpose(0, 2, 1, 3).reshape(b, H_IDX * t_new, D_IDX)
    widx = misc[:, :, MISC_WIDX:MISC_WIDX + H_IDX].transpose(0, 2, 1).reshape(b, H_IDX * t_new, 1)
    pad_page = lambda a, r: jnp.pad(a, ((0, 0), (0, (PAGE_SIZE - t_new) * r), (0, 0)))
    k_new, v_new = pad_page(k_new, H_B), pad_page(v_new, H_B)
    kidx_new_t = pad_page(kidx_new, 1).transpose(0, 2, 1)
    ck = cache_k.reshape(cache_k.shape[0], cache_k.shape[1], PAGE_SIZE * H_B, DH_B)
    cv = cache_v.reshape(cache_v.shape[0], cache_v.shape[1], PAGE_SIZE * H_B, DH_B)
    cidx_t = cache_idx_k.transpose(0, 1, 3, 2)

    per_b = lambda s: pl.BlockSpec((1,) + s, lambda bi, pt: (bi, 0, 0))
    hbm = pl.BlockSpec(memory_space=pl.ANY)
    in_specs = [pl.BlockSpec(memory_space=pltpu.SMEM),
                per_b((t_new, HB)), per_b((H_IDX * t_new, D_IDX)), per_b((H_IDX * t_new, 1)),
                per_b((PAGE_SIZE * H_B, DH_B)), per_b((PAGE_SIZE * H_B, DH_B)), per_b((D_IDX, PAGE_SIZE)),
                hbm, hbm, hbm]
    grid_spec = pltpu.PrefetchScalarGridSpec(
        num_scalar_prefetch=1,
        grid=(b,),
        in_specs=in_specs,
        out_specs=pl.BlockSpec((1, t_new, HB), lambda bi, pt: (bi, 0, 0)),
        scratch_shapes=[pltpu.VMEM((t_new, n_cols), F32), pltpu.VMEM((hq, n_cols), F32),
                        pltpu.VMEM((t_new, n_cols), F32), pltpu.VMEM((t_new, n_cols), jnp.int32),
                        pltpu.VMEM((hq, DH_B), F32),
                        pltpu.VMEM((H_B, t_new, n_cols - near0), F32),
                        pltpu.VMEM((LANES, LANES), BF16),
                        pltpu.VMEM((K_SLOTS, gp, D_IDX, PAGE_SIZE), F32),
                        pltpu.VMEM((K_SLOTS, gp * PAGE_SIZE * H_B, DH_B), F32),
                        pltpu.VMEM((V_SLOTS, gp * PAGE_SIZE * H_B, DH_B), F32),
                        pltpu.SemaphoreType.DMA((3, max(K_SLOTS, V_SLOTS)))])
    return pl.pallas_call(
        functools.partial(_attn_s_kernel, layer=layer, t_new=t_new, n_pages=n_pages, gp=gp, top_k=top_k),
        grid_spec=grid_spec,
        out_shape=jax.ShapeDtypeStruct((b, t_new, HB), F32),
        compiler_params=_cparams(1),
        name="attn_s",
    )(page_table, rel_bias, q_b, qidx, widx, k_new, v_new, kidx_new_t, cidx_t, ck, cv)


def _ffn_kernel(x_ref, oa_ref, ob_ref, ga_ref, gb_ref, fst_ref, wpa_ref, wpb_ref, wout_ref, n2_ref,
                wup_ref, cfw_ref, cfb_ref, wdown_ref, fg_ref, y_ref, fnew_ref, prev_scr, act_scr,
                *, tm, ch, d_ff, seq):
    t = pl.program_id(1)
    nt = pl.num_programs(1)
    npr = CONV_F - 1
    if seq is None:
        @pl.when(t == 0)
        def _():
            prev_scr[SUBLANES - npr:SUBLANES, :] = fst_ref[0]

    pa = _dot(oa_ref[0].astype(BF16), wpa_ref[...])
    pb = _dot(ob_ref[0].astype(BF16), wpb_ref[...])
    merged = jax.nn.sigmoid(ga_ref[0]) * pa + jax.nn.sigmoid(gb_ref[0]) * pb
    x1 = x_ref[0] + _dot(merged.astype(BF16), wout_ref[...])
    xn2 = _rms(x1, n2_ref[...]).astype(BF16)

    def conv_cols(c0):
        cols = slice(c0, c0 + ch)
        up = _dot(xn2, wup_ref[:, cols])
        if seq is None:
            prev = prev_scr[:, cols]
            s2, s1 = _shift_rows(up, 2, prev), _shift_rows(up, 1, prev)
            prev_scr[SUBLANES - npr:SUBLANES, cols] = up[tm - npr:tm, :]
        else:
            hist = fst_ref[0, :, cols]
            tpos = lax.broadcasted_iota(jnp.int32, up.shape, 0) & (seq - 1)
            s2 = jnp.where(tpos < 2, hist, pltpu.roll(up, 2, 0))
            s1 = jnp.where(tpos < 1, pltpu.roll(hist, tm - 1, 0), pltpu.roll(up, 1, 0))
            fnew_ref[:, :, cols] = up.reshape(tm // seq, seq, ch)[:, seq - npr:seq, :]
        out = s2 * cfw_ref[0:1, cols]
        out = out + s1 * cfw_ref[1:2, cols]
        out = out + up * cfw_ref[2:3, cols]
        return out + cfb_ref[:, cols]

    for c0 in range(0, d_ff, ch):
        gate = conv_cols(c0)
        val = conv_cols(d_ff + c0)
        act_scr[:, c0:c0 + ch] = (_silu(gate) * val).astype(BF16)
    y_ref[0] = _rms(x1 + _dot(act_scr[...], wdown_ref[...]), fg_ref[...])

    if seq is None:
        @pl.when(t == nt - 1)
        def _():
            fnew_ref[0] = prev_scr[SUBLANES - npr:SUBLANES, :]


def _ffn(x, o_a, o_b, gate_a, gate_b, ffn_state, wpa, wpb, wout, n2_row, wup, cfw, cfb_row, wdown, fg_row, tm):
    b, t, d = x.shape
    d_ff = wdown.shape[0]
    ch = 256
    npr = CONV_F - 1
    consts = (wpa, wpb, wout, n2_row, wup, cfw, cfb_row, wdown, fg_row)
    if t == SUBLANES and t > npr:
        seq, rows = t, b * t
        flat = lambda a: a.reshape(1, rows, a.shape[-1])
        x, o_a, o_b, gate_a, gate_b = map(flat, (x, o_a, o_b, gate_a, gate_b))
        hist = flat(jnp.pad(ffn_state, ((0, 0), (0, t - npr), (0, 0))))
        grid, tm = (1, 1), rows
        blk = lambda w: pl.BlockSpec((1, rows, w), lambda i, j: (0, 0, 0))
        state_in, state_out = blk(2 * d_ff), pl.BlockSpec((b, npr, 2 * d_ff), lambda i, j: (0, 0, 0))
    else:
        seq, hist = None, ffn_state
        grid = (b, t // tm)
        blk = lambda w: pl.BlockSpec((1, tm, w), lambda i, j: (i, j, 0))
        state_in = state_out = pl.BlockSpec((1, npr, 2 * d_ff), lambda i, j: (i, 0, 0))
    y, ffn_new = pl.pallas_call(
        functools.partial(_ffn_kernel, tm=tm, ch=ch, d_ff=d_ff, seq=seq),
        grid=grid,
        in_specs=[blk(d), blk(o_a.shape[-1]), blk(o_b.shape[-1]), blk(d), blk(d), state_in]
                 + [_const_spec(c.shape) for c in consts],
        out_specs=[blk(d), state_out],
        out_shape=[jax.ShapeDtypeStruct(x.shape, F32),
                   jax.ShapeDtypeStruct((b, npr, 2 * d_ff), F32)],
        scratch_shapes=[pltpu.VMEM((SUBLANES, 2 * d_ff), F32), pltpu.VMEM((tm, d_ff), BF16)],
        compiler_params=_cparams(2),
        name="ffn",
    )(x, o_a, o_b, gate_a, gate_b, hist, *consts)
    return y.reshape(b, t, d), ffn_new


def _misc_row(vals, col0):
    return jnp.zeros((1, LANES), F32).at[0, col0:col0 + vals.shape[0]].set(vals.astype(F32))


def _row_tiles(b, t):
    return dict(proj_tm=min(b * t, ROW_TILE), gdn_rows=min(t, ROW_TILE), ffn_tm=min(t, ROW_TILE))


def _layer(x, conv_state, rec_state, ffn_state, attend, lw, proj_tm, gdn_rows, ffn_tm):
    b, t, d = x.shape
    (w_in_bf, conv_a_w, alog_row, dtb_row, ng_row, wpa, wpb, wout, n1_row, n2_row, wup, cfw, cfb_row,
     wdown, fg_row) = lw
    (qkv_a, z_a, q_b, k_b, v_b, q_idx, gate_a, gate_b, misc, k_idx) = _proj(
        x.reshape(b * t, d), n1_row, w_in_bf, proj_tm)
    r3 = lambda a: a.reshape(b, -1, a.shape[-1])
    qkv_a, z_a, q_b, k_b, v_b, q_idx, gate_a, gate_b, misc, k_idx = map(
        r3, (qkv_a, z_a, q_b, k_b, v_b, q_idx, gate_a, gate_b, misc, k_idx))
    o_a, conv_new, rec_new = _gdn(qkv_a, misc, z_a, conv_state, rec_state, conv_a_w, alog_row, dtb_row,
                                  ng_row, gdn_rows)
    o_b = attend(q_b, k_b, v_b, q_idx, k_idx, misc)
    y, ffn_new = _ffn(x, o_a, o_b, gate_a, gate_b, ffn_state, wpa, wpb, wout, n2_row, wup, cfw, cfb_row,
                      wdown, fg_row, ffn_tm)
    kv_shape = (b, t, H_B, DH_B)
    return y, (k_b.reshape(kv_shape), v_b.reshape(kv_shape), k_idx, conv_new, rec_new, ffn_new)


def kernel(x_prompt, x_sample, cache_k, cache_v, cache_idx_k, state_a_conv, state_a_rec, state_ffn_conv,
           page_table, w_in, conv_a_w, a_log, dt_bias, norm_a_g, w_proj_a, w_proj_b, w_out, rel_bias,
           norm1_g, norm2_g, w_up, conv_f_w, conv_f_b, w_down, final_g):
    depth = w_in.shape[0]
    assert depth == 1, "the final norm is fused into the layer's last kernel"
    bp, tp, d = x_prompt.shape
    bs, ts, _ = x_sample.shape
    d_ff = w_down.shape[1]
    l = 0
    lw = (_prep_w_in(w_in[l], d), conv_a_w[l], _misc_row(a_log[l], MISC_A), _misc_row(dt_bias[l], MISC_A),
          norm_a_g[l].reshape(1, -1), w_proj_a[l].astype(BF16), w_proj_b[l].astype(BF16),
          w_out[l].astype(BF16), norm1_g[l].reshape(1, -1), norm2_g[l].reshape(1, -1),
          w_up[l].astype(BF16), conv_f_w[l], conv_f_b[l].reshape(1, -1), w_down[l].astype(BF16),
          final_g.reshape(1, -1))

    def attend_prompt(q_b, k_b, v_b, q_idx, k_idx, misc):
        return _attn_p(rel_bias, q_b, k_b, v_b, q_idx, misc)

    def attend_sample(q_b, k_b, v_b, q_idx, k_idx, misc):
        return _attn_s(page_table, rel_bias, q_b, k_b, v_b, q_idx, k_idx, misc, cache_k, cache_v,
                       cache_idx_k, l)

    zeros = lambda *s: jnp.zeros(s, x_prompt.dtype)
    y_p, st_p = _layer(x_prompt, zeros(bp, CONV_A - 1, QKV_A), zeros(bp, H_A, DK_A, DV_A),
                       zeros(bp, CONV_F - 1, 2 * d_ff), attend_prompt, lw, **_row_tiles(bp, tp))
    y_s, st_s = _layer(x_sample, state_a_conv[l], state_a_rec[l], state_ffn_conv[l], attend_sample, lw,
                       **_row_tiles(bs, ts))
    stack = lambda a: a[None]
    return (y_p, y_s) + tuple(map(stack, st_p)) + tuple(map(stack, st_s))
```

```python
import functools
import math

import jax
import jax.numpy as jnp
from jax import lax
from jax.experimental import pallas as pl
from jax.experimental.pallas import tpu as pltpu

F32 = jnp.float32
BF16 = jnp.bfloat16
HI = lax.Precision.HIGHEST

H_A, DK_A, DV_A, CONV_A, CHUNK_A = 4, 128, 128, 4, 64
H_B, DH_B, H_IDX, D_IDX = 4, 128, 8, 64
TOPK_MAX, N_BUCKETS, MAX_DIST = 256, 32, 2048
CONV_F, EPS, PAGE_SIZE = 3, 1e-6, 128
QKV_A = H_A * (2 * DK_A + DV_A)
HB = H_B * DH_B

LANES = 128
SUBLANES = 8
VMEM_LIMIT = 56 * 1024 * 1024
ROW_TILE = 256

MISC_KIDX, MISC_BETA, MISC_A, MISC_WIDX = 0, D_IDX, D_IDX + H_A, D_IDX + 2 * H_A

GDN_MM = "bf"

INT_MIN = -2 ** 31
COUNT_ACCS = 4
K_SLOTS = 3
V_SLOTS = 5
NEG_INF = float("-inf")


def _cparams(n_axes):
    return pltpu.CompilerParams(dimension_semantics=("arbitrary",) * n_axes,
                                vmem_limit_bytes=VMEM_LIMIT)


def _const_spec(shape):
    nd = len(shape)
    return pl.BlockSpec(shape, lambda *_: (0,) * nd, pipeline_mode=pl.Buffered(1))


def _silu(x):
    return x * jax.nn.sigmoid(x)


def _dot(a, b, precision=None):
    return jnp.dot(a, b, preferred_element_type=F32, precision=precision)


def _dot_nt(a, b, precision=None):
    return lax.dot_general(a, b, (((1,), (1,)), ((), ())), preferred_element_type=F32,
                           precision=precision)


_DIMS = {"nn": (((1,), (0,)), ((), ())), "nt": (((1,), (1,)), ((), ())), "tn": (((0,), (0,)), ((), ()))}


def _mm(a, b, mode, form="nn"):
    if mode == "bf":
        return lax.dot_general(a.astype(BF16), b.astype(BF16), _DIMS[form], preferred_element_type=F32)
    return lax.dot_general(a, b, _DIMS[form], preferred_element_type=F32, precision=HI)


def _cumsum_rows(x):
    row = lax.broadcasted_iota(jnp.int32, x.shape, 0)
    s = 1
    while s < x.shape[0]:
        x = x + jnp.where(row >= s, pltpu.roll(x, s, 0), 0.0)
        s *= 2
    return x


def _rms(x, g):
    return x * lax.rsqrt(jnp.mean(x * x, axis=-1, keepdims=True) + EPS) * g


def _shift_rows(x, k, prev):
    n = prev.shape[0]
    out = pltpu.roll(x, k, 0)
    head = out[0:SUBLANES]
    row = lax.broadcasted_iota(jnp.int32, head.shape, 0)
    for r in range(k):
        head = jnp.where(row == r, prev[n - k + r:n - k + r + 1, :], head)
    return head if x.shape[0] == SUBLANES else jnp.concatenate([head, out[SUBLANES:]], axis=0)


def _proj_kernel(x_ref, g_ref, w_ref, qkva_ref, za_ref, qb_ref, kb_ref, vb_ref, qidx_ref,
                 ga_ref, gb_ref, misc_ref, kidx_ref):
    xn = _rms(x_ref[...], g_ref[...]).astype(BF16)
    tm = x_ref.shape[0]
    off = 0
    for o_ref in (qkva_ref, za_ref, qb_ref, kb_ref, vb_ref, qidx_ref, ga_ref, gb_ref, misc_ref):
        if o_ref is kb_ref or o_ref is vb_ref:
            res = _dot(xn, w_ref[:, off:off + HB])
            for h in range(H_B):
                o_ref[pl.ds(h, tm, stride=H_B), :] = res[:, h * DH_B:(h + 1) * DH_B]
            off += HB
        else:
            wd = o_ref.shape[-1]
            o_ref[...] = _dot(xn, w_ref[:, off:off + wd])
            off += wd
    kidx_ref[...] = misc_ref[:, MISC_KIDX:MISC_KIDX + D_IDX]


def _prep_w_in(w_in_l, d_model):
    sizes = (QKV_A, H_A, H_A, H_A * DV_A, 3 * HB, H_IDX * D_IDX, D_IDX, H_IDX, d_model, d_model)
    parts, start = [], 0
    for s in sizes:
        parts.append(w_in_l[:, start:start + s])
        start += s
    qkv_a, beta, a_raw, z_a, qkv_b, q_idx, k_idx, w_idx, gate_a, gate_b = parts
    pad = jnp.zeros((w_in_l.shape[0], LANES - D_IDX - 2 * H_A - H_IDX), w_in_l.dtype)
    misc = jnp.concatenate([k_idx, beta, a_raw, w_idx, pad], axis=1)
    return jnp.concatenate([qkv_a, z_a, qkv_b, q_idx, gate_a, gate_b, misc], axis=1).astype(BF16)


def _proj(x2d, g_row, w_bf, tm):
    n, d = x2d.shape
    shapes = ((1, QKV_A), (1, H_A * DV_A), (1, HB), (H_B, DH_B), (H_B, DH_B), (1, H_IDX * D_IDX), (1, d), (1, d),
              (1, LANES), (1, D_IDX))
    row = lambda i: (i, 0)
    return pl.pallas_call(
        _proj_kernel,
        grid=(n // tm,),
        in_specs=[pl.BlockSpec((tm, d), row), _const_spec(g_row.shape), _const_spec(w_bf.shape)],
        out_specs=[pl.BlockSpec((tm * r, wd), row) for r, wd in shapes],
        out_shape=[jax.ShapeDtypeStruct((n * r, wd), F32) for r, wd in shapes],
        compiler_params=_cparams(1),
        name="proj",
    )(x2d, g_row, w_bf)


def _gdn_kernel(qkv_ref, misc_ref, z_ref, cst_ref, rec0_ref, cw_ref, alog_ref, dtb_ref, ng_ref,
                o_ref, cnew_ref, rnew_ref, s_scr, prev_scr, *, rows, chunk):
    t = pl.program_id(1)
    nt = pl.num_programs(1)
    npr = CONV_A - 1

    @pl.when(t == 0)
    def _():
        s_scr[...] = rec0_ref[0]
        prev_scr[SUBLANES - npr:SUBLANES, :] = cst_ref[0]

    x = qkv_ref[0]
    prev = prev_scr[...]
    conv = _shift_rows(x, 3, prev) * cw_ref[0:1, :]
    conv = conv + _shift_rows(x, 2, prev) * cw_ref[1:2, :]
    conv = conv + _shift_rows(x, 1, prev) * cw_ref[2:3, :]
    conv = conv + x * cw_ref[3:4, :]
    prev_scr[SUBLANES - npr:SUBLANES, :] = x[rows - npr:rows, :]

    @pl.when(t == nt - 1)
    def _():
        cnew_ref[0] = x[rows - npr:rows, :]

    act = _silu(conv)
    misc = misc_ref[0]
    beta_full = jax.nn.sigmoid(misc)
    sp = misc + dtb_ref[...]
    softplus = jnp.maximum(sp, 0.0) + jnp.log1p(jnp.exp(-jnp.abs(sp)))
    g_full = -jnp.exp(alog_ref[...]) * softplus

    ri = lax.broadcasted_iota(jnp.int32, (chunk, chunk), 0)
    ci = lax.broadcasted_iota(jnp.int32, (chunk, chunk), 1)
    tril = ri >= ci
    strict = ri > ci
    eye = (ri == ci).astype(F32)
    lane = lax.broadcasted_iota(jnp.int32, (chunk, LANES), 1)
    n_levels = int(math.log2(chunk))
    hk = H_A * DK_A
    chunks = list(range(0, rows, chunk))
    pairs = [(c0, h) for c0 in chunks for h in range(H_A)]

    gcum = {c0: _cumsum_rows(g_full[c0:c0 + chunk, :]) for c0 in chunks}
    gcum_t = {c0: gcum[c0].T for c0 in chunks} if chunk >= 64 else None

    pre = {}
    for c0, h in pairs:
        q = act[c0:c0 + chunk, h * DK_A:(h + 1) * DK_A]
        k = act[c0:c0 + chunk, hk + h * DK_A:hk + (h + 1) * DK_A]
        v = act[c0:c0 + chunk, 2 * hk + h * DV_A:2 * hk + (h + 1) * DV_A]
        q = q * lax.rsqrt(jnp.sum(q * q, axis=-1, keepdims=True) + EPS) * (DK_A ** -0.5)
        k = k * lax.rsqrt(jnp.sum(k * k, axis=-1, keepdims=True) + EPS)
        beta = beta_full[c0:c0 + chunk, MISC_BETA + h:MISC_BETA + h + 1]
        gc = gcum[c0][:, MISC_A + h:MISC_A + h + 1]
        if gcum_t is not None:
            g_row = gcum_t[c0][MISC_A + h:MISC_A + h + 1, :]
        else:
            g_row = _mm((lane == MISC_A + h).astype(F32), gcum[c0], "hi", "nt")
        gamma = jnp.where(tril, jnp.exp(jnp.where(tril, gc - g_row, 0.0)), 0.0)
        kb = k * beta
        egc = jnp.exp(gc)
        g_last = gc[chunk - 1:chunk, :]
        pre[c0, h] = dict(
            xp=-jnp.where(strict, _mm(kb, k, GDN_MM, "nt") * gamma, 0.0),
            rhs=jnp.concatenate([kb * egc, v * beta], axis=-1),
            attn=jnp.where(tril, _mm(q, k, GDN_MM, "nt") * gamma, 0.0),
            q_dec=q * egc, k_dec=k * jnp.exp(g_last - gc), e_last=jnp.exp(g_last))
    for p in pre.values():
        p["inv"] = eye + p["xp"]
    for _ in range(n_levels - 1):
        for p in pre.values():
            p["xp"] = _mm(p["xp"], p["xp"], GDN_MM)
        for p in pre.values():
            p["inv"] = p["inv"] + _mm(p["inv"], p["xp"], GDN_MM)
    for p in pre.values():
        p["sol"] = _mm(p["inv"], p["rhs"], GDN_MM)

    for c0, h in pairs:
        p = pre[c0, h]
        s = s_scr[h]
        v_new = p["sol"][:, DK_A:] - _mm(p["sol"][:, :DK_A], s, GDN_MM)
        o = _mm(p["q_dec"], s, GDN_MM) + _mm(p["attn"], v_new, GDN_MM)
        s_scr[h] = s * p["e_last"] + _mm(p["k_dec"], v_new, GDN_MM, "tn")
        z = z_ref[0, c0:c0 + chunk, h * DV_A:(h + 1) * DV_A]
        o_ref[0, c0:c0 + chunk, h * DV_A:(h + 1) * DV_A] = _rms(o, ng_ref[...]) * _silu(z)

    @pl.when(t == nt - 1)
    def _():
        rnew_ref[0] = s_scr[...]


def _gdn(qkv_a, misc, z_a, conv_state, rec0, conv_w, alog_row, dtb_row, ng_row, rows):
    b, t, _ = qkv_a.shape
    chunk = math.gcd(t, CHUNK_A)
    blk = lambda w: pl.BlockSpec((1, rows, w), lambda i, j: (i, j, 0))
    per_b3 = lambda s: pl.BlockSpec((1,) + s, lambda i, j: (i, 0, 0))
    per_b4 = lambda s: pl.BlockSpec((1,) + s, lambda i, j: (i, 0, 0, 0))
    return pl.pallas_call(
        functools.partial(_gdn_kernel, rows=rows, chunk=chunk),
        grid=(b, t // rows),
        in_specs=[blk(QKV_A), blk(LANES), blk(H_A * DV_A),
                  per_b3((CONV_A - 1, QKV_A)), per_b4((H_A, DK_A, DV_A)),
                  _const_spec(conv_w.shape), _const_spec(alog_row.shape),
                  _const_spec(dtb_row.shape), _const_spec(ng_row.shape)],
        out_specs=[blk(H_A * DV_A), per_b3((CONV_A - 1, QKV_A)), per_b4((H_A, DK_A, DV_A))],
        out_shape=[jax.ShapeDtypeStruct((b, t, H_A * DV_A), F32),
                   jax.ShapeDtypeStruct((b, CONV_A - 1, QKV_A), F32),
                   jax.ShapeDtypeStruct((b, H_A, DK_A, DV_A), F32)],
        scratch_shapes=[pltpu.VMEM((H_A, DK_A, DV_A), F32), pltpu.VMEM((SUBLANES, QKV_A), F32)],
        compiler_params=_cparams(2),
        name="gdn",
    )(qkv_a, misc, z_a, conv_state, rec0, conv_w, alog_row, dtb_row, ng_row)


def _rel_bias_tile(dist, bias_ref):
    nd = jnp.maximum(dist, 0)
    max_exact = N_BUCKETS // 2
    nf = jnp.maximum(nd, max_exact).astype(F32)
    large = max_exact + (jnp.log(nf / max_exact) / math.log(MAX_DIST / max_exact)
                         * (N_BUCKETS - max_exact)).astype(jnp.int32)
    large = jnp.minimum(large, N_BUCKETS - 1)
    bucket = jnp.where(nd < max_exact, nd, large)
    outs = []
    for h in range(H_B):
        val = jnp.full(dist.shape, bias_ref[0, h], F32)
        for bk in range(1, N_BUCKETS):
            val = jnp.where(bucket == bk, bias_ref[bk, h], val)
        outs.append(val)
    return outs


def _order_key(score, admissible):
    bits = pltpu.bitcast(score, jnp.int32)
    key = jnp.where(bits < 0, bits ^ jnp.int32(0x7FFFFFFF), bits)
    key = jnp.where(bits == jnp.int32(INT_MIN), 0, key)
    return jnp.where(admissible, key, jnp.int32(INT_MIN))


def _count(m, ka):
    x = m.astype(F32)
    step = LANES if ka == 1 else SUBLANES
    take = (lambda j: x[:, j:j + step]) if ka == 1 else (lambda j: x[j:j + step, :])
    parts = [take(j) for j in range(0, x.shape[ka], step)]
    while len(parts) > 1:
        odd = [parts[-1]] if len(parts) % 2 else []
        parts = [a + b for a, b in zip(parts[0::2], parts[1::2])] + odd
    return jnp.sum(parts[0], axis=ka, keepdims=True)


def _kth_largest(key_ref, top_k, ka):
    shape = (key_ref.shape[0], 1) if ka == 1 else (1, key_ref.shape[1])

    step = LANES if ka == 1 else SUBLANES
    slab_shape = (key_ref.shape[0], step) if ka == 1 else (step, key_ref.shape[1])

    def body(i, theta):
        inc = lax.shift_left(jnp.int32(1), jnp.int32(31) - i)
        cand = theta + inc
        cand_b = jnp.broadcast_to(cand, slab_shape)
        accs = [None] * COUNT_ACCS
        for n, j in enumerate(range(0, key_ref.shape[ka], step)):
            slab = key_ref[:, j:j + step] if ka == 1 else key_ref[j:j + step, :]
            part = (slab >= cand_b).astype(F32)
            accs[n % COUNT_ACCS] = part if accs[n % COUNT_ACCS] is None else accs[n % COUNT_ACCS] + part
        total = functools.reduce(lambda a, b: a + b, [a for a in accs if a is not None])
        cnt = jnp.sum(total, axis=ka, keepdims=True)
        return jnp.where(cnt >= top_k, cand, theta)

    return lax.fori_loop(0, 32, body, jnp.full(shape, INT_MIN, jnp.int32))


def _select_topk(key_ref, admissible, top_k, mask_ref, tri_ref, ka):
    theta = _kth_largest(key_ref, top_k, ka)
    key = key_ref[...]
    ge = key >= theta
    simple = jnp.logical_or(_count(ge, ka) == top_k, theta == jnp.int32(INT_MIN))
    n_bad = jnp.sum(jnp.where(simple, 0.0, 1.0))

    @pl.when(n_bad == 0.0)
    def _():
        mask_ref[...] = jnp.where(jnp.logical_and(ge, admissible), 0.0, NEG_INF)

    @pl.when(n_bad != 0.0)
    def _():
        gt = key > theta
        eq = key == theta
        need = top_k - _count(gt, ka)
        run = jnp.zeros(theta.shape, F32)
        for c0 in range(0, key.shape[ka], LANES):
            grp = (slice(None), slice(c0, c0 + LANES)) if ka == 1 else (slice(c0, c0 + LANES), slice(None))
            eq_c = eq[grp].astype(F32)
            if ka == 1:
                before = _dot(eq_c.astype(BF16), tri_ref[...]) + run
            else:
                before = _dot(tri_ref[...], eq_c.astype(BF16)) + run
            take = jnp.logical_and(eq[grp], before < need)
            m = jnp.logical_and(jnp.logical_or(gt[grp], take), admissible[grp])
            mask_ref[grp] = jnp.where(m, 0.0, NEG_INF)
            run = run + jnp.sum(eq_c, axis=ka, keepdims=True)


def _init_tri(tri_ref, ka):
    r = lax.broadcasted_iota(jnp.int32, (LANES, LANES), 0)
    c = lax.broadcasted_iota(jnp.int32, (LANES, LANES), 1)
    tri_ref[...] = ((r < c) if ka == 1 else (r > c)).astype(BF16)


def _attn_p_kernel(bias_ref, qb_ref, kb_ref, vb_ref, qidx_ref, miscq_ref, misck_ref, o_ref,
                   k_scr, vt_scr, kidx_scr, btab_scr, mask_scr, key_scr, tri_scr, *, tq, top_k, range_ends):
    b = pl.program_id(0)
    i = pl.program_id(1)
    nq = pl.num_programs(1)

    @pl.when(jnp.logical_and(b == 0, i == 0))
    def _():
        _init_tri(tri_scr, 0)
        kk = lax.broadcasted_iota(jnp.int32, (tq, tq), 0)
        qq = lax.broadcasted_iota(jnp.int32, (tq, tq), 1)
        def fill(dd, carry):
            tiles = _rel_bias_tile(dd * tq + qq - kk, bias_ref)
            r0 = pl.multiple_of((nq - 1 - dd) * tq, tq)
            for h in range(H_B):
                btab_scr[h, pl.ds(r0, tq), :] = tiles[h]
            return carry

        lax.fori_loop(0, nq, fill, 0)
        if nq > 1:
            for h in range(H_B):
                btab_scr[h, nq * tq:(2 * nq - 1) * tq, :] = jnp.zeros(((nq - 1) * tq, tq), F32)

    @pl.when(i == 0)
    def _():
        t_keys = k_scr.shape[0]
        for h in range(H_B):
            hd = slice(h * DH_B, (h + 1) * DH_B)
            k_scr[:, hd] = kb_ref[0, pl.ds(h, t_keys, stride=H_B), :].astype(BF16)
            vt_scr[hd, :] = vb_ref[0, pl.ds(h, t_keys, stride=H_B), :].T.astype(BF16)
        kidx_scr[...] = misck_ref[0, :, MISC_KIDX:MISC_KIDX + D_IDX].astype(BF16)

    def attend(n_keys):
        qidx = qidx_ref[0]
        misc_t = miscq_ref[0].T
        kidx = kidx_scr[0:n_keys, :]
        score = jnp.zeros((n_keys, tq), F32)
        for h in range(H_IDX):
            dots = _dot_nt(kidx, qidx[:, h * D_IDX:(h + 1) * D_IDX].astype(BF16))
            w = misc_t[MISC_WIDX + h:MISC_WIDX + h + 1, :] * (H_IDX ** -0.5) * (D_IDX ** -0.5)
            score = score + jnp.maximum(dots, 0.0) * w

        kpos = lax.broadcasted_iota(jnp.int32, (n_keys, tq), 0)
        qpos = i * tq + lax.broadcasted_iota(jnp.int32, (n_keys, tq), 1)
        admissible = kpos <= qpos
        key_ref = key_scr.at[0:n_keys, :]
        key_ref[...] = _order_key(score, admissible)
        mask_ref = mask_scr.at[0:n_keys, :]
        _select_topk(key_ref, admissible, top_k, mask_ref, tri_scr, 0)

        qb = qb_ref[0]
        boff = pl.multiple_of((nq - 1 - i) * tq, tq)
        for h in range(H_B):
            hd = slice(h * DH_B, (h + 1) * DH_B)
            logits = _dot_nt(k_scr[0:n_keys, hd], qb[:, hd].astype(BF16)) * (DH_B ** -0.5)
            logits = logits + btab_scr[h, pl.ds(boff, n_keys), :] + mask_ref[...]
            m = jnp.max(logits, axis=0, keepdims=True)
            p = jnp.exp(logits - m)
            denom = jnp.sum(p, axis=0, keepdims=True)
            out_t = _dot(vt_scr[hd, 0:n_keys], p.astype(BF16)) / denom
            o_ref[0, :, hd] = out_t.T

    for lo, hi in zip((0,) + range_ends[:-1], range_ends):
        pl.when(jnp.logical_and(i >= lo, i < hi))(functools.partial(attend, hi * tq))


def _attn_p(rel_bias, q_b, k_b, v_b, q_idx, misc, tq=128):
    b, t, _ = q_b.shape
    nq = t // tq
    top_k = min(TOPK_MAX, t // 4)
    e = nq // 8
    range_ends = (e, 2 * e, 3 * e, 4 * e, 6 * e, nq) if nq % 8 == 0 else (nq,)
    qblk = lambda w: pl.BlockSpec((1, tq, w), lambda bi, i: (bi, i, 0))
    full = lambda w: pl.BlockSpec((1, t, w), lambda bi, i: (bi, 0, 0))
    kv = pl.BlockSpec((1, t * H_B, DH_B), lambda bi, i: (bi, 0, 0))
    return pl.pallas_call(
        functools.partial(_attn_p_kernel, tq=tq, top_k=top_k, range_ends=range_ends),
        grid=(b, nq),
        in_specs=[pl.BlockSpec(memory_space=pltpu.SMEM),
                  qblk(HB), kv, kv, qblk(H_IDX * D_IDX), qblk(LANES), full(LANES)],
        out_specs=qblk(HB),
        out_shape=jax.ShapeDtypeStruct((b, t, HB), F32),
        scratch_shapes=[pltpu.VMEM((t, HB), BF16), pltpu.VMEM((HB, t), BF16),
                        pltpu.VMEM((t, D_IDX), BF16),
                        pltpu.VMEM((H_B, (2 * nq - 1) * tq, tq), F32),
                        pltpu.VMEM((t, tq), F32), pltpu.VMEM((t, tq), jnp.int32),
                        pltpu.VMEM((LANES, LANES), BF16)],
        compiler_params=_cparams(2),
        name="attn_p",
    )(rel_bias, q_b, k_b, v_b, q_idx, misc, misc)


def _attn_s_kernel(pt_ref, bias_ref, qb_ref, qidx_ref, widx_ref, knew_ref, vnew_ref, kidxnew_ref,
                   cidx_hbm, ck_hbm, cv_hbm, o_ref,
                   score_scr, logit_scr, mask_scr, key_scr, acc_scr, btab_scr, tri_scr, idx_buf, k_buf, v_buf, sems,
                   *, layer, t_new, n_pages, gp, top_k):
    b = pl.program_id(0)
    n_grp = n_pages // gp
    past = n_pages * PAGE_SIZE
    n_cols = past + PAGE_SIZE
    near0 = max(0, (past - MAX_DIST) // PAGE_SIZE) * PAGE_SIZE
    n_near = n_cols - near0
    pg_rows = PAGE_SIZE * H_B
    grp_keys = gp * PAGE_SIZE
    k_ahead = min(K_SLOTS - 1, n_grp)

    @pl.when(b == 0)
    def _():
        _init_tri(tri_scr, 1)
        trow = lax.broadcasted_iota(jnp.int32, (t_new, n_near), 0)
        col = near0 + lax.broadcasted_iota(jnp.int32, (t_new, n_near), 1)
        tiles = _rel_bias_tile(past + trow - col, bias_ref)
        for h in range(H_B):
            btab_scr[h] = tiles[h]

    def group_copies(hbm, buf, sem_row, slot, grp, seq=None):
        out = []
        for m in range(gp):
            page = pt_ref[b if seq is None else seq, grp * gp + m]
            dst = buf.at[slot, m] if buf is idx_buf else buf.at[slot, pl.ds(m * pg_rows, pg_rows)]
            out.append(pltpu.make_async_copy(hbm.at[layer, page], dst, sems.at[sem_row, slot]))
        return out

    def start_idx_k(slot, grp, seq=None):
        for c in (group_copies(cidx_hbm, idx_buf, 0, slot, grp, seq)
                  + group_copies(ck_hbm, k_buf, 1, slot, grp, seq)):
            c.start()

    def wait_idx_k(slot, grp):
        for c in group_copies(cidx_hbm, idx_buf, 0, slot, grp) + group_copies(ck_hbm, k_buf, 1, slot, grp):
            c.wait()

    def start_v(slot, grp):
        for c in group_copies(cv_hbm, v_buf, 2, slot, grp):
            c.start()

    def wait_v(slot, grp):
        for c in group_copies(cv_hbm, v_buf, 2, slot, grp):
            c.wait()

    qb = qb_ref[0].astype(BF16)
    qidx = qidx_ref[0].astype(BF16)
    widx = widx_ref[0] * (H_IDX ** -0.5)
    head_rows = lambda h: slice(h * t_new, (h + 1) * t_new)
    head_lanes = lambda h: slice(h * DH_B, (h + 1) * DH_B)

    def scores_of(kidx_t):
        dots = _dot(qidx, kidx_t.astype(BF16)) * (D_IDX ** -0.5)
        contrib = jnp.maximum(dots, 0.0) * widx
        s = contrib[0:t_new]
        for h in range(1, H_IDX):
            s = s + contrib[h * t_new:(h + 1) * t_new]
        return s

    @pl.when(b == 0)
    def _():
        for g in range(k_ahead):
            start_idx_k(g, g)

    def pass1(grp, carry):
        slot = grp % K_SLOTS
        ahead = grp + K_SLOTS - 1

        @pl.when(ahead < n_grp)
        def _():
            start_idx_k(ahead % K_SLOTS, ahead)

        wait_idx_k(slot, grp)
        c0 = pl.multiple_of(grp * grp_keys, grp_keys)
        for m in range(gp):
            score_scr[:, pl.ds(c0 + m * PAGE_SIZE, PAGE_SIZE)] = scores_of(idx_buf[slot, m])
        for h in range(H_B):
            k_h = k_buf[slot, pl.ds(h, grp_keys, stride=H_B), :].astype(BF16)
            logit_scr[head_rows(h), pl.ds(c0, grp_keys)] = _dot_nt(qb[:, head_lanes(h)], k_h)
        return carry

    lax.fori_loop(0, n_grp, pass1, 0)
    for g in range(min(V_SLOTS - 1, n_grp)):
        start_v(g, g)

    score_scr[:, past:n_cols] = scores_of(kidxnew_ref[0])
    for h in range(H_B):
        k_h = knew_ref[0, pl.ds(h, PAGE_SIZE, stride=H_B), :].astype(BF16)
        logit_scr[head_rows(h), past:n_cols] = _dot_nt(qb[:, head_lanes(h)], k_h)

    trow = lax.broadcasted_iota(jnp.int32, (t_new, n_cols), 0)
    col = lax.broadcasted_iota(jnp.int32, (t_new, n_cols), 1)
    admissible = col <= past + trow
    key_scr[...] = _order_key(score_scr[...], admissible)
    _select_topk(key_scr, admissible, top_k, mask_scr, tri_scr, 1)
    for h in range(H_B):
        rs = head_rows(h)
        near = logit_scr[rs, near0:n_cols] * (DH_B ** -0.5) + btab_scr[h] + mask_scr[:, near0:n_cols]
        mx = jnp.max(near, axis=-1, keepdims=True)
        if near0 > 0:
            far = logit_scr[rs, 0:near0] * (DH_B ** -0.5) + bias_ref[N_BUCKETS - 1, h] + mask_scr[:, 0:near0]
            mx = jnp.maximum(mx, jnp.max(far, axis=-1, keepdims=True))
        pn = jnp.exp(near - mx)
        denom = jnp.sum(pn, axis=-1, keepdims=True)
        if near0 > 0:
            pf = jnp.exp(far - mx)
            denom = denom + jnp.sum(pf, axis=-1, keepdims=True)
            logit_scr[rs, 0:near0] = pf / denom
        logit_scr[rs, near0:n_cols] = pn / denom
        v_h = vnew_ref[0, pl.ds(h, PAGE_SIZE, stride=H_B), :].astype(BF16)
        acc_scr[rs, :] = _dot(logit_scr[rs, past:n_cols].astype(BF16), v_h)

    @pl.when(b + 1 < pl.num_programs(0))
    def _():
        for g in range(k_ahead):
            start_idx_k(g, g, b + 1)

    def pass2(grp, carry):
        slot = grp % V_SLOTS
        ahead = grp + V_SLOTS - 1

        @pl.when(ahead < n_grp)
        def _():
            start_v(ahead % V_SLOTS, ahead)

        wait_v(slot, grp)
        c0 = pl.multiple_of(grp * grp_keys, grp_keys)
        for h in range(H_B):
            v_h = v_buf[slot, pl.ds(h, grp_keys, stride=H_B), :].astype(BF16)
            p_h = logit_scr[head_rows(h), pl.ds(c0, grp_keys)].astype(BF16)
            acc_scr[head_rows(h), :] += _dot(p_h, v_h)
        return carry

    lax.fori_loop(0, n_grp, pass2, 0)
    for h in range(H_B):
        o_ref[0, :, head_lanes(h)] = acc_scr[head_rows(h), :]


def _attn_s(page_table, rel_bias, q_b, k_new, v_new, q_idx, kidx_new, misc, cache_k, cache_v, cache_idx_k,
            layer):
    b, t_new, _ = q_b.shape
    n_pages = page_table.shape[1]
    past = n_pages * PAGE_SIZE
    n_cols = past + PAGE_SIZE
    top_k = min(TOPK_MAX, (past + t_new) // 4)
    gp = math.gcd(n_pages, 16)
    hq = H_B * t_new
    near0 = max(0, (past - MAX_DIST) // PAGE_SIZE) * PAGE_SIZE

    qidx = q_idx.reshape(b, t_new, H_IDX, D_IDX).transpose(0, 2, 1, 3).reshape(b, H_IDX * t_new, D_IDX)
    widx = misc[:, :, MISC_WIDX:MISC_WIDX + H_IDX].transpose(0, 2, 1).reshape(b, H_IDX * t_new, 1)
    pad_page = lambda a, r: jnp.pad(a, ((0, 0), (0, (PAGE_SIZE - t_new) * r), (0, 0)))
    k_new, v_new = pad_page(k_new, H_B), pad_page(v_new, H_B)
    kidx_new_t = pad_page(kidx_new, 1).transpose(0, 2, 1)
    ck = cache_k.reshape(cache_k.shape[0], cache_k.shape[1], PAGE_SIZE * H_B, DH_B)
    cv = cache_v.reshape(cache_v.shape[0], cache_v.shape[1], PAGE_SIZE * H_B, DH_B)
    cidx_t = cache_idx_k.transpose(0, 1, 3, 2)

    per_b = lambda s: pl.BlockSpec((1,) + s, lambda bi, pt: (bi, 0, 0))
    hbm = pl.BlockSpec(memory_space=pl.ANY)
    in_specs = [pl.BlockSpec(memory_space=pltpu.SMEM),
                per_b((t_new, HB)), per_b((H_IDX * t_new, D_IDX)), per_b((H_IDX * t_new, 1)),
                per_b((PAGE_SIZE * H_B, DH_B)), per_b((PAGE_SIZE * H_B, DH_B)), per_b((D_IDX, PAGE_SIZE)),
                hbm, hbm, hbm]
    grid_spec = pltpu.PrefetchScalarGridSpec(
        num_scalar_prefetch=1,
        grid=(b,),
        in_specs=in_specs,
        out_specs=pl.BlockSpec((1, t_new, HB), lambda bi, pt: (bi, 0, 0)),
        scratch_shapes=[pltpu.VMEM((t_new, n_cols), F32), pltpu.VMEM((hq, n_cols), F32),
                        pltpu.VMEM((t_new, n_cols), F32), pltpu.VMEM((t_new, n_cols), jnp.int32),
                        pltpu.VMEM((hq, DH_B), F32),
                        pltpu.VMEM((H_B, t_new, n_cols - near0), F32),
                        pltpu.VMEM((LANES, LANES), BF16),
                        pltpu.VMEM((K_SLOTS, gp, D_IDX, PAGE_SIZE), F32),
                        pltpu.VMEM((K_SLOTS, gp * PAGE_SIZE * H_B, DH_B), F32),
                        pltpu.VMEM((V_SLOTS, gp * PAGE_SIZE * H_B, DH_B), F32),
                        pltpu.SemaphoreType.DMA((3, max(K_SLOTS, V_SLOTS)))])
    return pl.pallas_call(
        functools.partial(_attn_s_kernel, layer=layer, t_new=t_new, n_pages=n_pages, gp=gp, top_k=top_k),
        grid_spec=grid_spec,
        out_shape=jax.ShapeDtypeStruct((b, t_new, HB), F32),
        compiler_params=_cparams(1),
        name="attn_s",
    )(page_table, rel_bias, q_b, qidx, widx, k_new, v_new, kidx_new_t, cidx_t, ck, cv)


def _ffn_kernel(x_ref, oa_ref, ob_ref, ga_ref, gb_ref, fst_ref, wpa_ref, wpb_ref, wout_ref, n2_ref,
                wup_ref, cfw_ref, cfb_ref, wdown_ref, fg_ref, y_ref, fnew_ref, prev_scr, act_scr,
                *, tm, ch, d_ff, seq):
    t = pl.program_id(1)
    nt = pl.num_programs(1)
    npr = CONV_F - 1
    if seq is None:
        @pl.when(t == 0)
        def _():
            prev_scr[SUBLANES - npr:SUBLANES, :] = fst_ref[0]

    pa = _dot(oa_ref[0].astype(BF16), wpa_ref[...])
    pb = _dot(ob_ref[0].astype(BF16), wpb_ref[...])
    merged = jax.nn.sigmoid(ga_ref[0]) * pa + jax.nn.sigmoid(gb_ref[0]) * pb
    x1 = x_ref[0] + _dot(merged.astype(BF16), wout_ref[...])
    xn2 = _rms(x1, n2_ref[...]).astype(BF16)

    def conv_cols(c0):
        cols = slice(c0, c0 + ch)
        up = _dot(xn2, wup_ref[:, cols])
        if seq is None:
            prev = prev_scr[:, cols]
            s2, s1 = _shift_rows(up, 2, prev), _shift_rows(up, 1, prev)
            prev_scr[SUBLANES - npr:SUBLANES, cols] = up[tm - npr:tm, :]
        else:
            hist = fst_ref[0, :, cols]
            tpos = lax.broadcasted_iota(jnp.int32, up.shape, 0) & (seq - 1)
            s2 = jnp.where(tpos < 2, hist, pltpu.roll(up, 2, 0))
            s1 = jnp.where(tpos < 1, pltpu.roll(hist, tm - 1, 0), pltpu.roll(up, 1, 0))
            fnew_ref[:, :, cols] = up.reshape(tm // seq, seq, ch)[:, seq - npr:seq, :]
        out = s2 * cfw_ref[0:1, cols]
        out = out + s1 * cfw_ref[1:2, cols]
        out = out + up * cfw_ref[2:3, cols]
        return out + cfb_ref[:, cols]

    for c0 in range(0, d_ff, ch):
        gate = conv_cols(c0)
        val = conv_cols(d_ff + c0)
        act_scr[:, c0:c0 + ch] = (_silu(gate) * val).astype(BF16)
    y_ref[0] = _rms(x1 + _dot(act_scr[...], wdown_ref[...]), fg_ref[...])

    if seq is None:
        @pl.when(t == nt - 1)
        def _():
            fnew_ref[0] = prev_scr[SUBLANES - npr:SUBLANES, :]


def _ffn(x, o_a, o_b, gate_a, gate_b, ffn_state, wpa, wpb, wout, n2_row, wup, cfw, cfb_row, wdown, fg_row, tm):
    b, t, d = x.shape
    d_ff = wdown.shape[0]
    ch = 256
    npr = CONV_F - 1
    consts = (wpa, wpb, wout, n2_row, wup, cfw, cfb_row, wdown, fg_row)
    if t == SUBLANES and t > npr:
        seq, rows = t, b * t
        flat = lambda a: a.reshape(1, rows, a.shape[-1])
        x, o_a, o_b, gate_a, gate_b = map(flat, (x, o_a, o_b, gate_a, gate_b))
        hist = flat(jnp.pad(ffn_state, ((0, 0), (0, t - npr), (0, 0))))
        grid, tm = (1, 1), rows
        blk = lambda w: pl.BlockSpec((1, rows, w), lambda i, j: (0, 0, 0))
        state_in, state_out = blk(2 * d_ff), pl.BlockSpec((b, npr, 2 * d_ff), lambda i, j: (0, 0, 0))
    else:
        seq, hist = None, ffn_state
        grid = (b, t // tm)
        blk = lambda w: pl.BlockSpec((1, tm, w), lambda i, j: (i, j, 0))
        state_in = state_out = pl.BlockSpec((1, npr, 2 * d_ff), lambda i, j: (i, 0, 0))
    y, ffn_new = pl.pallas_call(
        functools.partial(_ffn_kernel, tm=tm, ch=ch, d_ff=d_ff, seq=seq),
        grid=grid,
        in_specs=[blk(d), blk(o_a.shape[-1]), blk(o_b.shape[-1]), blk(d), blk(d), state_in]
                 + [_const_spec(c.shape) for c in consts],
        out_specs=[blk(d), state_out],
        out_shape=[jax.ShapeDtypeStruct(x.shape, F32),
                   jax.ShapeDtypeStruct((b, npr, 2 * d_ff), F32)],
        scratch_shapes=[pltpu.VMEM((SUBLANES, 2 * d_ff), F32), pltpu.VMEM((tm, d_ff), BF16)],
        compiler_params=_cparams(2),
        name="ffn",
    )(x, o_a, o_b, gate_a, gate_b, hist, *consts)
    return y.reshape(b, t, d), ffn_new


def _misc_row(vals, col0):
    return jnp.zeros((1, LANES), F32).at[0, col0:col0 + vals.shape[0]].set(vals.astype(F32))


def _row_tiles(b, t):
    return dict(proj_tm=min(b * t, ROW_TILE), gdn_rows=min(t, ROW_TILE), ffn_tm=min(t, 2 * ROW_TILE))


def _layer(x, conv_state, rec_state, ffn_state, attend, lw, proj_tm, gdn_rows, ffn_tm):
    b, t, d = x.shape
    (w_in_bf, conv_a_w, alog_row, dtb_row, ng_row, wpa, wpb, wout, n1_row, n2_row, wup, cfw, cfb_row,
     wdown, fg_row) = lw
    (qkv_a, z_a, q_b, k_b, v_b, q_idx, gate_a, gate_b, misc, k_idx) = _proj(
        x.reshape(b * t, d), n1_row, w_in_bf, proj_tm)
    r3 = lambda a: a.reshape(b, -1, a.shape[-1])
    qkv_a, z_a, q_b, k_b, v_b, q_idx, gate_a, gate_b, misc, k_idx = map(
        r3, (qkv_a, z_a, q_b, k_b, v_b, q_idx, gate_a, gate_b, misc, k_idx))
    o_a, conv_new, rec_new = _gdn(qkv_a, misc, z_a, conv_state, rec_state, conv_a_w, alog_row, dtb_row,
                                  ng_row, gdn_rows)
    o_b = attend(q_b, k_b, v_b, q_idx, k_idx, misc)
    y, ffn_new = _ffn(x, o_a, o_b, gate_a, gate_b, ffn_state, wpa, wpb, wout, n2_row, wup, cfw, cfb_row,
                      wdown, fg_row, ffn_tm)
    kv_shape = (b, t, H_B, DH_B)
    return y, (k_b.reshape(kv_shape), v_b.reshape(kv_shape), k_idx, conv_new, rec_new, ffn_new)


def kernel(x_prompt, x_sample, cache_k, cache_v, cache_idx_k, state_a_conv, state_a_rec, state_ffn_conv,
           page_table, w_in, conv_a_w, a_log, dt_bias, norm_a_g, w_proj_a, w_proj_b, w_out, rel_bias,
           norm1_g, norm2_g, w_up, conv_f_w, conv_f_b, w_down, final_g):
    depth = w_in.shape[0]
    assert depth == 1, "the final norm is fused into the layer's last kernel"
    bp, tp, d = x_prompt.shape
    bs, ts, _ = x_sample.shape
    d_ff = w_down.shape[1]
    l = 0
    lw = (_prep_w_in(w_in[l], d), conv_a_w[l], _misc_row(a_log[l], MISC_A), _misc_row(dt_bias[l], MISC_A),
          norm_a_g[l].reshape(1, -1), w_proj_a[l].astype(BF16), w_proj_b[l].astype(BF16),
          w_out[l].astype(BF16), norm1_g[l].reshape(1, -1), norm2_g[l].reshape(1, -1),
          w_up[l].astype(BF16), conv_f_w[l], conv_f_b[l].reshape(1, -1), w_down[l].astype(BF16),
          final_g.reshape(1, -1))

    def attend_prompt(q_b, k_b, v_b, q_idx, k_idx, misc):
        return _attn_p(rel_bias, q_b, k_b, v_b, q_idx, misc)

    def attend_sample(q_b, k_b, v_b, q_idx, k_idx, misc):
        return _attn_s(page_table, rel_bias, q_b, k_b, v_b, q_idx, k_idx, misc, cache_k, cache_v,
                       cache_idx_k, l)

    zeros = lambda *s: jnp.zeros(s, x_prompt.dtype)
    y_p, st_p = _layer(x_prompt, zeros(bp, CONV_A - 1, QKV_A), zeros(bp, H_A, DK_A, DV_A),
                       zeros(bp, CONV_F - 1, 2 * d_ff), attend_prompt, lw, **_row_tiles(bp, tp))
    y_s, st_s = _layer(x_sample, state_a_conv[l], state_a_rec[l], state_ffn_conv[l], attend_sample, lw,
                       **_row_tiles(bs, ts))
    stack = lambda a: a[None]
    return (y_p, y_s) + tuple(map(stack, st_p)) + tuple(map(stack, st_s))
```

```python
import functools
import math

import jax
import jax.numpy as jnp
from jax import lax
from jax.experimental import pallas as pl
from jax.experimental.pallas import tpu as pltpu

F32 = jnp.float32
BF16 = jnp.bfloat16
HI = lax.Precision.HIGHEST

H_A, DK_A, DV_A, CONV_A, CHUNK_A = 4, 128, 128, 4, 64
H_B, DH_B, H_IDX, D_IDX = 4, 128, 8, 64
TOPK_MAX, N_BUCKETS, MAX_DIST = 256, 32, 2048
CONV_F, EPS, PAGE_SIZE = 3, 1e-6, 128
QKV_A = H_A * (2 * DK_A + DV_A)
HB = H_B * DH_B

LANES = 128
SUBLANES = 8
VMEM_LIMIT = 56 * 1024 * 1024
ROW_TILE = 256

MISC_KIDX, MISC_BETA, MISC_A, MISC_WIDX = 0, D_IDX, D_IDX + H_A, D_IDX + 2 * H_A

GDN_MM = "bf"

INT_MIN = -2 ** 31
COUNT_ACCS = 4
K_SLOTS = 3
V_SLOTS = 5
NEG_INF = float("-inf")


def _cparams(n_axes):
    return pltpu.CompilerParams(dimension_semantics=("arbitrary",) * n_axes,
                                vmem_limit_bytes=VMEM_LIMIT)


def _const_spec(shape):
    nd = len(shape)
    return pl.BlockSpec(shape, lambda *_: (0,) * nd, pipeline_mode=pl.Buffered(1))


def _silu(x):
    return x * jax.nn.sigmoid(x)


def _dot(a, b, precision=None):
    return jnp.dot(a, b, preferred_element_type=F32, precision=precision)


def _dot_nt(a, b, precision=None):
    return lax.dot_general(a, b, (((1,), (1,)), ((), ())), preferred_element_type=F32,
                           precision=precision)


_DIMS = {"nn": (((1,), (0,)), ((), ())), "nt": (((1,), (1,)), ((), ())), "tn": (((0,), (0,)), ((), ()))}


def _mm(a, b, mode, form="nn"):
    if mode == "bf":
        return lax.dot_general(a.astype(BF16), b.astype(BF16), _DIMS[form], preferred_element_type=F32)
    return lax.dot_general(a, b, _DIMS[form], preferred_element_type=F32, precision=HI)


def _cumsum_rows(x):
    row = lax.broadcasted_iota(jnp.int32, x.shape, 0)
    s = 1
    while s < x.shape[0]:
        x = x + jnp.where(row >= s, pltpu.roll(x, s, 0), 0.0)
        s *= 2
    return x


def _rms(x, g):
    return x * lax.rsqrt(jnp.mean(x * x, axis=-1, keepdims=True) + EPS) * g


def _shift_rows(x, k, prev):
    n = prev.shape[0]
    out = pltpu.roll(x, k, 0)
    head = out[0:SUBLANES]
    row = lax.broadcasted_iota(jnp.int32, head.shape, 0)
    for r in range(k):
        head = jnp.where(row == r, prev[n - k + r:n - k + r + 1, :], head)
    return head if x.shape[0] == SUBLANES else jnp.concatenate([head, out[SUBLANES:]], axis=0)


def _proj_kernel(x_ref, g_ref, w_ref, qkva_ref, za_ref, qb_ref, kb_ref, vb_ref, qidx_ref,
                 ga_ref, gb_ref, misc_ref, kidx_ref):
    xn = _rms(x_ref[...], g_ref[...]).astype(BF16)
    tm = x_ref.shape[0]
    off = 0
    for o_ref in (qkva_ref, za_ref, qb_ref, kb_ref, vb_ref, qidx_ref, ga_ref, gb_ref, misc_ref):
        if o_ref is kb_ref or o_ref is vb_ref:
            res = _dot(xn, w_ref[:, off:off + HB])
            for h in range(H_B):
                o_ref[pl.ds(h, tm, stride=H_B), :] = res[:, h * DH_B:(h + 1) * DH_B]
            off += HB
        else:
            wd = o_ref.shape[-1]
            o_ref[...] = _dot(xn, w_ref[:, off:off + wd])
            off += wd
    kidx_ref[...] = misc_ref[:, MISC_KIDX:MISC_KIDX + D_IDX]


def _prep_w_in(w_in_l, d_model):
    sizes = (QKV_A, H_A, H_A, H_A * DV_A, 3 * HB, H_IDX * D_IDX, D_IDX, H_IDX, d_model, d_model)
    parts, start = [], 0
    for s in sizes:
        parts.append(w_in_l[:, start:start + s])
        start += s
    qkv_a, beta, a_raw, z_a, qkv_b, q_idx, k_idx, w_idx, gate_a, gate_b = parts
    pad = jnp.zeros((w_in_l.shape[0], LANES - D_IDX - 2 * H_A - H_IDX), w_in_l.dtype)
    misc = jnp.concatenate([k_idx, beta, a_raw, w_idx, pad], axis=1)
    return jnp.concatenate([qkv_a, z_a, qkv_b, q_idx, gate_a, gate_b, misc], axis=1).astype(BF16)


def _proj(x2d, g_row, w_bf, tm):
    n, d = x2d.shape
    shapes = ((1, QKV_A), (1, H_A * DV_A), (1, HB), (H_B, DH_B), (H_B, DH_B), (1, H_IDX * D_IDX), (1, d), (1, d),
              (1, LANES), (1, D_IDX))
    row = lambda i: (i, 0)
    return pl.pallas_call(
        _proj_kernel,
        grid=(n // tm,),
        in_specs=[pl.BlockSpec((tm, d), row), _const_spec(g_row.shape), _const_spec(w_bf.shape)],
        out_specs=[pl.BlockSpec((tm * r, wd), row) for r, wd in shapes],
        out_shape=[jax.ShapeDtypeStruct((n * r, wd), F32) for r, wd in shapes],
        compiler_params=_cparams(1),
        name="proj",
    )(x2d, g_row, w_bf)


def _gdn_kernel(qkv_ref, misc_ref, z_ref, cst_ref, rec0_ref, cw_ref, alog_ref, dtb_ref, ng_ref,
                o_ref, cnew_ref, rnew_ref, s_scr, prev_scr, *, rows, chunk):
    t = pl.program_id(1)
    nt = pl.num_programs(1)
    npr = CONV_A - 1

    @pl.when(t == 0)
    def _():
        s_scr[...] = rec0_ref[0]
        prev_scr[SUBLANES - npr:SUBLANES, :] = cst_ref[0]

    x = qkv_ref[0]
    prev = prev_scr[...]
    conv = _shift_rows(x, 3, prev) * cw_ref[0:1, :]
    conv = conv + _shift_rows(x, 2, prev) * cw_ref[1:2, :]
    conv = conv + _shift_rows(x, 1, prev) * cw_ref[2:3, :]
    conv = conv + x * cw_ref[3:4, :]
    prev_scr[SUBLANES - npr:SUBLANES, :] = x[rows - npr:rows, :]

    @pl.when(t == nt - 1)
    def _():
        cnew_ref[0] = x[rows - npr:rows, :]

    act = _silu(conv)
    misc = misc_ref[0]
    beta_full = jax.nn.sigmoid(misc)
    sp = misc + dtb_ref[...]
    softplus = jnp.maximum(sp, 0.0) + jnp.log1p(jnp.exp(-jnp.abs(sp)))
    g_full = -jnp.exp(alog_ref[...]) * softplus

    ri = lax.broadcasted_iota(jnp.int32, (chunk, chunk), 0)
    ci = lax.broadcasted_iota(jnp.int32, (chunk, chunk), 1)
    tril = ri >= ci
    strict = ri > ci
    eye = (ri == ci).astype(F32)
    lane = lax.broadcasted_iota(jnp.int32, (chunk, LANES), 1)
    n_levels = int(math.log2(chunk))
    hk = H_A * DK_A
    chunks = list(range(0, rows, chunk))
    pairs = [(c0, h) for c0 in chunks for h in range(H_A)]

    gcum = {c0: _cumsum_rows(g_full[c0:c0 + chunk, :]) for c0 in chunks}
    gcum_t = {c0: gcum[c0].T for c0 in chunks} if chunk >= 64 else None

    pre = {}
    for c0, h in pairs:
        q = act[c0:c0 + chunk, h * DK_A:(h + 1) * DK_A]
        k = act[c0:c0 + chunk, hk + h * DK_A:hk + (h + 1) * DK_A]
        v = act[c0:c0 + chunk, 2 * hk + h * DV_A:2 * hk + (h + 1) * DV_A]
        q = q * lax.rsqrt(jnp.sum(q * q, axis=-1, keepdims=True) + EPS) * (DK_A ** -0.5)
        k = k * lax.rsqrt(jnp.sum(k * k, axis=-1, keepdims=True) + EPS)
        beta = beta_full[c0:c0 + chunk, MISC_BETA + h:MISC_BETA + h + 1]
        gc = gcum[c0][:, MISC_A + h:MISC_A + h + 1]
        if gcum_t is not None:
            g_row = gcum_t[c0][MISC_A + h:MISC_A + h + 1, :]
        else:
            g_row = _mm((lane == MISC_A + h).astype(F32), gcum[c0], "hi", "nt")
        gamma = jnp.where(tril, jnp.exp(jnp.where(tril, gc - g_row, 0.0)), 0.0)
        kb = k * beta
        egc = jnp.exp(gc)
        g_last = gc[chunk - 1:chunk, :]
        pre[c0, h] = dict(
            xp=-jnp.where(strict, _mm(kb, k, GDN_MM, "nt") * gamma, 0.0),
            rhs=jnp.concatenate([kb * egc, v * beta], axis=-1),
            attn=jnp.where(tril, _mm(q, k, GDN_MM, "nt") * gamma, 0.0),
            q_dec=q * egc, k_dec=k * jnp.exp(g_last - gc), e_last=jnp.exp(g_last))
    for p in pre.values():
        p["inv"] = eye + p["xp"]
    for _ in range(n_levels - 1):
        for p in pre.values():
            p["xp"] = _mm(p["xp"], p["xp"], GDN_MM)
        for p in pre.values():
            p["inv"] = p["inv"] + _mm(p["inv"], p["xp"], GDN_MM)
    for p in pre.values():
        p["sol"] = _mm(p["inv"], p["rhs"], GDN_MM)

    for c0, h in pairs:
        p = pre[c0, h]
        s = s_scr[h]
        v_new = p["sol"][:, DK_A:] - _mm(p["sol"][:, :DK_A], s, GDN_MM)
        o = _mm(p["q_dec"], s, GDN_MM) + _mm(p["attn"], v_new, GDN_MM)
        s_scr[h] = s * p["e_last"] + _mm(p["k_dec"], v_new, GDN_MM, "tn")
        z = z_ref[0, c0:c0 + chunk, h * DV_A:(h + 1) * DV_A]
        o_ref[0, c0:c0 + chunk, h * DV_A:(h + 1) * DV_A] = _rms(o, ng_ref[...]) * _silu(z)

    @pl.when(t == nt - 1)
    def _():
        rnew_ref[0] = s_scr[...]


def _gdn(qkv_a, misc, z_a, conv_state, rec0, conv_w, alog_row, dtb_row, ng_row, rows):
    b, t, _ = qkv_a.shape
    chunk = math.gcd(t, CHUNK_A)
    blk = lambda w: pl.BlockSpec((1, rows, w), lambda i, j: (i, j, 0))
    per_b3 = lambda s: pl.BlockSpec((1,) + s, lambda i, j: (i, 0, 0))
    per_b4 = lambda s: pl.BlockSpec((1,) + s, lambda i, j: (i, 0, 0, 0))
    return pl.pallas_call(
        functools.partial(_gdn_kernel, rows=rows, chunk=chunk),
        grid=(b, t // rows),
        in_specs=[blk(QKV_A), blk(LANES), blk(H_A * DV_A),
                  per_b3((CONV_A - 1, QKV_A)), per_b4((H_A, DK_A, DV_A)),
                  _const_spec(conv_w.shape), _const_spec(alog_row.shape),
                  _const_spec(dtb_row.shape), _const_spec(ng_row.shape)],
        out_specs=[blk(H_A * DV_A), per_b3((CONV_A - 1, QKV_A)), per_b4((H_A, DK_A, DV_A))],
        out_shape=[jax.ShapeDtypeStruct((b, t, H_A * DV_A), F32),
                   jax.ShapeDtypeStruct((b, CONV_A - 1, QKV_A), F32),
                   jax.ShapeDtypeStruct((b, H_A, DK_A, DV_A), F32)],
        scratch_shapes=[pltpu.VMEM((H_A, DK_A, DV_A), F32), pltpu.VMEM((SUBLANES, QKV_A), F32)],
        compiler_params=_cparams(2),
        name="gdn",
    )(qkv_a, misc, z_a, conv_state, rec0, conv_w, alog_row, dtb_row, ng_row)


def _rel_bias_tile(dist, bias_ref):
    nd = jnp.maximum(dist, 0)
    max_exact = N_BUCKETS // 2
    nf = jnp.maximum(nd, max_exact).astype(F32)
    large = max_exact + (jnp.log(nf / max_exact) / math.log(MAX_DIST / max_exact)
                         * (N_BUCKETS - max_exact)).astype(jnp.int32)
    large = jnp.minimum(large, N_BUCKETS - 1)
    bucket = jnp.where(nd < max_exact, nd, large)
    outs = []
    for h in range(H_B):
        val = jnp.full(dist.shape, bias_ref[0, h], F32)
        for bk in range(1, N_BUCKETS):
            val = jnp.where(bucket == bk, bias_ref[bk, h], val)
        outs.append(val)
    return outs


def _order_key(score, admissible):
    bits = pltpu.bitcast(score, jnp.int32)
    key = jnp.where(bits < 0, bits ^ jnp.int32(0x7FFFFFFF), bits)
    key = jnp.where(bits == jnp.int32(INT_MIN), 0, key)
    return jnp.where(admissible, key, jnp.int32(INT_MIN))


def _count(m, ka):
    x = m.astype(F32)
    step = LANES if ka == 1 else SUBLANES
    take = (lambda j: x[:, j:j + step]) if ka == 1 else (lambda j: x[j:j + step, :])
    parts = [take(j) for j in range(0, x.shape[ka], step)]
    while len(parts) > 1:
        odd = [parts[-1]] if len(parts) % 2 else []
        parts = [a + b for a, b in zip(parts[0::2], parts[1::2])] + odd
    return jnp.sum(parts[0], axis=ka, keepdims=True)


def _kth_largest(key_ref, top_k, ka):
    shape = (key_ref.shape[0], 1) if ka == 1 else (1, key_ref.shape[1])

    step = LANES if ka == 1 else SUBLANES
    slab_shape = (key_ref.shape[0], step) if ka == 1 else (step, key_ref.shape[1])

    def body(i, theta):
        inc = lax.shift_left(jnp.int32(1), jnp.int32(31) - i)
        cand = theta + inc
        cand_b = jnp.broadcast_to(cand, slab_shape)
        accs = [None] * COUNT_ACCS
        for n, j in enumerate(range(0, key_ref.shape[ka], step)):
            slab = key_ref[:, j:j + step] if ka == 1 else key_ref[j:j + step, :]
            part = (slab >= cand_b).astype(F32)
            accs[n % COUNT_ACCS] = part if accs[n % COUNT_ACCS] is None else accs[n % COUNT_ACCS] + part
        total = functools.reduce(lambda a, b: a + b, [a for a in accs if a is not None])
        cnt = jnp.sum(total, axis=ka, keepdims=True)
        return jnp.where(cnt >= top_k, cand, theta)

    return lax.fori_loop(0, 32, body, jnp.full(shape, INT_MIN, jnp.int32))


def _select_topk(key_ref, admissible, top_k, mask_ref, tri_ref, ka):
    theta = _kth_largest(key_ref, top_k, ka)
    key = key_ref[...]
    ge = key >= theta
    simple = jnp.logical_or(_count(ge, ka) == top_k, theta == jnp.int32(INT_MIN))
    n_bad = jnp.sum(jnp.where(simple, 0.0, 1.0))

    @pl.when(n_bad == 0.0)
    def _():
        mask_ref[...] = jnp.where(jnp.logical_and(ge, admissible), 0.0, NEG_INF)

    @pl.when(n_bad != 0.0)
    def _():
        gt = key > theta
        eq = key == theta
        need = top_k - _count(gt, ka)
        run = jnp.zeros(theta.shape, F32)
        for c0 in range(0, key.shape[ka], LANES):
            grp = (slice(None), slice(c0, c0 + LANES)) if ka == 1 else (slice(c0, c0 + LANES), slice(None))
            eq_c = eq[grp].astype(F32)
            if ka == 1:
                before = _dot(eq_c.astype(BF16), tri_ref[...]) + run
            else:
                before = _dot(tri_ref[...], eq_c.astype(BF16)) + run
            take = jnp.logical_and(eq[grp], before < need)
            m = jnp.logical_and(jnp.logical_or(gt[grp], take), admissible[grp])
            mask_ref[grp] = jnp.where(m, 0.0, NEG_INF)
            run = run + jnp.sum(eq_c, axis=ka, keepdims=True)


def _init_tri(tri_ref, ka):
    r = lax.broadcasted_iota(jnp.int32, (LANES, LANES), 0)
    c = lax.broadcasted_iota(jnp.int32, (LANES, LANES), 1)
    tri_ref[...] = ((r < c) if ka == 1 else (r > c)).astype(BF16)


def _attn_p_kernel(bias_ref, qb_ref, kb_ref, vb_ref, qidx_ref, miscq_ref, misck_ref, o_ref,
                   k_scr, vt_scr, kidx_scr, btab_scr, mask_scr, key_scr, tri_scr, *, tq, top_k, range_ends):
    b = pl.program_id(0)
    i = pl.program_id(1)
    nq = pl.num_programs(1)

    @pl.when(jnp.logical_and(b == 0, i == 0))
    def _():
        _init_tri(tri_scr, 0)
        kk = lax.broadcasted_iota(jnp.int32, (tq, tq), 0)
        qq = lax.broadcasted_iota(jnp.int32, (tq, tq), 1)
        def fill(dd, carry):
            tiles = _rel_bias_tile(dd * tq + qq - kk, bias_ref)
            r0 = pl.multiple_of((nq - 1 - dd) * tq, tq)
            for h in range(H_B):
                btab_scr[h, pl.ds(r0, tq), :] = tiles[h]
            return carry

        lax.fori_loop(0, nq, fill, 0)
        if nq > 1:
            for h in range(H_B):
                btab_scr[h, nq * tq:(2 * nq - 1) * tq, :] = jnp.zeros(((nq - 1) * tq, tq), F32)

    @pl.when(i == 0)
    def _():
        t_keys = k_scr.shape[0]
        for h in range(H_B):
            hd = slice(h * DH_B, (h + 1) * DH_B)
            k_scr[:, hd] = kb_ref[0, pl.ds(h, t_keys, stride=H_B), :].astype(BF16)
            vt_scr[hd, :] = vb_ref[0, pl.ds(h, t_keys, stride=H_B), :].T.astype(BF16)
        kidx_scr[...] = misck_ref[0, :, MISC_KIDX:MISC_KIDX + D_IDX].astype(BF16)

    def attend(n_keys):
        qidx = qidx_ref[0]
        misc_t = miscq_ref[0].T
        kidx = kidx_scr[0:n_keys, :]
        score = jnp.zeros((n_keys, tq), F32)
        for h in range(H_IDX):
            dots = _dot_nt(kidx, qidx[:, h * D_IDX:(h + 1) * D_IDX].astype(BF16))
            w = misc_t[MISC_WIDX + h:MISC_WIDX + h + 1, :] * (H_IDX ** -0.5) * (D_IDX ** -0.5)
            score = score + jnp.maximum(dots, 0.0) * w

        kpos = lax.broadcasted_iota(jnp.int32, (n_keys, tq), 0)
        qpos = i * tq + lax.broadcasted_iota(jnp.int32, (n_keys, tq), 1)
        admissible = kpos <= qpos
        key_ref = key_scr.at[0:n_keys, :]
        key_ref[...] = _order_key(score, admissible)
        mask_ref = mask_scr.at[0:n_keys, :]
        _select_topk(key_ref, admissible, top_k, mask_ref, tri_scr, 0)

        qb = qb_ref[0]
        boff = pl.multiple_of((nq - 1 - i) * tq, tq)
        for h in range(H_B):
            hd = slice(h * DH_B, (h + 1) * DH_B)
            logits = _dot_nt(k_scr[0:n_keys, hd], qb[:, hd].astype(BF16)) * (DH_B ** -0.5)
            logits = logits + btab_scr[h, pl.ds(boff, n_keys), :] + mask_ref[...]
            m = jnp.max(logits, axis=0, keepdims=True)
            p = jnp.exp(logits - m)
            denom = jnp.sum(p, axis=0, keepdims=True)
            out_t = _dot(vt_scr[hd, 0:n_keys], p.astype(BF16)) / denom
            o_ref[0, :, hd] = out_t.T

    for lo, hi in zip((0,) + range_ends[:-1], range_ends):
        pl.when(jnp.logical_and(i >= lo, i < hi))(functools.partial(attend, hi * tq))


def _attn_p(rel_bias, q_b, k_b, v_b, q_idx, misc, tq=128):
    b, t, _ = q_b.shape
    nq = t // tq
    top_k = min(TOPK_MAX, t // 4)
    e = nq // 8
    range_ends = (e, 2 * e, 3 * e, 4 * e, 6 * e, nq) if nq % 8 == 0 else (nq,)
    qblk = lambda w: pl.BlockSpec((1, tq, w), lambda bi, i: (bi, i, 0))
    full = lambda w: pl.BlockSpec((1, t, w), lambda bi, i: (bi, 0, 0))
    kv = pl.BlockSpec((1, t * H_B, DH_B), lambda bi, i: (bi, 0, 0))
    return pl.pallas_call(
        functools.partial(_attn_p_kernel, tq=tq, top_k=top_k, range_ends=range_ends),
        grid=(b, nq),
        in_specs=[pl.BlockSpec(memory_space=pltpu.SMEM),
                  qblk(HB), kv, kv, qblk(H_IDX * D_IDX), qblk(LANES), full(LANES)],
        out_specs=qblk(HB),
        out_shape=jax.ShapeDtypeStruct((b, t, HB), F32),
        scratch_shapes=[pltpu.VMEM((t, HB), BF16), pltpu.VMEM((HB, t), BF16),
                        pltpu.VMEM((t, D_IDX), BF16),
                        pltpu.VMEM((H_B, (2 * nq - 1) * tq, tq), F32),
                        pltpu.VMEM((t, tq), F32), pltpu.VMEM((t, tq), jnp.int32),
                        pltpu.VMEM((LANES, LANES), BF16)],
        compiler_params=_cparams(2),
        name="attn_p",
    )(rel_bias, q_b, k_b, v_b, q_idx, misc, misc)


def _attn_s_kernel(pt_ref, bias_ref, qb_ref, qidx_ref, widx_ref, knew_ref, vnew_ref, kidxnew_ref,
                   cidx_hbm, ck_hbm, cv_hbm, o_ref,
                   score_scr, logit_scr, mask_scr, key_scr, acc_scr, btab_scr, tri_scr, idx_buf, k_buf, v_buf, sems,
                   *, layer, t_new, n_pages, gp, top_k):
    b = pl.program_id(0)
    n_grp = n_pages // gp
    past = n_pages * PAGE_SIZE
    n_cols = past + PAGE_SIZE
    near0 = max(0, (past - MAX_DIST) // PAGE_SIZE) * PAGE_SIZE
    n_near = n_cols - near0
    pg_rows = PAGE_SIZE * H_B
    grp_keys = gp * PAGE_SIZE
    k_ahead = min(K_SLOTS - 1, n_grp)

    @pl.when(b == 0)
    def _():
        _init_tri(tri_scr, 1)
        trow = lax.broadcasted_iota(jnp.int32, (t_new, n_near), 0)
        col = near0 + lax.broadcasted_iota(jnp.int32, (t_new, n_near), 1)
        tiles = _rel_bias_tile(past + trow - col, bias_ref)
        for h in range(H_B):
            btab_scr[h] = tiles[h]

    def group_copies(hbm, buf, sem_row, slot, grp, seq=None):
        out = []
        for m in range(gp):
            page = pt_ref[b if seq is None else seq, grp * gp + m]
            dst = buf.at[slot, m] if buf is idx_buf else buf.at[slot, pl.ds(m * pg_rows, pg_rows)]
            out.append(pltpu.make_async_copy(hbm.at[layer, page], dst, sems.at[sem_row, slot]))
        return out

    def start_idx_k(slot, grp, seq=None):
        for n, c in enumerate(group_copies(cidx_hbm, idx_buf, 0, slot, grp, seq)
                              + group_copies(ck_hbm, k_buf, 1, slot, grp, seq)):
            c.start(priority=n % 2)

    def wait_idx_k(slot, grp):
        for c in group_copies(cidx_hbm, idx_buf, 0, slot, grp) + group_copies(ck_hbm, k_buf, 1, slot, grp):
            c.wait()

    def start_v(slot, grp):
        for n, c in enumerate(group_copies(cv_hbm, v_buf, 2, slot, grp)):
            c.start(priority=n % 2)

    def wait_v(slot, grp):
        for c in group_copies(cv_hbm, v_buf, 2, slot, grp):
            c.wait()

    qb = qb_ref[0].astype(BF16)
    qidx = qidx_ref[0].astype(BF16)
    widx = widx_ref[0] * (H_IDX ** -0.5)
    head_rows = lambda h: slice(h * t_new, (h + 1) * t_new)
    head_lanes = lambda h: slice(h * DH_B, (h + 1) * DH_B)

    def scores_of(kidx_t):
        dots = _dot(qidx, kidx_t.astype(BF16)) * (D_IDX ** -0.5)
        contrib = jnp.maximum(dots, 0.0) * widx
        s = contrib[0:t_new]
        for h in range(1, H_IDX):
            s = s + contrib[h * t_new:(h + 1) * t_new]
        return s

    @pl.when(b == 0)
    def _():
        for g in range(k_ahead):
            start_idx_k(g, g)

    def pass1(grp, carry):
        slot = grp % K_SLOTS
        ahead = grp + K_SLOTS - 1

        @pl.when(ahead < n_grp)
        def _():
            start_idx_k(ahead % K_SLOTS, ahead)

        wait_idx_k(slot, grp)
        c0 = pl.multiple_of(grp * grp_keys, grp_keys)
        for m in range(gp):
            score_scr[:, pl.ds(c0 + m * PAGE_SIZE, PAGE_SIZE)] = scores_of(idx_buf[slot, m])
        for h in range(H_B):
            k_h = k_buf[slot, pl.ds(h, grp_keys, stride=H_B), :].astype(BF16)
            logit_scr[head_rows(h), pl.ds(c0, grp_keys)] = _dot_nt(qb[:, head_lanes(h)], k_h)
        return carry

    lax.fori_loop(0, n_grp, pass1, 0)
    for g in range(min(V_SLOTS - 1, n_grp)):
        start_v(g, g)

    score_scr[:, past:n_cols] = scores_of(kidxnew_ref[0])
    for h in range(H_B):
        k_h = knew_ref[0, pl.ds(h, PAGE_SIZE, stride=H_B), :].astype(BF16)
        logit_scr[head_rows(h), past:n_cols] = _dot_nt(qb[:, head_lanes(h)], k_h)

    trow = lax.broadcasted_iota(jnp.int32, (t_new, n_cols), 0)
    col = lax.broadcasted_iota(jnp.int32, (t_new, n_cols), 1)
    admissible = col <= past + trow
    key_scr[...] = _order_key(score_scr[...], admissible)
    _select_topk(key_scr, admissible, top_k, mask_scr, tri_scr, 1)
    for h in range(H_B):
        rs = head_rows(h)
        near = logit_scr[rs, near0:n_cols] * (DH_B ** -0.5) + btab_scr[h] + mask_scr[:, near0:n_cols]
        mx = jnp.max(near, axis=-1, keepdims=True)
        if near0 > 0:
            far = logit_scr[rs, 0:near0] * (DH_B ** -0.5) + bias_ref[N_BUCKETS - 1, h] + mask_scr[:, 0:near0]
            mx = jnp.maximum(mx, jnp.max(far, axis=-1, keepdims=True))
        pn = jnp.exp(near - mx)
        denom = jnp.sum(pn, axis=-1, keepdims=True)
        if near0 > 0:
            pf = jnp.exp(far - mx)
            denom = denom + jnp.sum(pf, axis=-1, keepdims=True)
            logit_scr[rs, 0:near0] = pf / denom
        logit_scr[rs, near0:n_cols] = pn / denom
        v_h = vnew_ref[0, pl.ds(h, PAGE_SIZE, stride=H_B), :].astype(BF16)
        acc_scr[rs, :] = _dot(logit_scr[rs, past:n_cols].astype(BF16), v_h)

    @pl.when(b + 1 < pl.num_programs(0))
    def _():
        for g in range(k_ahead):
            start_idx_k(g, g, b + 1)

    def pass2(grp, carry):
        slot = grp % V_SLOTS
        ahead = grp + V_SLOTS - 1

        @pl.when(ahead < n_grp)
        def _():
            start_v(ahead % V_SLOTS, ahead)

        wait_v(slot, grp)
        c0 = pl.multiple_of(grp * grp_keys, grp_keys)
        for h in range(H_B):
            v_h = v_buf[slot, pl.ds(h, grp_keys, stride=H_B), :].astype(BF16)
            p_h = logit_scr[head_rows(h), pl.ds(c0, grp_keys)].astype(BF16)
            acc_scr[head_rows(h), :] += _dot(p_h, v_h)
        return carry

    lax.fori_loop(0, n_grp, pass2, 0)
    for h in range(H_B):
        o_ref[0, :, head_lanes(h)] = acc_scr[head_rows(h), :]


def _attn_s(page_table, rel_bias, q_b, k_new, v_new, q_idx, kidx_new, misc, cache_k, cache_v, cache_idx_k,
            layer):
    b, t_new, _ = q_b.shape
    n_pages = page_table.shape[1]
    past = n_pages * PAGE_SIZE
    n_cols = past + PAGE_SIZE
    top_k = min(TOPK_MAX, (past + t_new) // 4)
    gp = math.gcd(n_pages, 16)
    hq = H_B * t_new
    near0 = max(0, (past - MAX_DIST) // PAGE_SIZE) * PAGE_SIZE

    qidx = q_idx.reshape(b, t_new, H_IDX, D_IDX).transpose(0, 2, 1, 3).reshape(b, H_IDX * t_new, D_IDX)
    widx = misc[:, :, MISC_WIDX:MISC_WIDX + H_IDX].transpose(0, 2, 1).reshape(b, H_IDX * t_new, 1)
    pad_page = lambda a, r: jnp.pad(a, ((0, 0), (0, (PAGE_SIZE - t_new) * r), (0, 0)))
    k_new, v_new = pad_page(k_new, H_B), pad_page(v_new, H_B)
    kidx_new_t = pad_page(kidx_new, 1).transpose(0, 2, 1)
    ck = cache_k.reshape(cache_k.shape[0], cache_k.shape[1], PAGE_SIZE * H_B, DH_B)
    cv = cache_v.reshape(cache_v.shape[0], cache_v.shape[1], PAGE_SIZE * H_B, DH_B)
    cidx_t = cache_idx_k.transpose(0, 1, 3, 2)

    per_b = lambda s: pl.BlockSpec((1,) + s, lambda bi, pt: (bi, 0, 0))
    hbm = pl.BlockSpec(memory_space=pl.ANY)
    in_specs = [pl.BlockSpec(memory_space=pltpu.SMEM),
                per_b((t_new, HB)), per_b((H_IDX * t_new, D_IDX)), per_b((H_IDX * t_new, 1)),
                per_b((PAGE_SIZE * H_B, DH_B)), per_b((PAGE_SIZE * H_B, DH_B)), per_b((D_IDX, PAGE_SIZE)),
                hbm, hbm, hbm]
    grid_spec = pltpu.PrefetchScalarGridSpec(
        num_scalar_prefetch=1,
        grid=(b,),
        in_specs=in_specs,
        out_specs=pl.BlockSpec((1, t_new, HB), lambda bi, pt: (bi, 0, 0)),
        scratch_shapes=[pltpu.VMEM((t_new, n_cols), F32), pltpu.VMEM((hq, n_cols), F32),
                        pltpu.VMEM((t_new, n_cols), F32), pltpu.VMEM((t_new, n_cols), jnp.int32),
                        pltpu.VMEM((hq, DH_B), F32),
                        pltpu.VMEM((H_B, t_new, n_cols - near0), F32),
                        pltpu.VMEM((LANES, LANES), BF16),
                        pltpu.VMEM((K_SLOTS, gp, D_IDX, PAGE_SIZE), F32),
                        pltpu.VMEM((K_SLOTS, gp * PAGE_SIZE * H_B, DH_B), F32),
                        pltpu.VMEM((V_SLOTS, gp * PAGE_SIZE * H_B, DH_B), F32),
                        pltpu.SemaphoreType.DMA((3, max(K_SLOTS, V_SLOTS)))])
    return pl.pallas_call(
        functools.partial(_attn_s_kernel, layer=layer, t_new=t_new, n_pages=n_pages, gp=gp, top_k=top_k),
        grid_spec=grid_spec,
        out_shape=jax.ShapeDtypeStruct((b, t_new, HB), F32),
        compiler_params=_cparams(1),
        name="attn_s",
    )(page_table, rel_bias, q_b, qidx, widx, k_new, v_new, kidx_new_t, cidx_t, ck, cv)


def _ffn_kernel(x_ref, oa_ref, ob_ref, ga_ref, gb_ref, fst_ref, wpa_ref, wpb_ref, wout_ref, n2_ref,
                wup_ref, cfw_ref, cfb_ref, wdown_ref, fg_ref, y_ref, fnew_ref, prev_scr, act_scr,
                *, tm, ch, d_ff, seq):
    t = pl.program_id(1)
    nt = pl.num_programs(1)
    npr = CONV_F - 1
    if seq is None:
        @pl.when(t == 0)
        def _():
            prev_scr[SUBLANES - npr:SUBLANES, :] = fst_ref[0]

    pa = _dot(oa_ref[0].astype(BF16), wpa_ref[...])
    pb = _dot(ob_ref[0].astype(BF16), wpb_ref[...])
    merged = jax.nn.sigmoid(ga_ref[0]) * pa + jax.nn.sigmoid(gb_ref[0]) * pb
    x1 = x_ref[0] + _dot(merged.astype(BF16), wout_ref[...])
    xn2 = _rms(x1, n2_ref[...]).astype(BF16)

    def conv_cols(c0):
        cols = slice(c0, c0 + ch)
        up = _dot(xn2, wup_ref[:, cols])
        if seq is None:
            prev = prev_scr[:, cols]
            s2, s1 = _shift_rows(up, 2, prev), _shift_rows(up, 1, prev)
            prev_scr[SUBLANES - npr:SUBLANES, cols] = up[tm - npr:tm, :]
        else:
            hist = fst_ref[0, :, cols]
            tpos = lax.broadcasted_iota(jnp.int32, up.shape, 0) & (seq - 1)
            s2 = jnp.where(tpos < 2, hist, pltpu.roll(up, 2, 0))
            s1 = jnp.where(tpos < 1, pltpu.roll(hist, tm - 1, 0), pltpu.roll(up, 1, 0))
            fnew_ref[:, :, cols] = up.reshape(tm // seq, seq, ch)[:, seq - npr:seq, :]
        out = s2 * cfw_ref[0:1, cols]
        out = out + s1 * cfw_ref[1:2, cols]
        out = out + up * cfw_ref[2:3, cols]
        return out + cfb_ref[:, cols]

    for c0 in range(0, d_ff, ch):
        gate = conv_cols(c0)
        val = conv_cols(d_ff + c0)
        act_scr[:, c0:c0 + ch] = (_silu(gate) * val).astype(BF16)
    y_ref[0] = _rms(x1 + _dot(act_scr[...], wdown_ref[...]), fg_ref[...])

    if seq is None:
        @pl.when(t == nt - 1)
        def _():
            fnew_ref[0] = prev_scr[SUBLANES - npr:SUBLANES, :]


def _ffn(x, o_a, o_b, gate_a, gate_b, ffn_state, wpa, wpb, wout, n2_row, wup, cfw, cfb_row, wdown, fg_row, tm):
    b, t, d = x.shape
    d_ff = wdown.shape[0]
    ch = 256
    npr = CONV_F - 1
    consts = (wpa, wpb, wout, n2_row, wup, cfw, cfb_row, wdown, fg_row)
    if t == SUBLANES and t > npr:
        seq, rows = t, b * t
        flat = lambda a: a.reshape(1, rows, a.shape[-1])
        x, o_a, o_b, gate_a, gate_b = map(flat, (x, o_a, o_b, gate_a, gate_b))
        hist = flat(jnp.pad(ffn_state, ((0, 0), (0, t - npr), (0, 0))))
        grid, tm = (1, 1), rows
        blk = lambda w: pl.BlockSpec((1, rows, w), lambda i, j: (0, 0, 0))
        state_in, state_out = blk(2 * d_ff), pl.BlockSpec((b, npr, 2 * d_ff), lambda i, j: (0, 0, 0))
    else:
        seq, hist = None, ffn_state
        grid = (b, t // tm)
        blk = lambda w: pl.BlockSpec((1, tm, w), lambda i, j: (i, j, 0))
        state_in = state_out = pl.BlockSpec((1, npr, 2 * d_ff), lambda i, j: (i, 0, 0))
    y, ffn_new = pl.pallas_call(
        functools.partial(_ffn_kernel, tm=tm, ch=ch, d_ff=d_ff, seq=seq),
        grid=grid,
        in_specs=[blk(d), blk(o_a.shape[-1]), blk(o_b.shape[-1]), blk(d), blk(d), state_in]
                 + [_const_spec(c.shape) for c in consts],
        out_specs=[blk(d), state_out],
        out_shape=[jax.ShapeDtypeStruct(x.shape, F32),
                   jax.ShapeDtypeStruct((b, npr, 2 * d_ff), F32)],
        scratch_shapes=[pltpu.VMEM((SUBLANES, 2 * d_ff), F32), pltpu.VMEM((tm, d_ff), BF16)],
        compiler_params=_cparams(2),
        name="ffn",
    )(x, o_a, o_b, gate_a, gate_b, hist, *consts)
    return y.reshape(b, t, d), ffn_new


def _misc_row(vals, col0):
    return jnp.zeros((1, LANES), F32).at[0, col0:col0 + vals.shape[0]].set(vals.astype(F32))


def _row_tiles(b, t):
    return dict(proj_tm=min(b * t, ROW_TILE), gdn_rows=min(t, ROW_TILE), ffn_tm=min(t, 2 * ROW_TILE))


def _layer(x, conv_state, rec_state, ffn_state, attend, lw, proj_tm, gdn_rows, ffn_tm):
    b, t, d = x.shape
    (w_in_bf, conv_a_w, alog_row, dtb_row, ng_row, wpa, wpb, wout, n1_row, n2_row, wup, cfw, cfb_row,
     wdown, fg_row) = lw
    (qkv_a, z_a, q_b, k_b, v_b, q_idx, gate_a, gate_b, misc, k_idx) = _proj(
        x.reshape(b * t, d), n1_row, w_in_bf, proj_tm)
    r3 = lambda a: a.reshape(b, -1, a.shape[-1])
    qkv_a, z_a, q_b, k_b, v_b, q_idx, gate_a, gate_b, misc, k_idx = map(
        r3, (qkv_a, z_a, q_b, k_b, v_b, q_idx, gate_a, gate_b, misc, k_idx))
    o_a, conv_new, rec_new = _gdn(qkv_a, misc, z_a, conv_state, rec_state, conv_a_w, alog_row, dtb_row,
                                  ng_row, gdn_rows)
    o_b = attend(q_b, k_b, v_b, q_idx, k_idx, misc)
    y, ffn_new = _ffn(x, o_a, o_b, gate_a, gate_b, ffn_state, wpa, wpb, wout, n2_row, wup, cfw, cfb_row,
                      wdown, fg_row, ffn_tm)
    kv_shape = (b, t, H_B, DH_B)
    return y, (k_b.reshape(kv_shape), v_b.reshape(kv_shape), k_idx, conv_new, rec_new, ffn_new)


def kernel(x_prompt, x_sample, cache_k, cache_v, cache_idx_k, state_a_conv, state_a_rec, state_ffn_conv,
           page_table, w_in, conv_a_w, a_log, dt_bias, norm_a_g, w_proj_a, w_proj_b, w_out, rel_bias,
           norm1_g, norm2_g, w_up, conv_f_w, conv_f_b, w_down, final_g):
    depth = w_in.shape[0]
    assert depth == 1, "the final norm is fused into the layer's last kernel"
    bp, tp, d = x_prompt.shape
    bs, ts, _ = x_sample.shape
    d_ff = w_down.shape[1]
    l = 0
    lw = (_prep_w_in(w_in[l], d), conv_a_w[l], _misc_row(a_log[l], MISC_A), _misc_row(dt_bias[l], MISC_A),
          norm_a_g[l].reshape(1, -1), w_proj_a[l].astype(BF16), w_proj_b[l].astype(BF16),
          w_out[l].astype(BF16), norm1_g[l].reshape(1, -1), norm2_g[l].reshape(1, -1),
          w_up[l].astype(BF16), conv_f_w[l], conv_f_b[l].reshape(1, -1), w_down[l].astype(BF16),
          final_g.reshape(1, -1))

    def attend_prompt(q_b, k_b, v_b, q_idx, k_idx, misc):
        return _attn_p(rel_bias, q_b, k_b, v_b, q_idx, misc)

    def attend_sample(q_b, k_b, v_b, q_idx, k_idx, misc):
        return _attn_s(page_table, rel_bias, q_b, k_b, v_b, q_idx, k_idx, misc, cache_k, cache_v,
                       cache_idx_k, l)

    zeros = lambda *s: jnp.zeros(s, x_prompt.dtype)
    y_p, st_p = _layer(x_prompt, zeros(bp, CONV_A - 1, QKV_A), zeros(bp, H_A, DK_A, DV_A),
                       zeros(bp, CONV_F - 1, 2 * d_ff), attend_prompt, lw, **_row_tiles(bp, tp))
    y_s, st_s = _layer(x_sample, state_a_conv[l], state_a_rec[l], state_ffn_conv[l], attend_sample, lw,
                       **_row_tiles(bs, ts))
    stack = lambda a: a[None]
    return (y_p, y_s) + tuple(map(stack, st_p)) + tuple(map(stack, st_s))
```
